```python
import jax
import jax.numpy as jnp
from jax import lax
import numpy as np

D_MODEL = 2048
BATCH = 2
SEQ = 4096
DEPTH = 1
DEC_BATCH = 128
DEC_SEQ = 1
PAST_LEN = 8192
PAGE_SIZE = 128

MLA_HEADS = 16
QK_NOPE = 128
QK_ROPE = 64
QK_HEAD = QK_NOPE + QK_ROPE
V_HEAD = 128
KV_LORA = 512
ROPE_THETA = 10000.0
Q_BLOCK = 128

GDN_HEADS = 16
GDN_DK = 128
GDN_DV = 128
GDN_QKV_W = GDN_HEADS * (2 * GDN_DK + GDN_DV)
CONV_W = 4
CHUNK = 64

N_GROUPS = 8
EXP_PER_GROUP = 8
N_EXPERTS = N_GROUPS * EXP_PER_GROUP
TOP_K = 2
D_EXPERT = 1408
MOE_BLOCK = 128

EPS = 1e-6
IN_SIZES = (MLA_HEADS * QK_HEAD, KV_LORA, QK_ROPE, GDN_QKV_W, GDN_HEADS, GDN_HEADS,
            GDN_HEADS * GDN_DV, D_MODEL, D_MODEL)
D_IN = sum(IN_SIZES)

kernel_name = 'hybrid_mla_gdn_hmoe_step'


def rmsnorm(x, g):
    xf = x.astype(jnp.float32)
    y = xf * lax.rsqrt(jnp.mean(xf * xf, axis=-1, keepdims=True) + EPS)
    return (y * g.astype(jnp.float32)).astype(x.dtype)


def l2norm(x):
    xf = x.astype(jnp.float32)
    return xf * lax.rsqrt(jnp.sum(xf * xf, axis=-1, keepdims=True) + EPS)


def rope_tables(pos):
    inv = ROPE_THETA ** (-jnp.arange(0, QK_ROPE, 2, dtype=jnp.float32) / QK_ROPE)
    ang = pos[:, None] * inv[None, :]
    return jnp.cos(ang), jnp.sin(ang)


def apply_rope(x, cos, sin):
    half = x.shape[-1] // 2
    x1, x2 = x[..., :half], x[..., half:]
    cos = cos.astype(x.dtype)
    sin = sin.astype(x.dtype)
    return jnp.concatenate([x1 * cos - x2 * sin, x2 * cos + x1 * sin], axis=-1)


def mixer_inputs(x, norm_g, w_in):
    proj = rmsnorm(x, norm_g) @ w_in
    return jnp.split(proj, np.cumsum(IN_SIZES)[:-1].tolist(), axis=-1)


def mla_prep(q_raw, ckv_raw, kr_raw, pos, norm_ckv, q_gain_nope, q_gain_rope, k_gain_rope):
    b, t = q_raw.shape[:2]
    q = q_raw.reshape(b, t, MLA_HEADS, QK_HEAD)
    cos, sin = rope_tables(pos)
    q_nope = rmsnorm(q[..., :QK_NOPE], q_gain_nope)
    q_rope = apply_rope(rmsnorm(q[..., QK_NOPE:], q_gain_rope), cos[:, None, :], sin[:, None, :])
    c = rmsnorm(ckv_raw, norm_ckv)
    k_rope = apply_rope(rmsnorm(kr_raw, k_gain_rope), cos, sin)
    return q_nope, q_rope, c, k_rope


def mla_prompt_attention(q_nope, q_rope, c, k_rope, w_uk, w_uv, k_gain_nope):
    b, t = c.shape[:2]
    k_nope = rmsnorm(jnp.einsum('btc,chd->bthd', c, w_uk.reshape(KV_LORA, MLA_HEADS, QK_NOPE)), k_gain_nope)
    v = jnp.einsum('btc,chd->bthd', c, w_uv.reshape(KV_LORA, MLA_HEADS, V_HEAD))
    n_blocks = t // Q_BLOCK
    qn_blocks = jnp.moveaxis(q_nope.reshape(b, n_blocks, Q_BLOCK, MLA_HEADS, QK_NOPE), 1, 0)
    qr_blocks = jnp.moveaxis(q_rope.reshape(b, n_blocks, Q_BLOCK, MLA_HEADS, QK_ROPE), 1, 0)
    k_pos = jnp.arange(t)
    scale = QK_HEAD ** -0.5

    def block(args):
        qn, qr, i = args
        s = jnp.einsum('bqhd,bkhd->bhqk', qn, k_nope) + jnp.einsum('bqhd,bkd->bhqk', qr, k_rope)
        q_pos = i * Q_BLOCK + jnp.arange(Q_BLOCK)
        mask = k_pos[None, :] <= q_pos[:, None]
        s = jnp.where(mask, s.astype(jnp.float32) * scale, -jnp.inf)
        p = jax.nn.softmax(s, axis=-1).astype(v.dtype)
        return jnp.einsum('bhqk,bkhd->bqhd', p, v)

    o = lax.map(block, (qn_blocks, qr_blocks, jnp.arange(n_blocks)))
    return jnp.moveaxis(o, 0, 1).reshape(b, t, MLA_HEADS * V_HEAD)


def mla_sample_attention(q_nope, q_rope, c_new, kr_new, ckv_pool, kr_pool, page_table,
                         w_uk, w_uv, k_gain_nope):
    n_past = page_table.shape[1] * PAGE_SIZE
    s_new = q_nope.shape[1]
    w_uk_h = w_uk.reshape(KV_LORA, MLA_HEADS, QK_NOPE)
    w_uv_h = w_uv.reshape(KV_LORA, MLA_HEADS, V_HEAD)
    scale = QK_HEAD ** -0.5
    q_idx = jnp.arange(s_new)[:, None]
    k_idx = jnp.arange(n_past + s_new)[None, :]
    mask = (k_idx < n_past) | (k_idx - n_past <= q_idx)

    def one(args):
        qn, qr, cn, krn, pt = args
        c_all = jnp.concatenate([ckv_pool[pt].reshape(n_past, KV_LORA), cn.astype(ckv_pool.dtype)], axis=0)
        kr_all = jnp.concatenate([kr_pool[pt].reshape(n_past, QK_ROPE), krn.astype(kr_pool.dtype)], axis=0)
        k_nope = rmsnorm(jnp.einsum('lc,chd->lhd', c_all, w_uk_h), k_gain_nope)
        s = jnp.einsum('qhd,lhd->hql', qn, k_nope) + jnp.einsum('qhd,ld->hql', qr, kr_all)
        s = jnp.where(mask[None], s.astype(jnp.float32) * scale, -jnp.inf)
        p = jax.nn.softmax(s, axis=-1).astype(c_all.dtype)
        o_lat = jnp.einsum('hql,lc->qhc', p, c_all)
        return jnp.einsum('qhc,chd->qhd', o_lat, w_uv_h)

    o = lax.map(one, (q_nope, q_rope, c_new, kr_new, page_table))
    return o.reshape(q_nope.shape[0], s_new, MLA_HEADS * V_HEAD)


def gdn_prep(qkv_raw, conv_buf, a, b_raw, conv_w, a_log, dt_bias):
    bsz, t = qkv_raw.shape[:2]
    xpad = jnp.concatenate([conv_buf.astype(qkv_raw.dtype), qkv_raw], axis=1)
    acc = xpad[:, 0:t] * conv_w[0]
    for w in range(1, CONV_W):
        acc = acc + xpad[:, w:w + t] * conv_w[w]
    qkv = jax.nn.silu(acc)
    new_buf = xpad[:, t:]
    hk = GDN_HEADS * GDN_DK
    q = l2norm(qkv[..., :hk].reshape(bsz, t, GDN_HEADS, GDN_DK)) * (GDN_DK ** -0.5)
    k = l2norm(qkv[..., hk:2 * hk].reshape(bsz, t, GDN_HEADS, GDN_DK))
    v = qkv[..., 2 * hk:].reshape(bsz, t, GDN_HEADS, GDN_DV).astype(jnp.float32)
    g = -jnp.exp(a_log.astype(jnp.float32)) * jax.nn.softplus(a.astype(jnp.float32) + dt_bias.astype(jnp.float32))
    beta = jax.nn.sigmoid(b_raw.astype(jnp.float32))
    return q, k, v, g, beta, new_buf


def gated_delta_rule(q, k, v, g, beta, s0):
    t = q.shape[1]
    c = min(CHUNK, t)
    n_chunks = -(-t // c)
    pad = n_chunks * c - t

    def to_chunks(x):
        x = jnp.pad(x, [(0, 0), (0, pad)] + [(0, 0)] * (x.ndim - 2))
        x = jnp.moveaxis(x, 1, 2)
        return x.reshape(x.shape[:2] + (n_chunks, c) + x.shape[3:])

    q, k, v, g, beta = to_chunks(q), to_chunks(k), to_chunks(v), to_chunks(g), to_chunks(beta)
    gc = jnp.cumsum(g, axis=-1)
    kb = k * beta[..., None]
    vb = v * beta[..., None]
    idx = jnp.arange(c)
    strict = idx[:, None] > idx[None, :]
    incl = idx[:, None] >= idx[None, :]
    diff = gc[..., :, None] - gc[..., None, :]
    dec_s = jnp.where(strict, jnp.exp(jnp.where(strict, diff, 0.0)), 0.0)
    dec_i = jnp.where(incl, jnp.exp(jnp.where(incl, diff, 0.0)), 0.0)
    m = jnp.einsum('bhncd,bhnkd->bhnck', kb, k) * dec_s
    rhs = jnp.concatenate([vb, kb * jnp.exp(gc)[..., None]], axis=-1)
    sol = lax.linalg.triangular_solve(jnp.eye(c, dtype=jnp.float32) + m, rhs,
                                      left_side=True, lower=True, unit_diagonal=True)
    dv = v.shape[-1]
    value, k_cumdecay = sol[..., :dv], sol[..., dv:]
    attn = jnp.einsum('bhncd,bhnkd->bhnck', q, k) * dec_i
    q_dec = q * jnp.exp(gc)[..., None]
    k_dec = k * jnp.exp(gc[..., -1:] - gc)[..., None]
    g_last = jnp.exp(gc[..., -1])
    xs = tuple(jnp.moveaxis(a, 2, 0) for a in (value, k_cumdecay, attn, q_dec, k_dec, g_last))

    def step(s, inp):
        val, kcd, at, qd, kd, gl = inp
        v_new = val - jnp.einsum('bhck,bhkv->bhcv', kcd, s)
        o = jnp.einsum('bhck,bhkv->bhcv', qd, s) + jnp.einsum('bhcj,bhjv->bhcv', at, v_new)
        s = s * gl[..., None, None] + jnp.einsum('bhck,bhcv->bhkv', kd, v_new)
        return s, o

    s_final, o = lax.scan(step, s0, xs)
    o = jnp.moveaxis(o, 0, 2)
    o = o.reshape(o.shape[:2] + (n_chunks * c, dv))[:, :, :t]
    return jnp.moveaxis(o, 1, 2), s_final


def gdn_branch(qkv_raw, a, b_raw, z, conv_buf, s0, conv_w, a_log, dt_bias, gdn_norm):
    q, k, v, g, beta, new_buf = gdn_prep(qkv_raw, conv_buf, a, b_raw, conv_w, a_log, dt_bias)
    o, s_final = gated_delta_rule(q, k, v, g, beta, s0.astype(jnp.float32))
    bsz, t = z.shape[:2]
    zf = z.reshape(bsz, t, GDN_HEADS, GDN_DV).astype(jnp.float32)
    o = rmsnorm(o, gdn_norm) * jax.nn.silu(zf)
    return o.reshape(bsz, t, GDN_HEADS * GDN_DV).astype(z.dtype), s_final, new_buf


def gated_merge(o_a, o_b, gate_a, gate_b, w_pa, w_pb, w_o):
    m = jax.nn.sigmoid(gate_a) * (o_a @ w_pa) + jax.nn.sigmoid(gate_b) * (o_b @ w_pb)
    return m @ w_o


def hier_moe(h, w_rg, b_rg, w_re, b_re, w_gate, w_up, w_down):
    shape = h.shape
    h = h.reshape(-1, shape[-1])
    t = h.shape[0]
    g_logits = (h @ w_rg).astype(jnp.float32) + b_rg.astype(jnp.float32)
    p_group = jax.nn.softmax(g_logits, axis=-1)
    grp = jnp.argmax(g_logits, axis=-1).astype(jnp.int32)
    p_grp_sel = jnp.take_along_axis(p_group, grp[:, None], axis=-1)[:, 0]
    e_logits = ((h @ w_re).astype(jnp.float32) + b_re.astype(jnp.float32)).reshape(t, N_GROUPS, EXP_PER_GROUP)
    e_sel = jnp.take_along_axis(e_logits, grp[:, None, None], axis=1)[:, 0]
    top_w, top_i = lax.top_k(jax.nn.softmax(e_sel, axis=-1), TOP_K)
    top_w = top_w / jnp.sum(top_w, axis=-1, keepdims=True) * p_grp_sel[:, None]
    expert = grp[:, None] * EXP_PER_GROUP + top_i.astype(jnp.int32)
    n_assign = t * TOP_K
    e_flat = expert.reshape(n_assign)
    w_flat = top_w.reshape(n_assign)
    tok = jnp.arange(n_assign) // TOP_K
    order = jnp.argsort(e_flat)
    se, stok, sw = e_flat[order], tok[order], w_flat[order]
    counts = jax.ops.segment_sum(jnp.ones((n_assign,), jnp.int32), se, num_segments=N_EXPERTS)
    pcounts = (counts + MOE_BLOCK - 1) // MOE_BLOCK * MOE_BLOCK
    starts = jnp.cumsum(counts) - counts
    pends = jnp.cumsum(pcounts)
    pstarts = pends - pcounts
    dest = pstarts[se] + jnp.arange(n_assign) - starts[se]
    n_blocks = -(-n_assign // MOE_BLOCK) + N_EXPERTS
    xb = jnp.zeros((n_blocks * MOE_BLOCK, h.shape[1]), h.dtype).at[dest].set(h[stok])
    bstart = jnp.arange(n_blocks) * MOE_BLOCK
    bexp = jnp.minimum(jnp.searchsorted(pends, bstart, side='right'), N_EXPERTS - 1)
    used = bstart < pends[-1]

    def run(args):
        xblk, e, u = args

        def expert_mlp(xx):
            return ((jax.nn.silu(xx @ w_gate[e]) * (xx @ w_up[e])) @ w_down[e]).astype(xx.dtype)

        return lax.cond(u, expert_mlp, jnp.zeros_like, xblk)

    yb = lax.map(run, (xb.reshape(n_blocks, MOE_BLOCK, -1), bexp, used)).reshape(n_blocks * MOE_BLOCK, -1)
    y = jnp.zeros_like(h).at[stok].add(yb[dest] * sw[:, None].astype(h.dtype))
    return y.reshape(shape)


def setup_inputs(seed: int = 0) -> dict:
    key = jax.random.key(seed)
    ks = jax.random.split(key, 32)
    f32 = jnp.float32
    n_pages = PAST_LEN // PAGE_SIZE
    n_pool = (DEC_BATCH * n_pages * 5) // 4

    def nrm(k, shape, scale):
        return jax.random.normal(k, shape, f32) * scale

    def gain(k, n):
        return 1.0 + 0.1 * jax.random.normal(k, (DEPTH, n), f32)

    page_table = jax.random.permutation(ks[6], n_pool)[:DEC_BATCH * n_pages].reshape(DEC_BATCH, n_pages).astype(jnp.int32)
    return {
        'x_prompt': nrm(ks[0], (BATCH, SEQ, D_MODEL), 1.0),
        'x_sample': nrm(ks[1], (DEC_BATCH, DEC_SEQ, D_MODEL), 1.0),
        'cache_ckv': nrm(ks[2], (DEPTH, n_pool, PAGE_SIZE, KV_LORA), 1.0),
        'cache_krope': nrm(ks[3], (DEPTH, n_pool, PAGE_SIZE, QK_ROPE), 1.0),
        'state_ssm': nrm(ks[4], (DEPTH, DEC_BATCH, GDN_HEADS, GDN_DK, GDN_DV), 0.1),
        'state_conv': nrm(ks[5], (DEPTH, DEC_BATCH, CONV_W - 1, GDN_QKV_W), 1.0),
        'page_table': page_table,
        'norm_attn': gain(ks[7], D_MODEL),
        'w_in': nrm(ks[8], (DEPTH, D_MODEL, D_IN), D_MODEL ** -0.5),
        'norm_ckv': gain(ks[9], KV_LORA),
        'w_uk': nrm(ks[10], (DEPTH, KV_LORA, MLA_HEADS * QK_NOPE), KV_LORA ** -0.5),
        'w_uv': nrm(ks[11], (DEPTH, KV_LORA, MLA_HEADS * V_HEAD), KV_LORA ** -0.5),
        'q_gain_nope': gain(ks[12], QK_NOPE),
        'q_gain_rope': gain(ks[13], QK_ROPE),
        'k_gain_nope': gain(ks[14], QK_NOPE),
        'k_gain_rope': gain(ks[15], QK_ROPE),
        'conv_w': nrm(ks[16], (DEPTH, CONV_W, GDN_QKV_W), 0.5),
        'gdn_a_log': jnp.log(jax.random.uniform(ks[17], (DEPTH, GDN_HEADS), f32, 1.0, 16.0)),
        'gdn_dt_bias': nrm(ks[18], (DEPTH, GDN_HEADS), 0.5),
        'gdn_norm': gain(ks[19], GDN_DV),
        'w_pa': nrm(ks[20], (DEPTH, MLA_HEADS * V_HEAD, D_MODEL), (MLA_HEADS * V_HEAD) ** -0.5),
        'w_pb': nrm(ks[21], (DEPTH, GDN_HEADS * GDN_DV, D_MODEL), (GDN_HEADS * GDN_DV) ** -0.5),
        'w_o': nrm(ks[22], (DEPTH, D_MODEL, D_MODEL), D_MODEL ** -0.5),
        'norm_ffn': gain(ks[23], D_MODEL),
        'w_router_group': nrm(ks[24], (DEPTH, D_MODEL, N_GROUPS), D_MODEL ** -0.5),
        'b_router_group': nrm(ks[25], (DEPTH, N_GROUPS), 0.01),
        'w_router_expert': nrm(ks[26], (DEPTH, D_MODEL, N_EXPERTS), D_MODEL ** -0.5),
        'b_router_expert': nrm(ks[27], (DEPTH, N_EXPERTS), 0.01),
        'w_gate': nrm(ks[28], (DEPTH, N_EXPERTS, D_MODEL, D_EXPERT), D_MODEL ** -0.5),
        'w_up': nrm(ks[29], (DEPTH, N_EXPERTS, D_MODEL, D_EXPERT), D_MODEL ** -0.5),
        'w_down': nrm(ks[30], (DEPTH, N_EXPERTS, D_EXPERT, D_MODEL), D_EXPERT ** -0.5),
    }


def reference(x_prompt, x_sample, cache_ckv, cache_krope, state_ssm, state_conv, page_table,
              norm_attn, w_in, norm_ckv, w_uk, w_uv, q_gain_nope, q_gain_rope, k_gain_nope, k_gain_rope,
              conv_w, gdn_a_log, gdn_dt_bias, gdn_norm, w_pa, w_pb, w_o,
              norm_ffn, w_router_group, b_router_group, w_router_expert, b_router_expert,
              w_gate, w_up, w_down):
    xp, xs = x_prompt, x_sample
    n_past = page_table.shape[1] * PAGE_SIZE
    pos_p = jnp.arange(xp.shape[1], dtype=jnp.float32)
    pos_s = n_past + jnp.arange(xs.shape[1], dtype=jnp.float32)
    ckv_p_l, kr_p_l, ckv_s_l, kr_s_l = [], [], [], []
    ssm_p_l, conv_p_l, ssm_s_l, conv_s_l = [], [], [], []
    for l in range(DEPTH):
        qp, ckvp, krp, qkvp, ap, bp, zp, gap, gbp = mixer_inputs(xp, norm_attn[l], w_in[l])
        qs, ckvs, krs, qkvs, as_, bs, zs, gas, gbs = mixer_inputs(xs, norm_attn[l], w_in[l])
        qn_p, qr_p, c_p, kr_p = mla_prep(qp, ckvp, krp, pos_p, norm_ckv[l], q_gain_nope[l], q_gain_rope[l], k_gain_rope[l])
        oa_p = mla_prompt_attention(qn_p, qr_p, c_p, kr_p, w_uk[l], w_uv[l], k_gain_nope[l])
        qn_s, qr_s, c_s, kr_s = mla_prep(qs, ckvs, krs, pos_s, norm_ckv[l], q_gain_nope[l], q_gain_rope[l], k_gain_rope[l])
        oa_s = mla_sample_attention(qn_s, qr_s, c_s, kr_s, cache_ckv[l], cache_krope[l], page_table,
                                    w_uk[l], w_uv[l], k_gain_nope[l])
        s0 = jnp.zeros((xp.shape[0], GDN_HEADS, GDN_DK, GDN_DV), jnp.float32)
        buf0 = jnp.zeros((xp.shape[0], CONV_W - 1, GDN_QKV_W), xp.dtype)
        ob_p, ssm_p, conv_p = gdn_branch(qkvp, ap, bp, zp, buf0, s0, conv_w[l], gdn_a_log[l], gdn_dt_bias[l], gdn_norm[l])
        ob_s, ssm_s, conv_s = gdn_branch(qkvs, as_, bs, zs, state_conv[l], state_ssm[l], conv_w[l],
                                         gdn_a_log[l], gdn_dt_bias[l], gdn_norm[l])
        hp = xp + gated_merge(oa_p, ob_p, gap, gbp, w_pa[l], w_pb[l], w_o[l])
        hs = xs + gated_merge(oa_s, ob_s, gas, gbs, w_pa[l], w_pb[l], w_o[l])
        xp = hp + hier_moe(rmsnorm(hp, norm_ffn[l]), w_router_group[l], b_router_group[l], w_router_expert[l],
                           b_router_expert[l], w_gate[l], w_up[l], w_down[l])
        xs = hs + hier_moe(rmsnorm(hs, norm_ffn[l]), w_router_group[l], b_router_group[l], w_router_expert[l],
                           b_router_expert[l], w_gate[l], w_up[l], w_down[l])
        ckv_p_l.append(c_p)
        kr_p_l.append(kr_p)
        ckv_s_l.append(c_s.astype(cache_ckv.dtype))
        kr_s_l.append(kr_s.astype(cache_krope.dtype))
        ssm_p_l.append(ssm_p.astype(state_ssm.dtype))
        conv_p_l.append(conv_p.astype(state_conv.dtype))
        ssm_s_l.append(ssm_s.astype(state_ssm.dtype))
        conv_s_l.append(conv_s.astype(state_conv.dtype))
    return (xp, xs, jnp.stack(ckv_p_l), jnp.stack(kr_p_l), jnp.stack(ckv_s_l), jnp.stack(kr_s_l),
            jnp.stack(ssm_p_l), jnp.stack(conv_p_l), jnp.stack(ssm_s_l), jnp.stack(conv_s_l))
```

```python
import functools

import jax
import jax.numpy as jnp
from jax import lax
from jax.experimental import pallas as pl
from jax.experimental.pallas import tpu as pltpu

F32 = jnp.float32
BF16 = jnp.bfloat16

D_MODEL = 2048
PAGE_SIZE = 128
MLA_HEADS = 16
QK_NOPE = 128
QK_ROPE = 64
QK_HEAD = QK_NOPE + QK_ROPE
QK_PAD = 256
V_HEAD = 128
KV_LORA = 512
ROPE_THETA = 10000.0
GDN_HEADS = 16
GDN_DK = 128
GDN_DV = 128
GDN_QKV_W = GDN_HEADS * (2 * GDN_DK + GDN_DV)
CONV_W = 4
GDN_CHUNK = 128
N_GROUPS = 8
EXP_PER_GROUP = 8
N_EXPERTS = N_GROUPS * EXP_PER_GROUP
TOP_K = 2
D_EXPERT = 1408
MOE_ROWS = 128
EPS = 1e-6

V7X_VMEM_LIMIT = 56 * 1024 * 1024

SEG_QN = 0
SEG_QR = SEG_QN + MLA_HEADS * QK_NOPE
SEG_CKV = SEG_QR + MLA_HEADS * QK_ROPE
SEG_MISC = SEG_CKV + KV_LORA
SEG_QKV = SEG_MISC + 128
SEG_Z = SEG_QKV + GDN_QKV_W
SEG_GA = SEG_Z + GDN_HEADS * GDN_DV
SEG_GB = SEG_GA + D_MODEL
D_PROJ = SEG_GB + D_MODEL


def _cparams(sem, vmem=V7X_VMEM_LIMIT):
    return pltpu.CompilerParams(dimension_semantics=sem, vmem_limit_bytes=vmem)


def _mm_kernel(a_ref, b_ref, o_ref):
    o_ref[...] = jnp.dot(a_ref[...], b_ref[...], preferred_element_type=F32).astype(o_ref.dtype)


def _tile(dim, cap, align):
    if dim <= cap:
        return dim
    for t in range(cap - cap % align, 0, -align):
        if dim % t == 0:
            return t
    raise ValueError(f"no {align}-aligned tile <= {cap} divides {dim}")


def _matmul(a, b, *, tm, tn, out_dtype, name):
    m, k = a.shape
    _, n = b.shape
    tm = _tile(m, tm, 16)
    tn = _tile(n, tn, 128)
    return pl.pallas_call(
        _mm_kernel,
        grid=(m // tm, n // tn),
        in_specs=[pl.BlockSpec((tm, k), lambda i, j: (i, 0)),
                  pl.BlockSpec((k, tn), lambda i, j: (0, j))],
        out_specs=pl.BlockSpec((tm, tn), lambda i, j: (i, j)),
        out_shape=jax.ShapeDtypeStruct((m, n), out_dtype),
        compiler_params=_cparams(("arbitrary", "arbitrary")),
        name=name,
    )(a, b)


def _headwise_matmul(a, b, *, heads, b_rows_by_head, name):
    m = a.shape[0]
    k = a.shape[1] // heads
    if b_rows_by_head:
        n = b.shape[1]
        b_spec = pl.BlockSpec((k, n), lambda h: (h, 0))
    else:
        n = b.shape[1] // heads
        b_spec = pl.BlockSpec((k, n), lambda h: (0, h))
    return pl.pallas_call(
        _mm_kernel,
        grid=(heads,),
        in_specs=[pl.BlockSpec((m, k), lambda h: (0, h)), b_spec],
        out_specs=pl.BlockSpec((m, n), lambda h: (0, h)),
        out_shape=jax.ShapeDtypeStruct((m, heads * n), F32),
        compiler_params=_cparams(("arbitrary",)),
        name=name,
    )(a, b)


def _flash_kernel(q_ref, k_ref, v_ref, o_ref, m_ref, l_ref, acc_ref, *, tq, scale):
    qi = pl.program_id(2)
    q = q_ref[...]
    m_ref[...] = jnp.full(m_ref.shape, -jnp.inf, F32)
    l_ref[...] = jnp.zeros(l_ref.shape, F32)
    acc_ref[...] = jnp.zeros(acc_ref.shape, F32)

    def block(ki, masked):
        start = pl.multiple_of(ki * tq, tq)
        k = k_ref[pl.ds(start, tq), :]
        v = v_ref[pl.ds(start, tq), :]
        s = lax.dot_general(q, k, (((1,), (1,)), ((), ())), preferred_element_type=F32) * scale
        if masked:
            row = lax.broadcasted_iota(jnp.int32, s.shape, 0)
            col = lax.broadcasted_iota(jnp.int32, s.shape, 1)
            s = jnp.where(col <= row, s, -jnp.inf)
        m_old = m_ref[...]
        m_new = jnp.maximum(m_old, jnp.max(s, axis=-1, keepdims=True))
        alpha = jnp.exp(m_old - m_new)
        p = jnp.exp(s - m_new)
        l_ref[...] = l_ref[...] * alpha + jnp.sum(p, axis=-1, keepdims=True)
        acc_ref[...] = acc_ref[...] * alpha + jnp.dot(p.astype(BF16), v, preferred_element_type=F32)
        m_ref[...] = m_new

    def body(ki, carry):
        block(ki, False)
        return carry

    lax.fori_loop(0, qi, body, 0)
    block(qi, True)
    o_ref[...] = (acc_ref[...] / l_ref[...]).astype(o_ref.dtype)


def _flash_attention(q_cat, k_cat, v, *, batch, seq, tq):
    tq = min(tq, seq)
    nq = seq // tq
    kern = functools.partial(_flash_kernel, tq=tq, scale=QK_HEAD ** -0.5)
    return pl.pallas_call(
        kern,
        grid=(batch, MLA_HEADS, nq),
        in_specs=[pl.BlockSpec((tq, QK_PAD), lambda b, h, i: (b * nq + i, h)),
                  pl.BlockSpec((seq, QK_PAD), lambda b, h, i: (b, h)),
                  pl.BlockSpec((seq, V_HEAD), lambda b, h, i: (b, h))],
        out_specs=pl.BlockSpec((tq, V_HEAD), lambda b, h, i: (b * nq + i, h)),
        out_shape=jax.ShapeDtypeStruct((batch * seq, MLA_HEADS * V_HEAD), BF16),
        scratch_shapes=[pltpu.VMEM((tq, 1), F32), pltpu.VMEM((tq, 1), F32), pltpu.VMEM((tq, V_HEAD), F32)],
        compiler_params=_cparams(("arbitrary", "arbitrary", "arbitrary")),
        name="mla_prompt_attention",
    )(q_cat, k_cat, v)


PAGES_PER_STEP = 8


def _paged_kernel(pt_ref, *refs, scale):
    npg = PAGES_PER_STEP
    c_refs = refs[:npg]
    kr_refs = refs[npg:2 * npg]
    (wukt_ref, qa_ref, qr_ref, qn_ref, knn_ref, krn_ref, cn_ref, o_ref,
     lhs_ref, c_scr, kr_scr, m_ref, l_ref, acc_ref) = refs[2 * npg:]
    b = pl.program_id(0)
    p = pl.program_id(1)
    n_hd = MLA_HEADS * QK_NOPE

    @pl.when((b == 0) & (p == 0))
    def _():
        lhs_ref[0:n_hd, :] = wukt_ref[...]

    @pl.when(p == 0)
    def _():
        lhs_ref[n_hd:n_hd + MLA_HEADS, :] = qa_ref[...]
        m_ref[...] = jnp.full(m_ref.shape, -jnp.inf, F32)
        l_ref[...] = jnp.zeros(l_ref.shape, F32)
        acc_ref[...] = jnp.zeros(acc_ref.shape, F32)

    for i in range(npg):
        c_scr[i * PAGE_SIZE:(i + 1) * PAGE_SIZE, :] = c_refs[i][...].astype(BF16)
        kr_scr[i * PAGE_SIZE:(i + 1) * PAGE_SIZE, :] = kr_refs[i][...].astype(BF16)
    c = c_scr[...]
    res = lax.dot_general(lhs_ref[...], c, (((1,), (1,)), ((), ())), preferred_element_type=F32)
    kt = res[0:n_hd, :].reshape(MLA_HEADS, QK_NOPE, c.shape[0])
    inv = lax.rsqrt(jnp.sum(kt * kt, axis=1) * (1.0 / QK_NOPE) + EPS)
    s_rope = lax.dot_general(qr_ref[...], kr_scr[...], (((1,), (1,)), ((), ())), preferred_element_type=F32)
    s = (res[n_hd:n_hd + MLA_HEADS, :] * inv + s_rope) * scale
    m_old = m_ref[...]
    m_new = jnp.maximum(m_old, jnp.max(s, axis=-1, keepdims=True))
    alpha = jnp.exp(m_old - m_new)
    pr = jnp.exp(s - m_new)
    l_ref[...] = l_ref[...] * alpha + jnp.sum(pr, axis=-1, keepdims=True)
    acc_ref[...] = acc_ref[...] * alpha + jnp.dot(pr.astype(BF16), c, preferred_element_type=F32)
    m_ref[...] = m_new

    @pl.when(p == pl.num_programs(1) - 1)
    def _():
        qn = qn_ref[...].astype(BF16).astype(F32)
        kn = knn_ref[...].astype(BF16).astype(F32)
        qr = qr_ref[...].astype(F32)
        krn = krn_ref[...].astype(BF16).astype(F32)
        s_new = (jnp.sum(qn * kn, axis=-1, keepdims=True) + jnp.sum(qr * krn, axis=-1, keepdims=True)) * scale
        m_old2 = m_ref[...]
        m_fin = jnp.maximum(m_old2, s_new)
        a2 = jnp.exp(m_old2 - m_fin)
        p_new = jnp.exp(s_new - m_fin)
        l_fin = l_ref[...] * a2 + p_new
        cn = cn_ref[...].astype(BF16).astype(F32)
        o_ref[...] = (acc_ref[...] * a2 + p_new * cn) / l_fin


def _paged_attention(page_table, ckv_pool, kr_pool, wukt, qa, qr, qn, knn, krn, cn):
    n_s, n_pages = page_table.shape
    npg = PAGES_PER_STEP
    assert n_pages % npg == 0
    steps = n_pages // npg
    span = npg * PAGE_SIZE

    def page_spec(width, i):
        return pl.BlockSpec((None, PAGE_SIZE, width), lambda b, p, pt, i=i: (pt[b, p * npg + i], 0, 0))

    def per_sample(shape):
        return pl.BlockSpec((None,) + shape, lambda b, p, pt: (b, 0, 0))

    in_specs = ([page_spec(KV_LORA, i) for i in range(npg)] + [page_spec(QK_ROPE, i) for i in range(npg)] + [
        pl.BlockSpec(wukt.shape, lambda b, p, pt: (0, 0)),
        per_sample((MLA_HEADS, KV_LORA)), per_sample((MLA_HEADS, QK_ROPE)),
        per_sample((MLA_HEADS, QK_NOPE)), per_sample((MLA_HEADS, QK_NOPE)),
        per_sample((1, QK_ROPE)), per_sample((1, KV_LORA))])
    grid_spec = pltpu.PrefetchScalarGridSpec(
        num_scalar_prefetch=1,
        grid=(n_s, steps),
        in_specs=in_specs,
        out_specs=pl.BlockSpec((None, MLA_HEADS, KV_LORA), lambda b, p, pt: (b, 0, 0)),
        scratch_shapes=[pltpu.VMEM((MLA_HEADS * QK_NOPE + MLA_HEADS, KV_LORA), BF16),
                        pltpu.VMEM((span, KV_LORA), BF16), pltpu.VMEM((span, QK_ROPE), BF16),
                        pltpu.VMEM((MLA_HEADS, 1), F32), pltpu.VMEM((MLA_HEADS, 1), F32),
                        pltpu.VMEM((MLA_HEADS, KV_LORA), F32)])
    return pl.pallas_call(
        functools.partial(_paged_kernel, scale=QK_HEAD ** -0.5),
        grid_spec=grid_spec,
        out_shape=jax.ShapeDtypeStruct((n_s, MLA_HEADS, KV_LORA), F32),
        compiler_params=_cparams(("arbitrary", "arbitrary")),
        name="mla_sample_attention",
    )(page_table, *([ckv_pool] * npg), *([kr_pool] * npg), wukt, qa, qr, qn, knn, krn, cn)


def _split3(x):
    hi = x.astype(BF16)
    r1 = x - hi.astype(F32)
    mid = r1.astype(BF16)
    lo = (r1 - mid.astype(F32)).astype(BF16)
    return hi, mid, lo


def _delta_kernel(q_ref, k_ref, v_ref, kt_ref, g_ref, gt_ref, beta_ref, o_ref, s_out_ref, s_ref):
    c = GDN_CHUNK
    n = pl.program_id(1)

    @pl.when(n == 0)
    def _():
        s_ref[...] = jnp.zeros(s_ref.shape, F32)

    row = lax.broadcasted_iota(jnp.int32, (c, c), 0)
    col = lax.broadcasted_iota(jnp.int32, (c, c), 1)
    incl = row >= col
    strict = row > col
    eye = (row == col).astype(F32)
    tril = incl.astype(BF16)
    triu = (row <= col).astype(BF16)
    g = g_ref[...]
    gt = gt_ref[...]
    gc = sum(jnp.dot(tril, part, preferred_element_type=F32) for part in _split3(g))
    gct = sum(jnp.dot(part, triu, preferred_element_type=F32) for part in _split3(gt))
    beta = beta_ref[...]

    for h in range(GDN_HEADS):
        sl = slice(h * GDN_DK, (h + 1) * GDN_DK)
        q = q_ref[:, sl]
        k = k_ref[:, sl]
        v = v_ref[:, sl]
        kt = kt_ref[sl, :]
        gcol = gc[:, h:h + 1]
        grow = gct[h:h + 1, :]
        glast = gct[h:h + 1, c - 1:c]
        bcol = beta[:, h:h + 1]
        kb = k * bcol
        vb = v * bcol
        e = jnp.exp(jnp.where(incl, gcol - grow, 0.0))
        dec_i = jnp.where(incl, e, 0.0)
        dec_s = jnp.where(strict, e, 0.0)
        ktb = kt.astype(BF16)
        a = jnp.dot(kb.astype(BF16), ktb, preferred_element_type=F32) * dec_s
        x = -a
        t = eye + x
        for _ in range(6):
            xb = x.astype(BF16)
            x = jnp.dot(xb, xb, preferred_element_type=F32)
            t = t + jnp.dot(t.astype(BF16), x.astype(BF16), preferred_element_type=F32)
        tb = t.astype(BF16)
        eg = jnp.exp(gcol)
        value = jnp.dot(tb, vb.astype(BF16), preferred_element_type=F32)
        kcd = jnp.dot(tb, (kb * eg).astype(BF16), preferred_element_type=F32)
        attn = jnp.dot(q.astype(BF16), ktb, preferred_element_type=F32) * dec_i
        s_old = s_ref[h]
        sb = s_old.astype(BF16)
        v_new = value - jnp.dot(kcd.astype(BF16), sb, preferred_element_type=F32)
        vnb = v_new.astype(BF16)
        o = (jnp.dot((q * eg).astype(BF16), sb, preferred_element_type=F32)
             + jnp.dot(attn.astype(BF16), vnb, preferred_element_type=F32))
        o_ref[:, sl] = o
        kdt = kt * jnp.exp(glast - grow)
        s_ref[h] = s_old * jnp.exp(glast) + jnp.dot(kdt.astype(BF16), vnb, preferred_element_type=F32)

    @pl.when(n == pl.num_programs(1) - 1)
    def _():
        s_out_ref[...] = s_ref[...]


def _delta_rule(q, k, v, kt, g, gt, beta, *, batch, seq):
    c = GDN_CHUNK
    assert seq % c == 0
    nc = seq // c
    w = GDN_HEADS * GDN_DK
    row_spec = pl.BlockSpec((c, w), lambda b, n: (b * nc + n, 0))
    small_spec = pl.BlockSpec((c, GDN_HEADS), lambda b, n: (b * nc + n, 0))
    return pl.pallas_call(
        _delta_kernel,
        grid=(batch, nc),
        in_specs=[row_spec, row_spec, row_spec,
                  pl.BlockSpec((None, w, c), lambda b, n: (b, 0, n)),
                  small_spec,
                  pl.BlockSpec((None, GDN_HEADS, c), lambda b, n: (b, 0, n)),
                  small_spec],
        out_specs=[row_spec, pl.BlockSpec((None, GDN_HEADS, GDN_DK, GDN_DV), lambda b, n: (b, 0, 0, 0))],
        out_shape=[jax.ShapeDtypeStruct((batch * seq, w), F32),
                   jax.ShapeDtypeStruct((batch, GDN_HEADS, GDN_DK, GDN_DV), F32)],
        scratch_shapes=[pltpu.VMEM((GDN_HEADS, GDN_DK, GDN_DV), F32)],
        compiler_params=_cparams(("arbitrary", "arbitrary")),
        name="gdn_delta_rule",
    )(q, k, v, kt, g, gt, beta)


def _gmm_kernel(bexp_ref, used_ref, x_ref, w_ref, o_ref, wb_ref):
    i = pl.program_id(0)
    prev = bexp_ref[jnp.maximum(i - 1, 0)]

    @pl.when((i == 0) | (bexp_ref[i] != prev))
    def _():
        wb_ref[...] = w_ref[...].astype(BF16)

    @pl.when(used_ref[i] != 0)
    def _():
        o_ref[...] = jnp.dot(x_ref[...], wb_ref[...], preferred_element_type=F32).astype(o_ref.dtype)

    @pl.when(used_ref[i] == 0)
    def _():
        o_ref[...] = jnp.zeros(o_ref.shape, o_ref.dtype)


def _gmm_down_kernel(bexp_ref, used_ref, g_ref, u_ref, w_ref, o_ref, wb_ref):
    i = pl.program_id(0)
    prev = bexp_ref[jnp.maximum(i - 1, 0)]

    @pl.when((i == 0) | (bexp_ref[i] != prev))
    def _():
        wb_ref[...] = w_ref[...].astype(BF16)

    @pl.when(used_ref[i] != 0)
    def _():
        gate = g_ref[...]
        act = gate * jax.nn.sigmoid(gate) * u_ref[...]
        o_ref[...] = jnp.dot(act.astype(BF16), wb_ref[...], preferred_element_type=F32)

    @pl.when(used_ref[i] == 0)
    def _():
        o_ref[...] = jnp.zeros(o_ref.shape, o_ref.dtype)


def _grouped_matmul(bexp, used, x, w, *, name):
    rows, k = x.shape
    n = w.shape[2]
    nb = rows // MOE_ROWS
    grid_spec = pltpu.PrefetchScalarGridSpec(
        num_scalar_prefetch=2,
        grid=(nb,),
        in_specs=[pl.BlockSpec((MOE_ROWS, k), lambda i, be, us: (i, 0)),
                  pl.BlockSpec((None, k, n), lambda i, be, us: (be[i], 0, 0))],
        out_specs=pl.BlockSpec((MOE_ROWS, n), lambda i, be, us: (i, 0)),
        scratch_shapes=[pltpu.VMEM((k, n), BF16)])
    return pl.pallas_call(
        _gmm_kernel, grid_spec=grid_spec,
        out_shape=jax.ShapeDtypeStruct((rows, n), F32),
        compiler_params=_cparams(("arbitrary",)),
        name=name,
    )(bexp, used, x, w)


def _grouped_down(bexp, used, gate, up, w):
    rows, k = gate.shape
    n = w.shape[2]
    nb = rows // MOE_ROWS
    row_spec = pl.BlockSpec((MOE_ROWS, k), lambda i, be, us: (i, 0))
    grid_spec = pltpu.PrefetchScalarGridSpec(
        num_scalar_prefetch=2,
        grid=(nb,),
        in_specs=[row_spec, row_spec, pl.BlockSpec((None, k, n), lambda i, be, us: (be[i], 0, 0))],
        out_specs=pl.BlockSpec((MOE_ROWS, n), lambda i, be, us: (i, 0)),
        scratch_shapes=[pltpu.VMEM((k, n), BF16)])
    return pl.pallas_call(
        _gmm_down_kernel, grid_spec=grid_spec,
        out_shape=jax.ShapeDtypeStruct((rows, n), F32),
        compiler_params=_cparams(("arbitrary",)),
        name="moe_down",
    )(bexp, used, gate, up, w)


def _rms(x, g):
    xf = x.astype(F32)
    return xf * lax.rsqrt(jnp.mean(xf * xf, axis=-1, keepdims=True) + EPS) * g.astype(F32)


def _l2n(x):
    return x * lax.rsqrt(jnp.sum(x * x, axis=-1, keepdims=True) + EPS)


def _rope(x, cos, sin):
    half = x.shape[-1] // 2
    x1, x2 = x[..., :half], x[..., half:]
    return jnp.concatenate([x1 * cos - x2 * sin, x2 * cos + x1 * sin], axis=-1)


def _rope_tables(pos):
    inv = ROPE_THETA ** (-jnp.arange(0, QK_ROPE, 2, dtype=F32) / QK_ROPE)
    ang = pos[:, None] * inv[None, :]
    return jnp.cos(ang), jnp.sin(ang)


def _reorder_w_in(w_in):
    o = 0
    wq = w_in[:, o:o + MLA_HEADS * QK_HEAD].reshape(D_MODEL, MLA_HEADS, QK_HEAD); o += MLA_HEADS * QK_HEAD
    wckv = w_in[:, o:o + KV_LORA]; o += KV_LORA
    wkr = w_in[:, o:o + QK_ROPE]; o += QK_ROPE
    wqkv = w_in[:, o:o + GDN_QKV_W]; o += GDN_QKV_W
    wa = w_in[:, o:o + GDN_HEADS]; o += GDN_HEADS
    wb = w_in[:, o:o + GDN_HEADS]; o += GDN_HEADS
    wz = w_in[:, o:o + GDN_HEADS * GDN_DV]; o += GDN_HEADS * GDN_DV
    wga = w_in[:, o:o + D_MODEL]; o += D_MODEL
    wgb = w_in[:, o:o + D_MODEL]
    pad = jnp.zeros((D_MODEL, 128 - QK_ROPE - 2 * GDN_HEADS), w_in.dtype)
    return jnp.concatenate([
        wq[:, :, :QK_NOPE].reshape(D_MODEL, -1), wq[:, :, QK_NOPE:].reshape(D_MODEL, -1),
        wckv, wkr, wa, wb, pad, wqkv, wz, wga, wgb], axis=1).astype(BF16)


def _mixer_inputs(x2d, norm_g, w_in_r, *, tm):
    u = _rms(x2d, norm_g).astype(BF16)
    return _matmul(u, w_in_r, tm=tm, tn=640, out_dtype=F32, name="in_proj")


def _mla_prep(proj, pos, norm_ckv, q_gain_nope, q_gain_rope, k_gain_rope):
    n = proj.shape[0]
    cos, sin = _rope_tables(pos)
    qn = _rms(proj[:, SEG_QN:SEG_QR].reshape(n, MLA_HEADS, QK_NOPE), q_gain_nope)
    qr = _rope(_rms(proj[:, SEG_QR:SEG_CKV].reshape(n, MLA_HEADS, QK_ROPE), q_gain_rope),
               cos[:, None, :], sin[:, None, :])
    c = _rms(proj[:, SEG_CKV:SEG_MISC], norm_ckv)
    kr = _rope(_rms(proj[:, SEG_MISC:SEG_MISC + QK_ROPE], k_gain_rope), cos, sin)
    return qn, qr, c, kr


def _gdn_gates(proj, a_log, dt_bias):
    a = proj[:, SEG_MISC + QK_ROPE:SEG_MISC + QK_ROPE + GDN_HEADS]
    b_raw = proj[:, SEG_MISC + QK_ROPE + GDN_HEADS:SEG_MISC + QK_ROPE + 2 * GDN_HEADS]
    g = -jnp.exp(a_log.astype(F32)) * jax.nn.softplus(a + dt_bias.astype(F32))
    return g, jax.nn.sigmoid(b_raw)


def _gated_out_norm(o, z, gdn_norm):
    n = o.shape[0]
    of = _rms(o.reshape(n, GDN_HEADS, GDN_DV), gdn_norm) * jax.nn.silu(z.reshape(n, GDN_HEADS, GDN_DV))
    return of.reshape(n, GDN_HEADS * GDN_DV)


def _merge(x2d, oa, ob, ga, gb, w_pa, w_pb, w_o, *, tm):
    pa = _matmul(oa.astype(BF16), w_pa, tm=tm, tn=1024, out_dtype=F32, name="proj_a")
    pb = _matmul(ob.astype(BF16), w_pb, tm=tm, tn=1024, out_dtype=F32, name="proj_b")
    m = jax.nn.sigmoid(ga) * pa + jax.nn.sigmoid(gb) * pb
    return x2d + _matmul(m.astype(BF16), w_o, tm=tm, tn=1024, out_dtype=F32, name="proj_o")


def _hier_moe(h, norm_g, w_rg, b_rg, w_re, b_re, w_gate, w_up, w_down):
    t = h.shape[0]
    hn = _rms(h, norm_g).astype(BF16)
    n_rt = N_GROUPS + N_EXPERTS
    w_rt = jnp.concatenate([w_rg, w_re, jnp.zeros((D_MODEL, 128 - n_rt), F32)], axis=1).astype(BF16)
    logits = _matmul(hn, w_rt, tm=1024, tn=128, out_dtype=F32, name="router")
    g_logits = logits[:, :N_GROUPS] + b_rg.astype(F32)
    p_group = jax.nn.softmax(g_logits, axis=-1)
    grp = jnp.argmax(g_logits, axis=-1).astype(jnp.int32)
    p_grp_sel = jnp.take_along_axis(p_group, grp[:, None], axis=-1)[:, 0]
    e_logits = (logits[:, N_GROUPS:n_rt] + b_re.astype(F32)).reshape(t, N_GROUPS, EXP_PER_GROUP)
    e_sel = jnp.take_along_axis(e_logits, grp[:, None, None], axis=1)[:, 0]
    top_w, top_i = lax.top_k(jax.nn.softmax(e_sel, axis=-1), TOP_K)
    top_w = top_w / jnp.sum(top_w, axis=-1, keepdims=True) * p_grp_sel[:, None]
    expert = grp[:, None] * EXP_PER_GROUP + top_i.astype(jnp.int32)
    n_assign = t * TOP_K
    e_flat = expert.reshape(n_assign)
    w_flat = top_w.reshape(n_assign)
    onehot = (e_flat[:, None] == jnp.arange(N_EXPERTS, dtype=jnp.int32)[None, :]).astype(jnp.int32)
    csum = jnp.cumsum(onehot, axis=0)
    rank = jnp.take_along_axis(csum, e_flat[:, None], axis=1)[:, 0] - 1
    counts = csum[-1]
    pcounts = (counts + MOE_ROWS - 1) // MOE_ROWS * MOE_ROWS
    pends = jnp.cumsum(pcounts)
    pstarts = pends - pcounts
    dest = pstarts[e_flat] + rank
    n_blocks = -(-n_assign // MOE_ROWS) + N_EXPERTS
    n_rows = n_blocks * MOE_ROWS
    tok = jnp.arange(n_assign, dtype=jnp.int32) // TOP_K
    src = jnp.zeros((n_rows,), jnp.int32).at[dest].set(tok)
    xb = hn[src]
    bstart = jnp.arange(n_blocks, dtype=jnp.int32) * MOE_ROWS
    bexp = jnp.minimum(jnp.searchsorted(pends, bstart, side='right'), N_EXPERTS - 1).astype(jnp.int32)
    used = (bstart < pends[-1]).astype(jnp.int32)
    gate = _grouped_matmul(bexp, used, xb, w_gate, name="moe_gate")
    up = _grouped_matmul(bexp, used, xb, w_up, name="moe_up")
    yb = _grouped_down(bexp, used, gate, up, w_down)
    y = (yb[dest] * w_flat[:, None]).reshape(t, TOP_K, D_MODEL)
    return y[:, 0] + y[:, 1]


def kernel(x_prompt, x_sample, cache_ckv, cache_krope, state_ssm, state_conv, page_table, norm_attn, w_in, norm_ckv, w_uk, w_uv, q_gain_nope, q_gain_rope, k_gain_nope, k_gain_rope, conv_w, gdn_a_log, gdn_dt_bias, gdn_norm, w_pa, w_pb, w_o, norm_ffn, w_router_group, b_router_group, w_router_expert, b_router_expert, w_gate, w_up, w_down):
    depth = w_in.shape[0]
    bsz, seq, _ = x_prompt.shape
    n_s, s_new, _ = x_sample.shape
    assert s_new == 1
    n_p = bsz * seq
    n_past = page_table.shape[1] * PAGE_SIZE
    pos_p = jnp.tile(jnp.arange(seq, dtype=F32), bsz)
    pos_s = jnp.full((n_s,), n_past, F32)
    xp = x_prompt.reshape(n_p, D_MODEL)
    xs = x_sample.reshape(n_s, D_MODEL)
    outs = [[] for _ in range(8)]
    for l in range(depth):
        w_in_r = _reorder_w_in(w_in[l])
        w_uk_b = w_uk[l].astype(BF16)
        w_uv_b = w_uv[l].astype(BF16)
        w_pa_b, w_pb_b, w_o_b = w_pa[l].astype(BF16), w_pb[l].astype(BF16), w_o[l].astype(BF16)
        proj_p = _mixer_inputs(xp, norm_attn[l], w_in_r, tm=1024)
        proj_s = _mixer_inputs(xs, norm_attn[l], w_in_r, tm=n_s)

        qn_p, qr_p, c_p, kr_p = _mla_prep(proj_p, pos_p, norm_ckv[l], q_gain_nope[l], q_gain_rope[l], k_gain_rope[l])
        c_pb = c_p.astype(BF16)
        kn_p = _rms(_matmul(c_pb, w_uk_b, tm=1024, tn=1024, out_dtype=F32, name="k_up").reshape(n_p, MLA_HEADS, QK_NOPE),
                    k_gain_nope[l])
        v_p = _matmul(c_pb, w_uv_b, tm=1024, tn=1024, out_dtype=BF16, name="v_up")
        zpad = jnp.zeros((n_p, MLA_HEADS, QK_PAD - QK_HEAD), BF16)
        q_cat = jnp.concatenate([qn_p.astype(BF16), qr_p.astype(BF16), zpad], axis=-1).reshape(n_p, MLA_HEADS * QK_PAD)
        k_cat = jnp.concatenate([kn_p.astype(BF16),
                                 jnp.broadcast_to(kr_p.astype(BF16)[:, None, :], (n_p, MLA_HEADS, QK_ROPE)),
                                 zpad], axis=-1).reshape(n_p, MLA_HEADS * QK_PAD)
        oa_p = _flash_attention(q_cat, k_cat, v_p, batch=bsz, seq=seq, tq=512)

        qn_s, qr_s, c_s, kr_s = _mla_prep(proj_s, pos_s, norm_ckv[l], q_gain_nope[l], q_gain_rope[l], k_gain_rope[l])
        knn_s = _rms(_matmul(c_s.astype(BF16), w_uk_b, tm=n_s, tn=1024, out_dtype=F32, name="k_up_s")
                     .reshape(n_s, MLA_HEADS, QK_NOPE), k_gain_nope[l])
        qg = (qn_s * k_gain_nope[l].astype(F32)).astype(BF16).reshape(n_s, MLA_HEADS * QK_NOPE)
        wukt = w_uk_b.T
        qa = _headwise_matmul(qg, wukt, heads=MLA_HEADS, b_rows_by_head=True, name="q_absorb")
        o_lat = _paged_attention(page_table, cache_ckv[l], cache_krope[l], wukt,
                                 qa.astype(BF16).reshape(n_s, MLA_HEADS, KV_LORA), qr_s.astype(BF16),
                                 qn_s, knn_s, kr_s[:, None, :], c_s[:, None, :])
        oa_s = _headwise_matmul(o_lat.astype(BF16).reshape(n_s, MLA_HEADS * KV_LORA), w_uv_b,
                                heads=MLA_HEADS, b_rows_by_head=False, name="v_absorb")

        qkv_raw_p = proj_p[:, SEG_QKV:SEG_Z].reshape(bsz, seq, GDN_QKV_W)
        xpad = jnp.concatenate([jnp.zeros((bsz, CONV_W - 1, GDN_QKV_W), F32), qkv_raw_p], axis=1)
        acc = xpad[:, 0:seq] * conv_w[l, 0]
        for w in range(1, CONV_W):
            acc = acc + xpad[:, w:w + seq] * conv_w[l, w]
        qkv = jax.nn.silu(acc).reshape(n_p, GDN_QKV_W)
        conv_p = xpad[:, seq:]
        hk = GDN_HEADS * GDN_DK
        q_g = (_l2n(qkv[:, :hk].reshape(n_p, GDN_HEADS, GDN_DK)) * (GDN_DK ** -0.5)).reshape(n_p, hk)
        k_g = _l2n(qkv[:, hk:2 * hk].reshape(n_p, GDN_HEADS, GDN_DK)).reshape(n_p, hk)
        v_g = qkv[:, 2 * hk:]
        g_p, beta_p = _gdn_gates(proj_p, gdn_a_log[l], gdn_dt_bias[l])
        kt_g = jnp.swapaxes(k_g.reshape(bsz, seq, hk), 1, 2)
        gt_p = jnp.swapaxes(g_p.reshape(bsz, seq, GDN_HEADS), 1, 2)
        o_g, ssm_p = _delta_rule(q_g, k_g, v_g, kt_g, g_p, gt_p, beta_p, batch=bsz, seq=seq)
        ob_p = _gated_out_norm(o_g, proj_p[:, SEG_Z:SEG_GA], gdn_norm[l])

        qkv_raw_s = proj_s[:, SEG_QKV:SEG_Z]
        xpad_s = jnp.concatenate([state_conv[l].astype(F32), qkv_raw_s[:, None, :]], axis=1)
        acc_s = xpad_s[:, 0] * conv_w[l, 0]
        for w in range(1, CONV_W):
            acc_s = acc_s + xpad_s[:, w] * conv_w[l, w]
        qkv_s = jax.nn.silu(acc_s)
        conv_s = xpad_s[:, 1:]
        q_s = _l2n(qkv_s[:, :hk].reshape(n_s, GDN_HEADS, GDN_DK)) * (GDN_DK ** -0.5)
        k_s = _l2n(qkv_s[:, hk:2 * hk].reshape(n_s, GDN_HEADS, GDN_DK))
        v_s = qkv_s[:, 2 * hk:].reshape(n_s, GDN_HEADS, GDN_DV)
        g_s, beta_s = _gdn_gates(proj_s, gdn_a_log[l], gdn_dt_bias[l])
        eg = jnp.exp(g_s)[..., None]
        s0 = state_ssm[l].astype(F32)
        kcd = k_s * beta_s[..., None] * eg
        v_new = v_s * beta_s[..., None] - jnp.einsum('bhk,bhkv->bhv', kcd, s0)
        qk = jnp.einsum('bhd,bhd->bh', q_s, k_s)[..., None]
        o_s = jnp.einsum('bhk,bhkv->bhv', q_s * eg, s0) + qk * v_new
        ssm_s = s0 * eg[..., None] + jnp.einsum('bhk,bhv->bhkv', k_s, v_new)
        ob_s = _gated_out_norm(o_s.reshape(n_s, hk), proj_s[:, SEG_Z:SEG_GA], gdn_norm[l])

        hp = _merge(xp, oa_p, ob_p, proj_p[:, SEG_GA:SEG_GB], proj_p[:, SEG_GB:], w_pa_b, w_pb_b, w_o_b, tm=1024)
        hs = _merge(xs, oa_s, ob_s, proj_s[:, SEG_GA:SEG_GB], proj_s[:, SEG_GB:], w_pa_b, w_pb_b, w_o_b, tm=n_s)
        h_all = jnp.concatenate([hp, hs], axis=0)
        y_all = h_all + _hier_moe(h_all, norm_ffn[l], w_router_group[l], b_router_group[l], w_router_expert[l],
                                  b_router_expert[l], w_gate[l], w_up[l], w_down[l])
        xp, xs = y_all[:n_p], y_all[n_p:]

        for lst, val in zip(outs, (c_p.reshape(bsz, seq, KV_LORA), kr_p.reshape(bsz, seq, QK_ROPE),
                                   c_s.reshape(n_s, 1, KV_LORA), kr_s.reshape(n_s, 1, QK_ROPE),
                                   ssm_p, conv_p, ssm_s, conv_s)):
            lst.append(val)
    return (xp.reshape(bsz, seq, D_MODEL), xs.reshape(n_s, 1, D_MODEL)) + tuple(jnp.stack(o) for o in outs)
```

```python
import functools

import jax
import jax.numpy as jnp
from jax import lax
from jax.experimental import pallas as pl
from jax.experimental.pallas import tpu as pltpu

F32 = jnp.float32
BF16 = jnp.bfloat16

D_MODEL = 2048
PAGE_SIZE = 128
MLA_HEADS = 16
QK_NOPE = 128
QK_ROPE = 64
QK_HEAD = QK_NOPE + QK_ROPE
QK_PAD = 256
V_HEAD = 128
KV_LORA = 512
ROPE_THETA = 10000.0
GDN_HEADS = 16
GDN_DK = 128
GDN_DV = 128
GDN_HK = GDN_HEADS * GDN_DK
GDN_QKV_W = GDN_HEADS * (2 * GDN_DK + GDN_DV)
CONV_W = 4
GDN_CHUNK = 128
N_GROUPS = 8
EXP_PER_GROUP = 8
N_EXPERTS = N_GROUPS * EXP_PER_GROUP
TOP_K = 2
D_EXPERT = 1408
MOE_ROWS = 128
EPS = 1e-6
LANES = 128
SUBLANES = 8

V7X_VMEM_LIMIT = 56 * 1024 * 1024

SEG_Z = 0
SEG_GA = SEG_Z + GDN_HEADS * GDN_DV
SEG_GB = SEG_GA + D_MODEL
SEG_QKV = SEG_GB + D_MODEL
SEG_QN = SEG_QKV + GDN_QKV_W
SEG_QR = SEG_QN + MLA_HEADS * QK_NOPE
SEG_CKV = SEG_QR + MLA_HEADS * QK_ROPE
SEG_MISC = SEG_CKV + KV_LORA
D_PROJ = SEG_MISC + LANES
MISC_A = QK_ROPE
MISC_B = QK_ROPE + GDN_HEADS


def _cparams(sem, vmem=V7X_VMEM_LIMIT):
    return pltpu.CompilerParams(dimension_semantics=sem, vmem_limit_bytes=vmem)


def _tile(dim, cap, align):
    if dim <= cap:
        return dim
    for t in range(cap - cap % align, 0, -align):
        if dim % t == 0:
            return t
    raise ValueError(f"no {align}-aligned tile <= {cap} divides {dim}")


def _nt_dot(a, b):
    return lax.dot_general(a, b, (((1,), (1,)), ((), ())), preferred_element_type=F32)


def _rms_rows(x, gain):
    return x * lax.rsqrt(jnp.mean(x * x, axis=-1, keepdims=True) + EPS) * gain


def _sigmoid(x):
    return 1.0 / (1.0 + jnp.exp(-x))


def _in_proj_kernel(x_ref, g_ref, w_ref, o_ref, u_ref):
    @pl.when(pl.program_id(1) == 0)
    def _():
        u_ref[...] = _rms_rows(x_ref[...], g_ref[...]).astype(BF16)

    o_ref[...] = jnp.dot(u_ref[...], w_ref[...], preferred_element_type=F32)


def _in_proj(x, gain, w, *, tm):
    m, d = x.shape
    n = w.shape[1]
    tm = _tile(m, tm, 16)
    tn = _tile(n, 640, LANES)
    return pl.pallas_call(
        _in_proj_kernel,
        grid=(m // tm, n // tn),
        in_specs=[pl.BlockSpec((tm, d), lambda i, j: (i, 0)),
                  pl.BlockSpec((1, d), lambda i, j: (0, 0)),
                  pl.BlockSpec((d, tn), lambda i, j: (0, j))],
        out_specs=pl.BlockSpec((tm, tn), lambda i, j: (i, j)),
        out_shape=jax.ShapeDtypeStruct((m, n), F32),
        scratch_shapes=[pltpu.VMEM((tm, d), BF16)],
        compiler_params=_cparams(("arbitrary", "arbitrary")),
        name="in_proj",
    )(x, gain, w)


def _mm_kernel(a_ref, b_ref, o_ref):
    o_ref[...] = jnp.dot(a_ref[...], b_ref[...], preferred_element_type=F32).astype(o_ref.dtype)


def _headwise_matmul(a, b, *, heads, b_rows_by_head, name):
    m = a.shape[0]
    k = a.shape[1] // heads
    if b_rows_by_head:
        n = b.shape[1]
        b_spec = pl.BlockSpec((k, n), lambda h: (h, 0))
    else:
        n = b.shape[1] // heads
        b_spec = pl.BlockSpec((k, n), lambda h: (0, h))
    return pl.pallas_call(
        _mm_kernel,
        grid=(heads,),
        in_specs=[pl.BlockSpec((m, k), lambda h: (0, h)), b_spec],
        out_specs=pl.BlockSpec((m, n), lambda h: (0, h)),
        out_shape=jax.ShapeDtypeStruct((m, heads * n), F32),
        compiler_params=_cparams(("arbitrary",)),
        name=name,
    )(a, b)


def _swap_halves(y, lane):
    half = QK_ROPE // 2
    return jnp.where(lane % QK_ROPE < half, pltpu.roll(y, LANES - half, axis=1), pltpu.roll(y, half, axis=1))


def _rope_pair(x, gain2, cos4, sin4, lane):
    lo = lane < QK_ROPE
    ss = x * x
    s_lo = jnp.sum(jnp.where(lo, ss, 0.0), axis=-1, keepdims=True)
    s_hi = jnp.sum(jnp.where(lo, 0.0, ss), axis=-1, keepdims=True)
    r = jnp.where(lo, lax.rsqrt(s_lo * (1.0 / QK_ROPE) + EPS), lax.rsqrt(s_hi * (1.0 / QK_ROPE) + EPS))
    y = x * r * gain2
    return y * cos4 + _swap_halves(y, lane) * sin4


def _mla_prep_kernel(qn_ref, qr_ref, ckv_ref, misc_ref, cos_ref, sin_ref, gqn_ref, gqr_ref, gkn_ref, gkr_ref,
                     gc_ref, wuk_ref, *rest, with_vt, with_qg):
    rest = list(rest)
    wuvt_ref = rest.pop(0) if with_vt else None
    qcat_ref, kcat_ref, c_ref, kr_ref = rest[:4]
    rest = rest[4:]
    vt_ref = rest.pop(0) if with_vt else None
    qg_ref = rest.pop(0) if with_qg else None
    rows = qn_ref.shape[0]
    lane = lax.broadcasted_iota(jnp.int32, (rows, LANES), 1)
    lo = lane < QK_ROPE
    cos4 = cos_ref[...]
    sin4 = sin_ref[...]

    c = _rms_rows(ckv_ref[...], gc_ref[...])
    c_ref[...] = c
    cb = c.astype(BF16)
    kr = jnp.where(lo, _rope_pair(misc_ref[...], gkr_ref[...], cos4, sin4, lane), 0.0)
    kr_ref[...] = kr
    krb = kr.astype(BF16)

    gqn = gqn_ref[...]
    gkn = gkn_ref[...]
    for h in range(MLA_HEADS):
        qn = _rms_rows(qn_ref[:, h * QK_NOPE:(h + 1) * QK_NOPE], gqn)
        qcat_ref[:, h * QK_PAD:h * QK_PAD + QK_NOPE] = qn.astype(BF16)
        if with_qg:
            qg_ref[:, h * QK_NOPE:(h + 1) * QK_NOPE] = (qn * gkn).astype(BF16)
    gqr = gqr_ref[...]
    for j in range(MLA_HEADS // 2):
        rot = _rope_pair(qr_ref[:, j * LANES:(j + 1) * LANES], gqr, cos4, sin4, lane)
        even = jnp.where(lo, rot, 0.0)
        odd = jnp.where(lo, pltpu.roll(rot, QK_ROPE, axis=1), 0.0)
        qcat_ref[:, (2 * j) * QK_PAD + QK_NOPE:(2 * j + 1) * QK_PAD] = even.astype(BF16)
        qcat_ref[:, (2 * j + 1) * QK_PAD + QK_NOPE:(2 * j + 2) * QK_PAD] = odd.astype(BF16)

    kn = jnp.dot(cb, wuk_ref[...], preferred_element_type=F32)
    for h in range(MLA_HEADS):
        kh = _rms_rows(kn[:, h * QK_NOPE:(h + 1) * QK_NOPE], gkn)
        kcat_ref[:, h * QK_PAD:h * QK_PAD + QK_NOPE] = kh.astype(BF16)
        kcat_ref[:, h * QK_PAD + QK_NOPE:(h + 1) * QK_PAD] = krb
    if with_vt:
        vt_ref[...] = _nt_dot(wuvt_ref[...], cb).astype(BF16)


def _mla_prep(proj, cos4, sin4, gqn, gqr2, gkn, gkr2, gckv, w_uk_b, w_uvt_b, *, tm, with_vt, with_qg):
    n = proj.shape[0]
    tm = _tile(n, tm, 16)
    nt = n // tm
    hq = MLA_HEADS * QK_NOPE
    row = lambda w, off: pl.BlockSpec((tm, w), lambda i, off=off, w=w: (i, off // w))
    const = lambda a: pl.BlockSpec(a.shape, lambda i: (0,) * a.ndim)
    in_specs = [row(hq, SEG_QN), row(MLA_HEADS * QK_ROPE, SEG_QR), row(KV_LORA, SEG_CKV), row(LANES, SEG_MISC),
                pl.BlockSpec((tm, LANES), lambda i: (i, 0)), pl.BlockSpec((tm, LANES), lambda i: (i, 0)),
                const(gqn), const(gqr2), const(gkn), const(gkr2), const(gckv), const(w_uk_b)]
    args = [proj, proj, proj, proj, cos4, sin4, gqn, gqr2, gkn, gkr2, gckv, w_uk_b]
    if with_vt:
        in_specs.append(const(w_uvt_b))
        args.append(w_uvt_b)
    out_specs = [pl.BlockSpec((tm, MLA_HEADS * QK_PAD), lambda i: (i, 0)),
                 pl.BlockSpec((tm, MLA_HEADS * QK_PAD), lambda i: (i, 0)),
                 pl.BlockSpec((tm, KV_LORA), lambda i: (i, 0)),
                 pl.BlockSpec((tm, LANES), lambda i: (i, 0))]
    out_shape = [jax.ShapeDtypeStruct((n, MLA_HEADS * QK_PAD), BF16),
                 jax.ShapeDtypeStruct((n, MLA_HEADS * QK_PAD), BF16),
                 jax.ShapeDtypeStruct((n, KV_LORA), F32),
                 jax.ShapeDtypeStruct((n, LANES), F32)]
    if with_vt:
        out_specs.append(pl.BlockSpec((None, MLA_HEADS * V_HEAD, tm), lambda i: (i, 0, 0)))
        out_shape.append(jax.ShapeDtypeStruct((nt, MLA_HEADS * V_HEAD, tm), BF16))
    if with_qg:
        out_specs.append(pl.BlockSpec((tm, hq), lambda i: (i, 0)))
        out_shape.append(jax.ShapeDtypeStruct((n, hq), BF16))
    return pl.pallas_call(
        functools.partial(_mla_prep_kernel, with_vt=with_vt, with_qg=with_qg),
        grid=(nt,), in_specs=in_specs, out_specs=out_specs, out_shape=out_shape,
        compiler_params=_cparams(("arbitrary",)),
        name="mla_prep",
    )(*args)


def _flash_kernel(q_ref, k_ref, vt_ref, o_ref, m_ref, l_ref, acc_ref, *, tq, scale):
    qi = pl.program_id(2)
    q = q_ref[...]
    m_ref[...] = jnp.full(m_ref.shape, -jnp.inf, F32)
    l_ref[...] = jnp.zeros(l_ref.shape, F32)
    acc_ref[...] = jnp.zeros(acc_ref.shape, F32)

    def block(ki, masked):
        start = pl.multiple_of(ki * tq, tq)
        k = k_ref[pl.ds(start, tq), :]
        s = _nt_dot(k, q) * scale
        if masked:
            key = lax.broadcasted_iota(jnp.int32, s.shape, 0)
            qry = lax.broadcasted_iota(jnp.int32, s.shape, 1)
            s = jnp.where(key <= qry, s, -jnp.inf)
        m_old = m_ref[...]
        m_new = jnp.maximum(m_old, jnp.max(s, axis=0, keepdims=True))
        alpha = jnp.exp(m_old - m_new)
        p = jnp.exp(s - m_new)
        l_ref[...] = l_ref[...] * alpha + jnp.sum(p, axis=0, keepdims=True)
        acc_ref[...] = acc_ref[...] * alpha + jnp.dot(vt_ref[ki], p.astype(BF16), preferred_element_type=F32)
        m_ref[...] = m_new

    def body(ki, carry):
        block(ki, False)
        return carry

    lax.fori_loop(0, qi, body, 0)
    block(qi, True)
    o_ref[...] = (acc_ref[...] / l_ref[...]).T.astype(o_ref.dtype)


def _flash_attention(q_cat, k_cat, vt, *, batch, seq, tq):
    nq = seq // tq
    kern = functools.partial(_flash_kernel, tq=tq, scale=QK_HEAD ** -0.5)
    return pl.pallas_call(
        kern,
        grid=(batch, MLA_HEADS, nq),
        in_specs=[pl.BlockSpec((tq, QK_PAD), lambda b, h, i: (b * nq + i, h)),
                  pl.BlockSpec((seq, QK_PAD), lambda b, h, i: (b, h)),
                  pl.BlockSpec((nq, V_HEAD, tq), lambda b, h, i: (b, h, 0))],
        out_specs=pl.BlockSpec((tq, V_HEAD), lambda b, h, i: (b * nq + i, h)),
        out_shape=jax.ShapeDtypeStruct((batch * seq, MLA_HEADS * V_HEAD), BF16),
        scratch_shapes=[pltpu.VMEM((1, tq), F32), pltpu.VMEM((1, tq), F32), pltpu.VMEM((V_HEAD, tq), F32)],
        compiler_params=_cparams(("arbitrary", "arbitrary", "arbitrary")),
        name="mla_prompt_attention",
    )(q_cat, k_cat, vt)


PAGES_PER_STEP = 16
PAGE_GROUPS = 2


def _paged_kernel(pt_ref, *refs, scale):
    npg = PAGES_PER_STEP
    c_refs = refs[:npg]
    kr_refs = refs[npg:2 * npg]
    (wukt_ref, qa_ref, qcat_ref, kcat_ref, cn_ref, o_ref,
     lhs_ref, c_scr, kr_scr, m_ref, l_ref, acc_ref) = refs[2 * npg:]
    b = pl.program_id(0)
    p = pl.program_id(1)
    n_hd = MLA_HEADS * QK_NOPE

    @pl.when((b == 0) & (p == 0))
    def _():
        lhs_ref[0:n_hd, :] = wukt_ref[...]

    @pl.when(p == 0)
    def _():
        lhs_ref[n_hd:n_hd + MLA_HEADS, :] = qa_ref[...]
        m_ref[...] = jnp.full(m_ref.shape, -jnp.inf, F32)
        l_ref[...] = jnp.zeros(l_ref.shape, F32)
        acc_ref[...] = jnp.zeros(acc_ref.shape, F32)

    for i in range(npg):
        c_scr[i * PAGE_SIZE:(i + 1) * PAGE_SIZE, :] = c_refs[i][...].astype(BF16)
        kr_scr[i * PAGE_SIZE:(i + 1) * PAGE_SIZE, :] = kr_refs[i][...].astype(BF16)
    qr = qcat_ref[:, QK_NOPE:QK_NOPE + QK_ROPE]
    span = npg * PAGE_SIZE // PAGE_GROUPS
    scores = []
    for g in range(PAGE_GROUPS):
        c = c_scr[g * span:(g + 1) * span, :]
        res = _nt_dot(lhs_ref[...], c)
        kt = res[0:n_hd, :].reshape(MLA_HEADS, QK_NOPE, span)
        inv = lax.rsqrt(jnp.sum(kt * kt, axis=1) * (1.0 / QK_NOPE) + EPS)
        s_rope = _nt_dot(qr, kr_scr[g * span:(g + 1) * span, :])
        scores.append((res[n_hd:n_hd + MLA_HEADS, :] * inv + s_rope) * scale)
    m_old = m_ref[...]
    m_new = m_old
    for s in scores:
        m_new = jnp.maximum(m_new, jnp.max(s, axis=-1, keepdims=True))
    alpha = jnp.exp(m_old - m_new)
    l_new = l_ref[...] * alpha
    acc = acc_ref[...] * alpha
    for g, s in enumerate(scores):
        pr = jnp.exp(s - m_new)
        l_new = l_new + jnp.sum(pr, axis=-1, keepdims=True)
        acc = acc + jnp.dot(pr.astype(BF16), c_scr[g * span:(g + 1) * span, :], preferred_element_type=F32)
    l_ref[...] = l_new
    acc_ref[...] = acc
    m_ref[...] = m_new

    @pl.when(p == pl.num_programs(1) - 1)
    def _():
        s_new = jnp.sum(qcat_ref[...].astype(F32) * kcat_ref[...].astype(F32), axis=-1, keepdims=True) * scale
        m_fin = jnp.maximum(m_new, s_new)
        a2 = jnp.exp(m_new - m_fin)
        p_new = jnp.exp(s_new - m_fin)
        l_fin = l_new * a2 + p_new
        cn = cn_ref[...].astype(BF16).astype(F32)
        o_ref[...] = (acc * a2 + p_new * cn) / l_fin


def _paged_attention(page_table, ckv_pool, kr_pool, wukt, qa, q_cat, k_cat, cn):
    n_s, n_pages = page_table.shape
    npg = PAGES_PER_STEP
    assert n_pages % npg == 0
    steps = n_pages // npg
    span = npg * PAGE_SIZE

    def page_spec(width, i):
        return pl.BlockSpec((None, PAGE_SIZE, width), lambda b, p, pt, i=i: (pt[b, p * npg + i], 0, 0))

    def per_sample(shape):
        return pl.BlockSpec((None,) + shape, lambda b, p, pt: (b, 0, 0))

    in_specs = ([page_spec(KV_LORA, i) for i in range(npg)] + [page_spec(QK_ROPE, i) for i in range(npg)] + [
        pl.BlockSpec(wukt.shape, lambda b, p, pt: (0, 0)),
        per_sample((MLA_HEADS, KV_LORA)), per_sample((MLA_HEADS, QK_PAD)), per_sample((MLA_HEADS, QK_PAD)),
        per_sample((1, KV_LORA))])
    grid_spec = pltpu.PrefetchScalarGridSpec(
        num_scalar_prefetch=1,
        grid=(n_s, steps),
        in_specs=in_specs,
        out_specs=pl.BlockSpec((None, MLA_HEADS, KV_LORA), lambda b, p, pt: (b, 0, 0)),
        scratch_shapes=[pltpu.VMEM((MLA_HEADS * QK_NOPE + MLA_HEADS, KV_LORA), BF16),
                        pltpu.VMEM((span, KV_LORA), BF16), pltpu.VMEM((span, QK_ROPE), BF16),
                        pltpu.VMEM((MLA_HEADS, 1), F32), pltpu.VMEM((MLA_HEADS, 1), F32),
                        pltpu.VMEM((MLA_HEADS, KV_LORA), F32)])
    return pl.pallas_call(
        functools.partial(_paged_kernel, scale=QK_HEAD ** -0.5),
        grid_spec=grid_spec,
        out_shape=jax.ShapeDtypeStruct((n_s, MLA_HEADS, KV_LORA), F32),
        compiler_params=_cparams(("arbitrary", "arbitrary")),
        name="mla_sample_attention",
    )(page_table, *([ckv_pool] * npg), *([kr_pool] * npg), wukt, qa, q_cat, k_cat, cn)


def _gdn_prep_kernel(x_ref, halo_ref, misc_ref, cw_ref, alog_ref, dtb_ref, qkv_ref, kt_ref, gates_ref, xs_ref,
                     *, tiles_per_seq):
    i = pl.program_id(0)
    s = pl.program_id(1)
    tm = x_ref.shape[0]
    first = (i % tiles_per_seq) == 0

    @pl.when(first)
    def _():
        xs_ref[0:SUBLANES, :] = jnp.zeros((SUBLANES, xs_ref.shape[1]), F32)

    @pl.when(jnp.logical_not(first))
    def _():
        xs_ref[0:SUBLANES, :] = halo_ref[...]

    xs_ref[SUBLANES:SUBLANES + tm, :] = x_ref[...]
    acc = xs_ref[pl.ds(SUBLANES - (CONV_W - 1), tm), :] * cw_ref[0:1, :]
    for w in range(1, CONV_W):
        acc = acc + xs_ref[pl.ds(SUBLANES - (CONV_W - 1) + w, tm), :] * cw_ref[w:w + 1, :]
    y = acc * _sigmoid(acc)

    @pl.when(s == 2)
    def _():
        qkv_ref[...] = y

    @pl.when(s < 2)
    def _():
        scale = jnp.where(s == 0, GDN_DK ** -0.5, 1.0).astype(F32)
        for h in range(GDN_HEADS):
            yh = y[:, h * GDN_DK:(h + 1) * GDN_DK]
            yn = yh * (lax.rsqrt(jnp.sum(yh * yh, axis=-1, keepdims=True) + EPS) * scale)
            qkv_ref[:, h * GDN_DK:(h + 1) * GDN_DK] = yn

    @pl.when(s == 1)
    def _():
        kt_ref[...] = qkv_ref[...].T

    @pl.when(s == 0)
    def _():
        x = misc_ref[...] + dtb_ref[...]
        softplus = jnp.maximum(x, 0.0) + jnp.log(1.0 + jnp.exp(-jnp.abs(x)))
        lane = lax.broadcasted_iota(jnp.int32, x.shape, 1)
        is_beta = (lane >= MISC_B) & (lane < MISC_B + GDN_HEADS)
        gates_ref[...] = jnp.where(is_beta, _sigmoid(misc_ref[...]), -jnp.exp(alog_ref[...]) * softplus)


def _gdn_prep(proj, conv_w, alog_l, dtb_l, *, batch, seq, tm):
    n = batch * seq
    tm = _tile(seq, tm, LANES)
    tps = seq // tm
    sec0 = SEG_QKV // GDN_HK
    halo_blocks = tm // SUBLANES
    return pl.pallas_call(
        functools.partial(_gdn_prep_kernel, tiles_per_seq=tps),
        grid=(n // tm, 3),
        in_specs=[pl.BlockSpec((tm, GDN_HK), lambda i, s: (i, sec0 + s)),
                  pl.BlockSpec((SUBLANES, GDN_HK), lambda i, s: (jnp.maximum(i * halo_blocks - 1, 0), sec0 + s)),
                  pl.BlockSpec((tm, LANES), lambda i, s: (i, SEG_MISC // LANES)),
                  pl.BlockSpec((CONV_W, GDN_HK), lambda i, s: (0, s)),
                  pl.BlockSpec((1, LANES), lambda i, s: (0, 0)),
                  pl.BlockSpec((1, LANES), lambda i, s: (0, 0))],
        out_specs=[pl.BlockSpec((tm, GDN_HK), lambda i, s: (i, s)),
                   pl.BlockSpec((None, GDN_HK, tm), lambda i, s: (i // tps, 0, i % tps)),
                   pl.BlockSpec((tm, LANES), lambda i, s: (i, 0))],
        out_shape=[jax.ShapeDtypeStruct((n, 3 * GDN_HK), F32),
                   jax.ShapeDtypeStruct((batch, GDN_HK, seq), F32),
                   jax.ShapeDtypeStruct((n, LANES), F32)],
        scratch_shapes=[pltpu.VMEM((tm + SUBLANES, GDN_HK), F32)],
        compiler_params=_cparams(("arbitrary", "arbitrary")),
        name="gdn_prep",
    )(proj, proj, proj, conv_w, alog_l, dtb_l)


def _split3(x):
    hi = x.astype(BF16)
    r1 = x - hi.astype(F32)
    mid = r1.astype(BF16)
    lo = (r1 - mid.astype(F32)).astype(BF16)
    return hi, mid, lo


def _bdot(a, b):
    return jnp.dot(a.astype(BF16), b.astype(BF16), preferred_element_type=F32)


def _delta_kernel(q_ref, k_ref, v_ref, kt_ref, gates_ref, o_ref, s_out_ref, s_ref):
    c = GDN_CHUNK
    n = pl.program_id(1)
    heads = range(GDN_HEADS)

    @pl.when(n == 0)
    def _():
        s_ref[...] = jnp.zeros(s_ref.shape, F32)

    row = lax.broadcasted_iota(jnp.int32, (c, c), 0)
    col = lax.broadcasted_iota(jnp.int32, (c, c), 1)
    incl = row >= col
    strict = row > col
    eye = (row == col).astype(F32)
    tril = incl.astype(BF16)
    triu = (row <= col).astype(BF16)
    gates = gates_ref[...]
    gates_t = gates.T
    gc_all = sum(jnp.dot(tril, part, preferred_element_type=F32) for part in _split3(gates))
    gct_all = sum(jnp.dot(part, triu, preferred_element_type=F32) for part in _split3(gates_t))

    sl = [slice(h * GDN_DK, (h + 1) * GDN_DK) for h in heads]
    gcol = [gc_all[:, MISC_A + h:MISC_A + h + 1] for h in heads]
    grow = [gct_all[MISC_A + h:MISC_A + h + 1, :] for h in heads]
    glast = [gct_all[MISC_A + h:MISC_A + h + 1, c - 1:c] for h in heads]
    bcol = [gates[:, MISC_B + h:MISC_B + h + 1] for h in heads]
    eg = [jnp.exp(g) for g in gcol]
    e = [jnp.exp(jnp.where(incl, gcol[h] - grow[h], 0.0)) for h in heads]
    kb = [k_ref[:, sl[h]] * bcol[h] for h in heads]
    ktb = [kt_ref[sl[h], :].astype(BF16) for h in heads]
    x = [-(jnp.dot(kb[h].astype(BF16), ktb[h], preferred_element_type=F32) * jnp.where(strict, e[h], 0.0))
         for h in heads]
    t = [eye + x[h] for h in heads]
    for _ in range(6):
        x = [_bdot(x[h], x[h]) for h in heads]
        t = [t[h] + _bdot(t[h], x[h]) for h in heads]
    rhs = [jnp.concatenate([v_ref[:, sl[h]] * bcol[h], kb[h] * eg[h]], axis=1) for h in heads]
    sol = [_bdot(t[h], rhs[h]) for h in heads]
    attn = [jnp.dot(q_ref[:, sl[h]].astype(BF16), ktb[h], preferred_element_type=F32) * jnp.where(incl, e[h], 0.0)
            for h in heads]
    s_old = [s_ref[h] for h in heads]
    lhs = [jnp.concatenate([sol[h][:, GDN_DV:], q_ref[:, sl[h]] * eg[h]], axis=0) for h in heads]
    prod = [_bdot(lhs[h], s_old[h]) for h in heads]
    v_new = [sol[h][:, :GDN_DV] - prod[h][:c] for h in heads]
    for h in heads:
        o_ref[:, sl[h]] = prod[h][c:] + _bdot(attn[h], v_new[h])
    for h in heads:
        kdt = kt_ref[sl[h], :] * jnp.exp(glast[h] - grow[h])
        s_ref[h] = s_old[h] * jnp.exp(glast[h]) + _bdot(kdt, v_new[h])

    @pl.when(n == pl.num_programs(1) - 1)
    def _():
        s_out_ref[...] = s_ref[...]


def _delta_rule(qkv, kt, gates, *, batch, seq):
    c = GDN_CHUNK
    assert seq % c == 0
    nc = seq // c
    sec = lambda s: pl.BlockSpec((c, GDN_HK), lambda b, n, s=s: (b * nc + n, s))
    return pl.pallas_call(
        _delta_kernel,
        grid=(batch, nc),
        in_specs=[sec(0), sec(1), sec(2),
                  pl.BlockSpec((None, GDN_HK, c), lambda b, n: (b, 0, n)),
                  pl.BlockSpec((c, LANES), lambda b, n: (b * nc + n, 0))],
        out_specs=[pl.BlockSpec((c, GDN_HK), lambda b, n: (b * nc + n, 0)),
                   pl.BlockSpec((None, GDN_HEADS, GDN_DK, GDN_DV), lambda b, n: (b, 0, 0, 0))],
        out_shape=[jax.ShapeDtypeStruct((batch * seq, GDN_HK), F32),
                   jax.ShapeDtypeStruct((batch, GDN_HEADS, GDN_DK, GDN_DV), F32)],
        scratch_shapes=[pltpu.VMEM((GDN_HEADS, GDN_DK, GDN_DV), F32)],
        compiler_params=_cparams(("arbitrary", "arbitrary")),
        name="gdn_delta_rule",
    )(qkv, qkv, qkv, kt, gates)


def _merge_kernel(og_ref, z_ref, oa_ref, ga_ref, gb_ref, gn_ref, wpa_ref, wpb_ref, m_ref, ob_ref):
    @pl.when(pl.program_id(1) == 0)
    def _():
        gn = gn_ref[...]
        for h in range(GDN_HEADS):
            sl = slice(h * GDN_DV, (h + 1) * GDN_DV)
            z = z_ref[:, sl]
            ob_ref[:, sl] = (_rms_rows(og_ref[:, sl], gn) * (z * _sigmoid(z))).astype(BF16)

    pa = jnp.dot(oa_ref[...], wpa_ref[...], preferred_element_type=F32)
    pb = jnp.dot(ob_ref[...], wpb_ref[...], preferred_element_type=F32)
    m_ref[...] = (_sigmoid(ga_ref[...]) * pa + _sigmoid(gb_ref[...]) * pb).astype(BF16)


def _merge(o_g, oa, proj, gdn_norm, w_pa_b, w_pb_b, *, tm):
    n = o_g.shape[0]
    tm = _tile(n, tm, 16)
    tn = 1024
    nj = D_MODEL // tn
    return pl.pallas_call(
        _merge_kernel,
        grid=(n // tm, nj),
        in_specs=[pl.BlockSpec((tm, GDN_HK), lambda i, j: (i, 0)),
                  pl.BlockSpec((tm, GDN_HK), lambda i, j: (i, SEG_Z // GDN_HK)),
                  pl.BlockSpec((tm, MLA_HEADS * V_HEAD), lambda i, j: (i, 0)),
                  pl.BlockSpec((tm, tn), lambda i, j: (i, SEG_GA // tn + j)),
                  pl.BlockSpec((tm, tn), lambda i, j: (i, SEG_GB // tn + j)),
                  pl.BlockSpec((1, GDN_DV), lambda i, j: (0, 0)),
                  pl.BlockSpec((MLA_HEADS * V_HEAD, tn), lambda i, j: (0, j)),
                  pl.BlockSpec((GDN_HK, tn), lambda i, j: (0, j))],
        out_specs=pl.BlockSpec((tm, tn), lambda i, j: (i, j)),
        out_shape=jax.ShapeDtypeStruct((n, D_MODEL), BF16),
        scratch_shapes=[pltpu.VMEM((tm, GDN_HK), BF16)],
        compiler_params=_cparams(("arbitrary", "arbitrary")),
        name="gated_merge",
    )(o_g, proj, oa, proj, proj, gdn_norm, w_pa_b, w_pb_b)


def _out_proj_kernel(*refs, aliased):
    if aliased:
        refs = refs[3:]
    m_ref, x_ref, wo_ref, gf_ref, wrt_ref, h_ref, hn_ref, lg_ref = refs
    h = x_ref[...] + jnp.dot(m_ref[...], wo_ref[...], preferred_element_type=F32)
    h_ref[...] = h
    hn = _rms_rows(h, gf_ref[...]).astype(BF16)
    hn_ref[...] = hn
    lg_ref[...] = jnp.dot(hn, wrt_ref[...], preferred_element_type=F32)


def _out_proj(m, x, w_o_b, norm_ffn, w_rt_b, *, tm, total_rows, row_offset, prev):
    n = m.shape[0]
    tm = _tile(n, tm, 16)
    assert row_offset % tm == 0
    off = row_offset // tm
    const = lambda a: pl.BlockSpec(a.shape, lambda i: (0,) * a.ndim)
    in_specs = [pl.BlockSpec((tm, D_MODEL), lambda i: (i, 0)), pl.BlockSpec((tm, D_MODEL), lambda i: (i, 0)),
                const(w_o_b), const(norm_ffn), const(w_rt_b)]
    args = [m, x, w_o_b, norm_ffn, w_rt_b]
    aliases = {}
    if prev is not None:
        in_specs = [pl.BlockSpec(memory_space=pl.ANY)] * 3 + in_specs
        args = list(prev) + args
        aliases = {0: 0, 1: 1, 2: 2}
    return pl.pallas_call(
        functools.partial(_out_proj_kernel, aliased=prev is not None),
        grid=(n // tm,),
        in_specs=in_specs,
        out_specs=[pl.BlockSpec((tm, D_MODEL), lambda i: (i + off, 0)),
                   pl.BlockSpec((tm, D_MODEL), lambda i: (i + off, 0)),
                   pl.BlockSpec((tm, LANES), lambda i: (i + off, 0))],
        out_shape=[jax.ShapeDtypeStruct((total_rows, D_MODEL), F32),
                   jax.ShapeDtypeStruct((total_rows, D_MODEL), BF16),
                   jax.ShapeDtypeStruct((total_rows, LANES), F32)],
        input_output_aliases=aliases,
        compiler_params=_cparams(("arbitrary",)),
        name="out_proj_router",
    )(*args)


def _gmm_kernel(bexp_ref, used_ref, x_ref, w_ref, o_ref, wb_ref):
    i = pl.program_id(0)
    prev = bexp_ref[jnp.maximum(i - 1, 0)]

    @pl.when((i == 0) | (bexp_ref[i] != prev))
    def _():
        wb_ref[...] = w_ref[...].astype(BF16)

    @pl.when(used_ref[i] != 0)
    def _():
        o_ref[...] = jnp.dot(x_ref[...], wb_ref[...], preferred_element_type=F32).astype(o_ref.dtype)

    @pl.when(used_ref[i] == 0)
    def _():
        o_ref[...] = jnp.zeros(o_ref.shape, o_ref.dtype)


def _gmm_down_kernel(bexp_ref, used_ref, g_ref, u_ref, w_ref, o_ref, wb_ref):
    i = pl.program_id(0)
    prev = bexp_ref[jnp.maximum(i - 1, 0)]

    @pl.when((i == 0) | (bexp_ref[i] != prev))
    def _():
        wb_ref[...] = w_ref[...].astype(BF16)

    @pl.when(used_ref[i] != 0)
    def _():
        gate = g_ref[...]
        act = gate * _sigmoid(gate) * u_ref[...]
        o_ref[...] = jnp.dot(act.astype(BF16), wb_ref[...], preferred_element_type=F32)

    @pl.when(used_ref[i] == 0)
    def _():
        o_ref[...] = jnp.zeros(o_ref.shape, o_ref.dtype)


def _grouped_matmul(bexp, used, x, w, *, name):
    rows, k = x.shape
    n = w.shape[2]
    nb = rows // MOE_ROWS
    grid_spec = pltpu.PrefetchScalarGridSpec(
        num_scalar_prefetch=2,
        grid=(nb,),
        in_specs=[pl.BlockSpec((MOE_ROWS, k), lambda i, be, us: (i, 0)),
                  pl.BlockSpec((None, k, n), lambda i, be, us: (be[i], 0, 0))],
        out_specs=pl.BlockSpec((MOE_ROWS, n), lambda i, be, us: (i, 0)),
        scratch_shapes=[pltpu.VMEM((k, n), BF16)])
    return pl.pallas_call(
        _gmm_kernel, grid_spec=grid_spec,
        out_shape=jax.ShapeDtypeStruct((rows, n), F32),
        compiler_params=_cparams(("arbitrary",)),
        name=name,
    )(bexp, used, x, w)


def _grouped_down(bexp, used, gate, up, w):
    rows, k = gate.shape
    n = w.shape[2]
    nb = rows // MOE_ROWS
    row_spec = pl.BlockSpec((MOE_ROWS, k), lambda i, be, us: (i, 0))
    grid_spec = pltpu.PrefetchScalarGridSpec(
        num_scalar_prefetch=2,
        grid=(nb,),
        in_specs=[row_spec, row_spec, pl.BlockSpec((None, k, n), lambda i, be, us: (be[i], 0, 0))],
        out_specs=pl.BlockSpec((MOE_ROWS, n), lambda i, be, us: (i, 0)),
        scratch_shapes=[pltpu.VMEM((k, n), BF16)])
    return pl.pallas_call(
        _gmm_down_kernel, grid_spec=grid_spec,
        out_shape=jax.ShapeDtypeStruct((rows, n), F32),
        compiler_params=_cparams(("arbitrary",)),
        name="moe_down",
    )(bexp, used, gate, up, w)


def _l2n(x):
    return x * lax.rsqrt(jnp.sum(x * x, axis=-1, keepdims=True) + EPS)


def _rms(x, g):
    return x * lax.rsqrt(jnp.mean(x * x, axis=-1, keepdims=True) + EPS) * g


def _rope_tables(pos):
    inv = ROPE_THETA ** (-jnp.arange(0, QK_ROPE, 2, dtype=F32) / QK_ROPE)
    ang = pos[:, None] * inv[None, :]
    cos, sin = jnp.cos(ang), jnp.sin(ang)
    return jnp.concatenate([cos, cos, cos, cos], axis=1), jnp.concatenate([-sin, sin, -sin, sin], axis=1)


def _reorder_w_in(w_in):
    o = 0
    wq = w_in[:, o:o + MLA_HEADS * QK_HEAD].reshape(D_MODEL, MLA_HEADS, QK_HEAD); o += MLA_HEADS * QK_HEAD
    wckv = w_in[:, o:o + KV_LORA]; o += KV_LORA
    wkr = w_in[:, o:o + QK_ROPE]; o += QK_ROPE
    wqkv = w_in[:, o:o + GDN_QKV_W]; o += GDN_QKV_W
    wa = w_in[:, o:o + GDN_HEADS]; o += GDN_HEADS
    wb = w_in[:, o:o + GDN_HEADS]; o += GDN_HEADS
    wz = w_in[:, o:o + GDN_HEADS * GDN_DV]; o += GDN_HEADS * GDN_DV
    wga = w_in[:, o:o + D_MODEL]; o += D_MODEL
    wgb = w_in[:, o:o + D_MODEL]
    pad = jnp.zeros((D_MODEL, LANES - QK_ROPE - 2 * GDN_HEADS), w_in.dtype)
    return jnp.concatenate([
        wz, wga, wgb, wqkv, wq[:, :, :QK_NOPE].reshape(D_MODEL, -1), wq[:, :, QK_NOPE:].reshape(D_MODEL, -1),
        wckv, wkr, wa, wb, pad], axis=1).astype(BF16)


def _lane_vec(v, offset):
    return jnp.zeros((1, LANES), F32).at[0, offset:offset + v.shape[0]].set(v.astype(F32))


def _hier_moe(h, hn, logits, b_rg, b_re, w_gate, w_up, w_down):
    t = h.shape[0]
    n_rt = N_GROUPS + N_EXPERTS
    g_logits = logits[:, :N_GROUPS] + b_rg.astype(F32)
    p_group = jax.nn.softmax(g_logits, axis=-1)
    grp = jnp.argmax(g_logits, axis=-1).astype(jnp.int32)
    p_grp_sel = jnp.take_along_axis(p_group, grp[:, None], axis=-1)[:, 0]
    e_logits = (logits[:, N_GROUPS:n_rt] + b_re.astype(F32)).reshape(t, N_GROUPS, EXP_PER_GROUP)
    e_sel = jnp.take_along_axis(e_logits, grp[:, None, None], axis=1)[:, 0]
    top_w, top_i = lax.top_k(jax.nn.softmax(e_sel, axis=-1), TOP_K)
    top_w = top_w / jnp.sum(top_w, axis=-1, keepdims=True) * p_grp_sel[:, None]
    expert = grp[:, None] * EXP_PER_GROUP + top_i.astype(jnp.int32)
    n_assign = t * TOP_K
    e_flat = expert.reshape(n_assign)
    w_flat = top_w.reshape(n_assign)
    onehot = (e_flat[:, None] == jnp.arange(N_EXPERTS, dtype=jnp.int32)[None, :]).astype(jnp.int32)
    csum = jnp.cumsum(onehot, axis=0)
    rank = jnp.take_along_axis(csum, e_flat[:, None], axis=1)[:, 0] - 1
    counts = csum[-1]
    pcounts = (counts + MOE_ROWS - 1) // MOE_ROWS * MOE_ROWS
    pends = jnp.cumsum(pcounts)
    pstarts = pends - pcounts
    dest = pstarts[e_flat] + rank
    n_blocks = -(-n_assign // MOE_ROWS) + N_EXPERTS
    n_rows = n_blocks * MOE_ROWS
    tok = jnp.arange(n_assign, dtype=jnp.int32) // TOP_K
    src = jnp.zeros((n_rows,), jnp.int32).at[dest].set(tok)
    xb = hn[src]
    bstart = jnp.arange(n_blocks, dtype=jnp.int32) * MOE_ROWS
    bexp = jnp.minimum(jnp.searchsorted(pends, bstart, side='right'), N_EXPERTS - 1).astype(jnp.int32)
    used = (bstart < pends[-1]).astype(jnp.int32)
    gate = _grouped_matmul(bexp, used, xb, w_gate, name="moe_gate")
    up = _grouped_matmul(bexp, used, xb, w_up, name="moe_up")
    yb = _grouped_down(bexp, used, gate, up, w_down)
    y = (yb[dest] * w_flat[:, None]).reshape(t, TOP_K, D_MODEL)
    return h + (y[:, 0] + y[:, 1])


def kernel(x_prompt, x_sample, cache_ckv, cache_krope, state_ssm, state_conv, page_table, norm_attn, w_in, norm_ckv, w_uk, w_uv, q_gain_nope, q_gain_rope, k_gain_nope, k_gain_rope, conv_w, gdn_a_log, gdn_dt_bias, gdn_norm, w_pa, w_pb, w_o, norm_ffn, w_router_group, b_router_group, w_router_expert, b_router_expert, w_gate, w_up, w_down):
    depth = w_in.shape[0]
    bsz, seq, _ = x_prompt.shape
    n_s, s_new, _ = x_sample.shape
    assert s_new == 1
    n_p = bsz * seq
    n_all = n_p + n_s
    n_past = page_table.shape[1] * PAGE_SIZE
    cos_p, sin_p = _rope_tables(jnp.tile(jnp.arange(seq, dtype=F32), bsz))
    cos_s, sin_s = _rope_tables(jnp.full((n_s,), n_past, F32))
    tq = _tile(seq, 512, LANES)
    xp = x_prompt.reshape(n_p, D_MODEL)
    xs = x_sample.reshape(n_s, D_MODEL)
    outs = [[] for _ in range(8)]
    row = lambda v: v.astype(F32)[None, :]
    for l in range(depth):
        w_in_r = _reorder_w_in(w_in[l])
        w_uk_b = w_uk[l].astype(BF16)
        w_uvt_b = w_uv[l].T.astype(BF16)
        wukt = w_uk_b.T
        w_pa_b, w_pb_b, w_o_b = w_pa[l].astype(BF16), w_pb[l].astype(BF16), w_o[l].astype(BF16)
        w_rt_b = jnp.concatenate([w_router_group[l], w_router_expert[l],
                                  jnp.zeros((D_MODEL, LANES - N_GROUPS - N_EXPERTS), F32)], axis=1).astype(BF16)
        gains = (row(q_gain_nope[l]), jnp.tile(row(q_gain_rope[l]), (1, 2)), row(k_gain_nope[l]),
                 jnp.tile(row(k_gain_rope[l]), (1, 2)), row(norm_ckv[l]))
        alog_l = _lane_vec(gdn_a_log[l], MISC_A)
        dtb_l = _lane_vec(gdn_dt_bias[l], MISC_A)
        proj_p = _in_proj(xp, row(norm_attn[l]), w_in_r, tm=1024)
        proj_s = _in_proj(xs, row(norm_attn[l]), w_in_r, tm=n_s)

        qcat_p, kcat_p, c_p, kr_p, vt_p = _mla_prep(proj_p, cos_p, sin_p, *gains, w_uk_b, w_uvt_b,
                                                    tm=tq, with_vt=True, with_qg=False)
        oa_p = _flash_attention(qcat_p, kcat_p, vt_p, batch=bsz, seq=seq, tq=tq)

        qcat_s, kcat_s, c_s, kr_s, qg_s = _mla_prep(proj_s, cos_s, sin_s, *gains, w_uk_b, w_uvt_b,
                                                    tm=n_s, with_vt=False, with_qg=True)
        qa = _headwise_matmul(qg_s, wukt, heads=MLA_HEADS, b_rows_by_head=True, name="q_absorb")
        o_lat = _paged_attention(page_table, cache_ckv[l], cache_krope[l], wukt,
                                 qa.astype(BF16).reshape(n_s, MLA_HEADS, KV_LORA),
                                 qcat_s.reshape(n_s, MLA_HEADS, QK_PAD), kcat_s.reshape(n_s, MLA_HEADS, QK_PAD),
                                 c_s[:, None, :])
        oa_s = _headwise_matmul(o_lat.astype(BF16).reshape(n_s, MLA_HEADS * KV_LORA), w_uv[l].astype(BF16),
                                heads=MLA_HEADS, b_rows_by_head=False, name="v_absorb").astype(BF16)

        qkv_p, kt_p, gates_p = _gdn_prep(proj_p, conv_w[l], alog_l, dtb_l, batch=bsz, seq=seq, tm=512)
        o_g, ssm_p = _delta_rule(qkv_p, kt_p, gates_p, batch=bsz, seq=seq)
        conv_p = proj_p[:, SEG_QKV:SEG_QKV + GDN_QKV_W].reshape(bsz, seq, GDN_QKV_W)[:, seq - (CONV_W - 1):]

        qkv_raw_s = proj_s[:, SEG_QKV:SEG_QKV + GDN_QKV_W]
        xpad_s = jnp.concatenate([state_conv[l].astype(F32), qkv_raw_s[:, None, :]], axis=1)
        acc_s = xpad_s[:, 0] * conv_w[l, 0]
        for w in range(1, CONV_W):
            acc_s = acc_s + xpad_s[:, w] * conv_w[l, w]
        qkv_s = jax.nn.silu(acc_s)
        conv_s = xpad_s[:, 1:]
        q_s = _l2n(qkv_s[:, :GDN_HK].reshape(n_s, GDN_HEADS, GDN_DK)) * (GDN_DK ** -0.5)
        k_s = _l2n(qkv_s[:, GDN_HK:2 * GDN_HK].reshape(n_s, GDN_HEADS, GDN_DK))
        v_s = qkv_s[:, 2 * GDN_HK:].reshape(n_s, GDN_HEADS, GDN_DV)
        a_s = proj_s[:, SEG_MISC + MISC_A:SEG_MISC + MISC_A + GDN_HEADS]
        b_s = proj_s[:, SEG_MISC + MISC_B:SEG_MISC + MISC_B + GDN_HEADS]
        g_s = -jnp.exp(gdn_a_log[l].astype(F32)) * jax.nn.softplus(a_s + gdn_dt_bias[l].astype(F32))
        beta_s = jax.nn.sigmoid(b_s)
        eg = jnp.exp(g_s)[..., None]
        s0 = state_ssm[l].astype(F32)
        kcd = k_s * beta_s[..., None] * eg
        v_new = v_s * beta_s[..., None] - jnp.einsum('bhk,bhkv->bhv', kcd, s0)
        qk = jnp.einsum('bhd,bhd->bh', q_s, k_s)[..., None]
        o_s = jnp.einsum('bhk,bhkv->bhv', q_s * eg, s0) + qk * v_new
        ssm_s = s0 * eg[..., None] + jnp.einsum('bhk,bhv->bhkv', k_s, v_new)

        m_p = _merge(o_g, oa_p, proj_p, row(gdn_norm[l]), w_pa_b, w_pb_b, tm=512)
        m_s = _merge(o_s.reshape(n_s, GDN_HK), oa_s, proj_s, row(gdn_norm[l]), w_pa_b, w_pb_b, tm=n_s)
        res_p = _out_proj(m_p, xp, w_o_b, row(norm_ffn[l]), w_rt_b, tm=512, total_rows=n_all, row_offset=0,
                          prev=None)
        h_all, hn_all, lg_all = _out_proj(m_s, xs, w_o_b, row(norm_ffn[l]), w_rt_b, tm=n_s, total_rows=n_all,
                                          row_offset=n_p, prev=res_p)
        y_all = _hier_moe(h_all, hn_all, lg_all, b_router_group[l], b_router_expert[l],
                          w_gate[l], w_up[l], w_down[l])
        xp, xs = y_all[:n_p], y_all[n_p:]

        for lst, val in zip(outs, (c_p.reshape(bsz, seq, KV_LORA), kr_p[:, :QK_ROPE].reshape(bsz, seq, QK_ROPE),
                                   c_s.reshape(n_s, 1, KV_LORA), kr_s[:, :QK_ROPE].reshape(n_s, 1, QK_ROPE),
                                   ssm_p, conv_p, ssm_s, conv_s)):
            lst.append(val)
    return (xp.reshape(bsz, seq, D_MODEL), xs.reshape(n_s, 1, D_MODEL)) + tuple(jnp.stack(o) for o in outs)
```

```python
import functools

import jax
import jax.numpy as jnp
from jax import lax
from jax.experimental import pallas as pl
from jax.experimental.pallas import tpu as pltpu

F32 = jnp.float32
BF16 = jnp.bfloat16

D_MODEL = 2048
PAGE_SIZE = 128
MLA_HEADS = 16
QK_NOPE = 128
QK_ROPE = 64
QK_HEAD = QK_NOPE + QK_ROPE
QK_PAD = 256
V_HEAD = 128
KV_LORA = 512
ROPE_THETA = 10000.0
GDN_HEADS = 16
GDN_DK = 128
GDN_DV = 128
GDN_HK = GDN_HEADS * GDN_DK
GDN_QKV_W = GDN_HEADS * (2 * GDN_DK + GDN_DV)
CONV_W = 4
GDN_CHUNK = 128
N_GROUPS = 8
EXP_PER_GROUP = 8
N_EXPERTS = N_GROUPS * EXP_PER_GROUP
TOP_K = 2
D_EXPERT = 1408
MOE_ROWS = 128
EPS = 1e-6
LANES = 128
SUBLANES = 8

V7X_VMEM_LIMIT = 56 * 1024 * 1024

SEG_Z = 0
SEG_GA = SEG_Z + GDN_HEADS * GDN_DV
SEG_GB = SEG_GA + D_MODEL
SEG_QKV = SEG_GB + D_MODEL
SEG_QN = SEG_QKV + GDN_QKV_W
SEG_QR = SEG_QN + MLA_HEADS * QK_NOPE
SEG_CKV = SEG_QR + MLA_HEADS * QK_ROPE
SEG_MISC = SEG_CKV + KV_LORA
PROJ_TN = 1024
D_PROJ = -(-(SEG_MISC + LANES) // PROJ_TN) * PROJ_TN
MISC_A = QK_ROPE
MISC_B = QK_ROPE + GDN_HEADS


def _cparams(sem, vmem=V7X_VMEM_LIMIT):
    return pltpu.CompilerParams(dimension_semantics=sem, vmem_limit_bytes=vmem)


def _tile(dim, cap, align):
    if dim <= cap:
        return dim
    for t in range(cap - cap % align, 0, -align):
        if dim % t == 0:
            return t
    raise ValueError(f"no {align}-aligned tile <= {cap} divides {dim}")


def _nt_dot(a, b):
    return lax.dot_general(a, b, (((1,), (1,)), ((), ())), preferred_element_type=F32)


def _rms_rows(x, gain):
    return x * lax.rsqrt(jnp.mean(x * x, axis=-1, keepdims=True) + EPS) * gain


def _sigmoid(x):
    return 1.0 / (1.0 + jnp.exp(-x))


def _in_proj_kernel(x_ref, g_ref, w_ref, o_ref, u_ref):
    @pl.when(pl.program_id(1) == 0)
    def _():
        u_ref[...] = _rms_rows(x_ref[...], g_ref[...]).astype(BF16)

    o_ref[...] = jnp.dot(u_ref[...], w_ref[...], preferred_element_type=F32)


def _in_proj(x, gain, w, *, tm):
    m, d = x.shape
    n = w.shape[1]
    tm = _tile(m, tm, 16)
    tn = PROJ_TN
    assert n % tn == 0
    return pl.pallas_call(
        _in_proj_kernel,
        grid=(m // tm, n // tn),
        in_specs=[pl.BlockSpec((tm, d), lambda i, j: (i, 0)),
                  pl.BlockSpec((1, d), lambda i, j: (0, 0)),
                  pl.BlockSpec((d, tn), lambda i, j: (0, j))],
        out_specs=pl.BlockSpec((tm, tn), lambda i, j: (i, j)),
        out_shape=jax.ShapeDtypeStruct((m, n), F32),
        scratch_shapes=[pltpu.VMEM((tm, d), BF16)],
        compiler_params=_cparams(("arbitrary", "arbitrary")),
        name="in_proj",
    )(x, gain, w)


def _mm_kernel(a_ref, b_ref, o_ref):
    o_ref[...] = jnp.dot(a_ref[...], b_ref[...], preferred_element_type=F32).astype(o_ref.dtype)


def _headwise_matmul(a, b, *, heads, b_rows_by_head, name):
    m = a.shape[0]
    k = a.shape[1] // heads
    if b_rows_by_head:
        n = b.shape[1]
        b_spec = pl.BlockSpec((k, n), lambda h: (h, 0))
    else:
        n = b.shape[1] // heads
        b_spec = pl.BlockSpec((k, n), lambda h: (0, h))
    return pl.pallas_call(
        _mm_kernel,
        grid=(heads,),
        in_specs=[pl.BlockSpec((m, k), lambda h: (0, h)), b_spec],
        out_specs=pl.BlockSpec((m, n), lambda h: (0, h)),
        out_shape=jax.ShapeDtypeStruct((m, heads * n), F32),
        compiler_params=_cparams(("arbitrary",)),
        name=name,
    )(a, b)


def _swap_halves(y, lane):
    half = QK_ROPE // 2
    return jnp.where(lane % QK_ROPE < half, pltpu.roll(y, LANES - half, axis=1), pltpu.roll(y, half, axis=1))


def _rope_pair(x, gain2, cos4, sin4, lane):
    lo = lane < QK_ROPE
    ss = x * x
    s_lo = jnp.sum(jnp.where(lo, ss, 0.0), axis=-1, keepdims=True)
    s_hi = jnp.sum(jnp.where(lo, 0.0, ss), axis=-1, keepdims=True)
    r = jnp.where(lo, lax.rsqrt(s_lo * (1.0 / QK_ROPE) + EPS), lax.rsqrt(s_hi * (1.0 / QK_ROPE) + EPS))
    y = x * r * gain2
    return y * cos4 + _swap_halves(y, lane) * sin4


def _mla_prep_kernel(qn_ref, qr_ref, ckv_ref, misc_ref, cos_ref, sin_ref, gqn_ref, gqr_ref, gkn_ref, gkr_ref,
                     gc_ref, wuk_ref, *rest, with_vt, with_qg):
    rest = list(rest)
    wuvt_ref = rest.pop(0) if with_vt else None
    qcat_ref, kcat_ref, c_ref, kr_ref = rest[:4]
    rest = rest[4:]
    vt_ref = rest.pop(0) if with_vt else None
    qg_ref = rest.pop(0) if with_qg else None
    rows = qn_ref.shape[0]
    lane = lax.broadcasted_iota(jnp.int32, (rows, LANES), 1)
    lo = lane < QK_ROPE
    cos4 = cos_ref[...]
    sin4 = sin_ref[...]

    c = _rms_rows(ckv_ref[...], gc_ref[...])
    c_ref[...] = c
    cb = c.astype(BF16)
    kr = jnp.where(lo, _rope_pair(misc_ref[...], gkr_ref[...], cos4, sin4, lane), 0.0)
    kr_ref[...] = kr
    krb = kr.astype(BF16)

    gqn = gqn_ref[...]
    gkn = gkn_ref[...]
    for h in range(MLA_HEADS):
        qn = _rms_rows(qn_ref[:, h * QK_NOPE:(h + 1) * QK_NOPE], gqn)
        qcat_ref[:, h * QK_PAD:h * QK_PAD + QK_NOPE] = qn.astype(BF16)
        if with_qg:
            qg_ref[:, h * QK_NOPE:(h + 1) * QK_NOPE] = (qn * gkn).astype(BF16)
    gqr = gqr_ref[...]
    for j in range(MLA_HEADS // 2):
        rot = _rope_pair(qr_ref[:, j * LANES:(j + 1) * LANES], gqr, cos4, sin4, lane)
        even = jnp.where(lo, rot, 0.0)
        odd = jnp.where(lo, pltpu.roll(rot, QK_ROPE, axis=1), 0.0)
        qcat_ref[:, (2 * j) * QK_PAD + QK_NOPE:(2 * j + 1) * QK_PAD] = even.astype(BF16)
        qcat_ref[:, (2 * j + 1) * QK_PAD + QK_NOPE:(2 * j + 2) * QK_PAD] = odd.astype(BF16)

    kn = jnp.dot(cb, wuk_ref[...], preferred_element_type=F32)
    for h in range(MLA_HEADS):
        kh = _rms_rows(kn[:, h * QK_NOPE:(h + 1) * QK_NOPE], gkn)
        kcat_ref[:, h * QK_PAD:h * QK_PAD + QK_NOPE] = kh.astype(BF16)
        kcat_ref[:, h * QK_PAD + QK_NOPE:(h + 1) * QK_PAD] = krb
    if with_vt:
        vt_ref[...] = _nt_dot(wuvt_ref[...], cb).astype(BF16)


def _mla_prep(proj, cos4, sin4, gqn, gqr2, gkn, gkr2, gckv, w_uk_b, w_uvt_b, *, tm, with_vt, with_qg):
    n = proj.shape[0]
    tm = _tile(n, tm, 16)
    nt = n // tm
    hq = MLA_HEADS * QK_NOPE
    row = lambda w, off: pl.BlockSpec((tm, w), lambda i, off=off, w=w: (i, off // w))
    const = lambda a: pl.BlockSpec(a.shape, lambda i: (0,) * a.ndim)
    in_specs = [row(hq, SEG_QN), row(MLA_HEADS * QK_ROPE, SEG_QR), row(KV_LORA, SEG_CKV), row(LANES, SEG_MISC),
                pl.BlockSpec((tm, LANES), lambda i: (i, 0)), pl.BlockSpec((tm, LANES), lambda i: (i, 0)),
                const(gqn), const(gqr2), const(gkn), const(gkr2), const(gckv), const(w_uk_b)]
    args = [proj, proj, proj, proj, cos4, sin4, gqn, gqr2, gkn, gkr2, gckv, w_uk_b]
    if with_vt:
        in_specs.append(const(w_uvt_b))
        args.append(w_uvt_b)
    out_specs = [pl.BlockSpec((tm, MLA_HEADS * QK_PAD), lambda i: (i, 0)),
                 pl.BlockSpec((tm, MLA_HEADS * QK_PAD), lambda i: (i, 0)),
                 pl.BlockSpec((tm, KV_LORA), lambda i: (i, 0)),
                 pl.BlockSpec((tm, LANES), lambda i: (i, 0))]
    out_shape = [jax.ShapeDtypeStruct((n, MLA_HEADS * QK_PAD), BF16),
                 jax.ShapeDtypeStruct((n, MLA_HEADS * QK_PAD), BF16),
                 jax.ShapeDtypeStruct((n, KV_LORA), F32),
                 jax.ShapeDtypeStruct((n, LANES), F32)]
    if with_vt:
        out_specs.append(pl.BlockSpec((None, MLA_HEADS * V_HEAD, tm), lambda i: (i, 0, 0)))
        out_shape.append(jax.ShapeDtypeStruct((nt, MLA_HEADS * V_HEAD, tm), BF16))
    if with_qg:
        out_specs.append(pl.BlockSpec((tm, hq), lambda i: (i, 0)))
        out_shape.append(jax.ShapeDtypeStruct((n, hq), BF16))
    return pl.pallas_call(
        functools.partial(_mla_prep_kernel, with_vt=with_vt, with_qg=with_qg),
        grid=(nt,), in_specs=in_specs, out_specs=out_specs, out_shape=out_shape,
        compiler_params=_cparams(("arbitrary",)),
        name="mla_prep",
    )(*args)


def _flash_kernel(q_ref, k_ref, vt_ref, o_ref, m_ref, l_ref, acc_ref, *, tq, scale):
    qi = pl.program_id(2)
    q = q_ref[...]
    m_ref[...] = jnp.full(m_ref.shape, -jnp.inf, F32)
    l_ref[...] = jnp.zeros(l_ref.shape, F32)
    acc_ref[...] = jnp.zeros(acc_ref.shape, F32)

    c2 = scale * 1.4426950408889634

    def block(ki, nblk, masked):
        start = pl.multiple_of(ki * tq, tq)
        k = k_ref[pl.ds(start, nblk * tq), :]
        s = _nt_dot(k, q)
        if masked:
            key = lax.broadcasted_iota(jnp.int32, s.shape, 0)
            qry = lax.broadcasted_iota(jnp.int32, s.shape, 1)
            s = jnp.where(key <= qry, s, -jnp.inf)
        m_old = m_ref[...]
        m_new = jnp.maximum(m_old, jnp.max(s, axis=0, keepdims=True))
        alpha = jnp.exp2((m_old - m_new) * c2)
        p32 = jnp.exp2((s - m_new) * c2)
        l_ref[...] = l_ref[...] * alpha + jnp.sum(p32, axis=0, keepdims=True)
        p = p32.astype(BF16)
        pv = jnp.dot(vt_ref[ki], p[0:tq], preferred_element_type=F32)
        for j in range(1, nblk):
            pv = pv + jnp.dot(vt_ref[ki + j], p[j * tq:(j + 1) * tq], preferred_element_type=F32)
        acc_ref[...] = acc_ref[...] * alpha + pv
        m_ref[...] = m_new

    def body(kp, carry):
        block(2 * kp, 2, False)
        return carry

    lax.fori_loop(0, qi // 2, body, 0)

    @pl.when(qi % 2 == 1)
    def _():
        block(qi - 1, 1, False)

    block(qi, 1, True)
    o_ref[...] = (acc_ref[...] / l_ref[...]).T.astype(o_ref.dtype)


def _flash_attention(q_cat, k_cat, vt, *, batch, seq, tq):
    nq = seq // tq
    kern = functools.partial(_flash_kernel, tq=tq, scale=QK_HEAD ** -0.5)
    return pl.pallas_call(
        kern,
        grid=(batch, MLA_HEADS, nq),
        in_specs=[pl.BlockSpec((tq, QK_PAD), lambda b, h, i: (b * nq + i, h)),
                  pl.BlockSpec((seq, QK_PAD), lambda b, h, i: (b, h)),
                  pl.BlockSpec((nq, V_HEAD, tq), lambda b, h, i: (b, h, 0))],
        out_specs=pl.BlockSpec((tq, V_HEAD), lambda b, h, i: (b * nq + i, h)),
        out_shape=jax.ShapeDtypeStruct((batch * seq, MLA_HEADS * V_HEAD), BF16),
        scratch_shapes=[pltpu.VMEM((1, tq), F32), pltpu.VMEM((1, tq), F32), pltpu.VMEM((V_HEAD, tq), F32)],
        compiler_params=_cparams(("arbitrary", "arbitrary", "arbitrary")),
        name="mla_prompt_attention",
    )(q_cat, k_cat, vt)


PAGES_PER_STEP = 16
PAGE_GROUPS = 2


def _paged_kernel(pt_ref, *refs, scale, steps):
    npg = PAGES_PER_STEP
    c_refs = refs[:npg]
    krt_refs = refs[npg:2 * npg]
    (wukt_ref, qa_ref, qcat_ref, qcat_prev_ref, kcat_prev_ref, cn_prev_ref, o_ref,
     lhs_ref, c_scr, krt_scr, s_scr, m_ref, l_ref, acc_ref) = refs[2 * npg:]
    t = pl.program_id(0)
    n_hd = MLA_HEADS * QK_NOPE
    span = npg * PAGE_SIZE // PAGE_GROUPS
    prev_chunk = (t + steps - 1) % steps

    @pl.when(t == 0)
    def _():
        lhs_ref[0:n_hd, :] = wukt_ref[...]
        c_scr[1] = jnp.zeros(c_scr.shape[1:], BF16)
        s_scr[1] = jnp.zeros(s_scr.shape[1:], F32)
        m_ref[...] = jnp.zeros(m_ref.shape, F32)
        l_ref[...] = jnp.zeros(l_ref.shape, F32)
        acc_ref[...] = jnp.zeros(acc_ref.shape, F32)

    @pl.when(t % steps == 0)
    def _():
        lhs_ref[n_hd:n_hd + MLA_HEADS, :] = qa_ref[...]

    def step(cur, prv):
        first = prev_chunk == 0
        s_prev = s_scr[prv]
        m_old = jnp.where(first, -jnp.inf, m_ref[...])
        l_old = jnp.where(first, 0.0, l_ref[...])
        acc_old = jnp.where(first, 0.0, acc_ref[...])
        m_new = jnp.maximum(m_old, jnp.max(s_prev, axis=-1, keepdims=True))
        alpha = jnp.exp(m_old - m_new)
        pr = jnp.exp(s_prev - m_new)
        l_ref[...] = l_old * alpha + jnp.sum(pr, axis=-1, keepdims=True)
        acc_ref[...] = acc_old * alpha + jnp.dot(pr.astype(BF16), c_scr[prv], preferred_element_type=F32)
        m_ref[...] = m_new
        for i in range(npg):
            c_scr[cur, i * PAGE_SIZE:(i + 1) * PAGE_SIZE, :] = c_refs[i][...].astype(BF16)
            krt_scr[:, i * PAGE_SIZE:(i + 1) * PAGE_SIZE] = krt_refs[i][...].astype(BF16)
        qr = qcat_ref[:, QK_NOPE:QK_NOPE + QK_ROPE]
        for g in range(PAGE_GROUPS):
            c = c_scr[cur, g * span:(g + 1) * span, :]
            res = _nt_dot(lhs_ref[...], c)
            kt = res[0:n_hd, :].reshape(MLA_HEADS, QK_NOPE, span)
            inv = lax.rsqrt(jnp.sum(kt * kt, axis=1) * (1.0 / QK_NOPE) + EPS)
            s_rope = jnp.dot(qr, krt_scr[:, g * span:(g + 1) * span], preferred_element_type=F32)
            s_scr[cur, :, g * span:(g + 1) * span] = (res[n_hd:n_hd + MLA_HEADS, :] * inv + s_rope) * scale

    @pl.when(t % 2 == 0)
    def _():
        step(0, 1)

    @pl.when(t % 2 == 1)
    def _():
        step(1, 0)

    @pl.when((prev_chunk == steps - 1) & (t > 0))
    def _():
        s_new = jnp.sum(qcat_prev_ref[...].astype(F32) * kcat_prev_ref[...].astype(F32), axis=-1,
                        keepdims=True) * scale
        m_old = m_ref[...]
        m_fin = jnp.maximum(m_old, s_new)
        a2 = jnp.exp(m_old - m_fin)
        p_new = jnp.exp(s_new - m_fin)
        l_fin = l_ref[...] * a2 + p_new
        cn = cn_prev_ref[...].astype(BF16).astype(F32)
        o_ref[...] = (acc_ref[...] * a2 + p_new * cn) / l_fin


def _paged_attention(page_table, ckv_pool, krt_pool, wukt, qa, q_cat, k_cat, cn):
    n_s, n_pages = page_table.shape
    npg = PAGES_PER_STEP
    assert n_pages % npg == 0
    steps = n_pages // npg
    n_chunks = n_s * steps
    span = npg * PAGE_SIZE
    pt_flat = page_table.reshape(n_s * n_pages)

    def page_spec(shape, i):
        return pl.BlockSpec((None,) + shape,
                            lambda t, pt, i=i: (pt[jnp.minimum(t, n_chunks - 1) * npg + i], 0, 0))

    def cur_sample(shape):
        return pl.BlockSpec((None,) + shape, lambda t, pt: (jnp.minimum(t // steps, n_s - 1), 0, 0))

    def prev_sample(shape):
        return pl.BlockSpec((None,) + shape, lambda t, pt: (jnp.maximum(t - 1, 0) // steps, 0, 0))

    in_specs = ([page_spec((PAGE_SIZE, KV_LORA), i) for i in range(npg)]
                + [page_spec((QK_ROPE, PAGE_SIZE), i) for i in range(npg)]
                + [pl.BlockSpec(wukt.shape, lambda t, pt: (0, 0)),
                   cur_sample((MLA_HEADS, KV_LORA)), cur_sample((MLA_HEADS, QK_PAD)),
                   prev_sample((MLA_HEADS, QK_PAD)), prev_sample((MLA_HEADS, QK_PAD)), prev_sample((1, KV_LORA))])
    grid_spec = pltpu.PrefetchScalarGridSpec(
        num_scalar_prefetch=1,
        grid=(n_chunks + 1,),
        in_specs=in_specs,
        out_specs=prev_sample((MLA_HEADS, KV_LORA)),
        scratch_shapes=[pltpu.VMEM((MLA_HEADS * QK_NOPE + MLA_HEADS, KV_LORA), BF16),
                        pltpu.VMEM((2, span, KV_LORA), BF16), pltpu.VMEM((QK_ROPE, span), BF16),
                        pltpu.VMEM((2, MLA_HEADS, span), F32),
                        pltpu.VMEM((MLA_HEADS, 1), F32), pltpu.VMEM((MLA_HEADS, 1), F32),
                        pltpu.VMEM((MLA_HEADS, KV_LORA), F32)])
    return pl.pallas_call(
        functools.partial(_paged_kernel, scale=QK_HEAD ** -0.5, steps=steps),
        grid_spec=grid_spec,
        out_shape=jax.ShapeDtypeStruct((n_s, MLA_HEADS, KV_LORA), F32),
        compiler_params=_cparams(("arbitrary",)),
        name="mla_sample_attention",
    )(pt_flat, *([ckv_pool] * npg), *([krt_pool] * npg), wukt, qa, q_cat, q_cat, k_cat, cn)


def _gdn_prep_kernel(x_ref, halo_ref, misc_ref, cw_ref, alog_ref, dtb_ref, qkv_ref, kt_ref, gates_ref, xs_ref,
                     *, tiles_per_seq):
    i = pl.program_id(0)
    s = pl.program_id(1)
    tm = x_ref.shape[0]
    first = (i % tiles_per_seq) == 0

    @pl.when(first)
    def _():
        xs_ref[0:SUBLANES, :] = jnp.zeros((SUBLANES, xs_ref.shape[1]), F32)

    @pl.when(jnp.logical_not(first))
    def _():
        xs_ref[0:SUBLANES, :] = halo_ref[...]

    xs_ref[SUBLANES:SUBLANES + tm, :] = x_ref[...]
    acc = xs_ref[pl.ds(SUBLANES - (CONV_W - 1), tm), :] * cw_ref[0:1, :]
    for w in range(1, CONV_W):
        acc = acc + xs_ref[pl.ds(SUBLANES - (CONV_W - 1) + w, tm), :] * cw_ref[w:w + 1, :]
    y = acc * _sigmoid(acc)

    @pl.when(s == 2)
    def _():
        qkv_ref[...] = y

    @pl.when(s < 2)
    def _():
        scale = jnp.where(s == 0, GDN_DK ** -0.5, 1.0).astype(F32)
        for h in range(GDN_HEADS):
            yh = y[:, h * GDN_DK:(h + 1) * GDN_DK]
            yn = yh * (lax.rsqrt(jnp.sum(yh * yh, axis=-1, keepdims=True) + EPS) * scale)
            qkv_ref[:, h * GDN_DK:(h + 1) * GDN_DK] = yn

    @pl.when(s == 1)
    def _():
        kt_ref[...] = qkv_ref[...].T

    @pl.when(s == 0)
    def _():
        x = misc_ref[...] + dtb_ref[...]
        softplus = jnp.maximum(x, 0.0) + jnp.log(1.0 + jnp.exp(-jnp.abs(x)))
        lane = lax.broadcasted_iota(jnp.int32, x.shape, 1)
        is_beta = (lane >= MISC_B) & (lane < MISC_B + GDN_HEADS)
        gates_ref[...] = jnp.where(is_beta, _sigmoid(misc_ref[...]), -jnp.exp(alog_ref[...]) * softplus)


def _gdn_prep(proj, conv_w, alog_l, dtb_l, *, batch, seq, tm):
    n = batch * seq
    tm = _tile(seq, tm, LANES)
    tps = seq // tm
    sec0 = SEG_QKV // GDN_HK
    halo_blocks = tm // SUBLANES
    return pl.pallas_call(
        functools.partial(_gdn_prep_kernel, tiles_per_seq=tps),
        grid=(n // tm, 3),
        in_specs=[pl.BlockSpec((tm, GDN_HK), lambda i, s: (i, sec0 + s)),
                  pl.BlockSpec((SUBLANES, GDN_HK), lambda i, s: (jnp.maximum(i * halo_blocks - 1, 0), sec0 + s)),
                  pl.BlockSpec((tm, LANES), lambda i, s: (i, SEG_MISC // LANES)),
                  pl.BlockSpec((CONV_W, GDN_HK), lambda i, s: (0, s)),
                  pl.BlockSpec((1, LANES), lambda i, s: (0, 0)),
                  pl.BlockSpec((1, LANES), lambda i, s: (0, 0))],
        out_specs=[pl.BlockSpec((tm, GDN_HK), lambda i, s: (i, s)),
                   pl.BlockSpec((None, GDN_HK, tm), lambda i, s: (i // tps, 0, i % tps)),
                   pl.BlockSpec((tm, LANES), lambda i, s: (i, 0))],
        out_shape=[jax.ShapeDtypeStruct((n, 3 * GDN_HK), F32),
                   jax.ShapeDtypeStruct((batch, GDN_HK, seq), F32),
                   jax.ShapeDtypeStruct((n, LANES), F32)],
        scratch_shapes=[pltpu.VMEM((tm + SUBLANES, GDN_HK), F32)],
        compiler_params=_cparams(("arbitrary", "arbitrary")),
        name="gdn_prep",
    )(proj, proj, proj, conv_w, alog_l, dtb_l)


def _split3(x):
    hi = x.astype(BF16)
    r1 = x - hi.astype(F32)
    mid = r1.astype(BF16)
    lo = (r1 - mid.astype(F32)).astype(BF16)
    return hi, mid, lo


def _bdot(a, b):
    return jnp.dot(a.astype(BF16), b.astype(BF16), preferred_element_type=F32)


def _delta_kernel(q_ref, k_ref, v_ref, kt_ref, gates_ref, o_ref, s_out_ref, s_ref):
    c = GDN_CHUNK
    n = pl.program_id(1)
    heads = range(GDN_HEADS)

    @pl.when(n == 0)
    def _():
        s_ref[...] = jnp.zeros(s_ref.shape, F32)

    row = lax.broadcasted_iota(jnp.int32, (c, c), 0)
    col = lax.broadcasted_iota(jnp.int32, (c, c), 1)
    incl = row >= col
    strict = row > col
    eye = (row == col).astype(F32)
    tril = incl.astype(BF16)
    triu = (row <= col).astype(BF16)
    gates = gates_ref[...]
    gates_t = gates.T
    gc_all = sum(jnp.dot(tril, part, preferred_element_type=F32) for part in _split3(gates))
    gct_all = sum(jnp.dot(part, triu, preferred_element_type=F32) for part in _split3(gates_t))

    sl = [slice(h * GDN_DK, (h + 1) * GDN_DK) for h in heads]
    gcol = [gc_all[:, MISC_A + h:MISC_A + h + 1] for h in heads]
    grow = [gct_all[MISC_A + h:MISC_A + h + 1, :] for h in heads]
    glast = [gct_all[MISC_A + h:MISC_A + h + 1, c - 1:c] for h in heads]
    bcol = [gates[:, MISC_B + h:MISC_B + h + 1] for h in heads]
    eg = [jnp.exp(g) for g in gcol]
    e = [jnp.exp(jnp.where(incl, gcol[h] - grow[h], 0.0)) for h in heads]
    kb = [k_ref[:, sl[h]] * bcol[h] for h in heads]
    ktb = [kt_ref[sl[h], :].astype(BF16) for h in heads]
    x = [-(jnp.dot(kb[h].astype(BF16), ktb[h], preferred_element_type=F32) * jnp.where(strict, e[h], 0.0))
         for h in heads]
    t = [eye + x[h] for h in heads]
    for _ in range(6):
        x = [_bdot(x[h], x[h]) for h in heads]
        t = [t[h] + _bdot(t[h], x[h]) for h in heads]
    rhs = [jnp.concatenate([v_ref[:, sl[h]] * bcol[h], kb[h] * eg[h]], axis=1) for h in heads]
    sol = [_bdot(t[h], rhs[h]) for h in heads]
    attn = [jnp.dot(q_ref[:, sl[h]].astype(BF16), ktb[h], preferred_element_type=F32) * jnp.where(incl, e[h], 0.0)
            for h in heads]
    s_old = [s_ref[h] for h in heads]
    lhs = [jnp.concatenate([sol[h][:, GDN_DV:], q_ref[:, sl[h]] * eg[h]], axis=0) for h in heads]
    prod = [_bdot(lhs[h], s_old[h]) for h in heads]
    v_new = [sol[h][:, :GDN_DV] - prod[h][:c] for h in heads]
    for h in heads:
        o_ref[:, sl[h]] = prod[h][c:] + _bdot(attn[h], v_new[h])
    for h in heads:
        kdt = kt_ref[sl[h], :] * jnp.exp(glast[h] - grow[h])
        s_ref[h] = s_old[h] * jnp.exp(glast[h]) + _bdot(kdt, v_new[h])

    @pl.when(n == pl.num_programs(1) - 1)
    def _():
        s_out_ref[...] = s_ref[...]


def _delta_rule(qkv, kt, gates, *, batch, seq):
    c = GDN_CHUNK
    assert seq % c == 0
    nc = seq // c
    sec = lambda s: pl.BlockSpec((c, GDN_HK), lambda b, n, s=s: (b * nc + n, s))
    return pl.pallas_call(
        _delta_kernel,
        grid=(batch, nc),
        in_specs=[sec(0), sec(1), sec(2),
                  pl.BlockSpec((None, GDN_HK, c), lambda b, n: (b, 0, n)),
                  pl.BlockSpec((c, LANES), lambda b, n: (b * nc + n, 0))],
        out_specs=[pl.BlockSpec((c, GDN_HK), lambda b, n: (b * nc + n, 0)),
                   pl.BlockSpec((None, GDN_HEADS, GDN_DK, GDN_DV), lambda b, n: (b, 0, 0, 0))],
        out_shape=[jax.ShapeDtypeStruct((batch * seq, GDN_HK), F32),
                   jax.ShapeDtypeStruct((batch, GDN_HEADS, GDN_DK, GDN_DV), F32)],
        scratch_shapes=[pltpu.VMEM((GDN_HEADS, GDN_DK, GDN_DV), F32)],
        compiler_params=_cparams(("arbitrary", "arbitrary")),
        name="gdn_delta_rule",
    )(qkv, qkv, qkv, kt, gates)


def _merge_kernel(og_ref, z_ref, oa_ref, ga_ref, gb_ref, gn_ref, wpa_ref, wpb_ref, m_ref, ob_ref):
    @pl.when(pl.program_id(1) == 0)
    def _():
        gn = gn_ref[...]
        for h in range(GDN_HEADS):
            sl = slice(h * GDN_DV, (h + 1) * GDN_DV)
            z = z_ref[:, sl]
            ob_ref[:, sl] = (_rms_rows(og_ref[:, sl], gn) * (z * _sigmoid(z))).astype(BF16)

    pa = jnp.dot(oa_ref[...], wpa_ref[...], preferred_element_type=F32)
    pb = jnp.dot(ob_ref[...], wpb_ref[...], preferred_element_type=F32)
    m_ref[...] = (_sigmoid(ga_ref[...]) * pa + _sigmoid(gb_ref[...]) * pb).astype(BF16)


def _merge(o_g, oa, proj, gdn_norm, w_pa_b, w_pb_b, *, tm):
    n = o_g.shape[0]
    tm = _tile(n, tm, 16)
    tn = 1024
    nj = D_MODEL // tn
    return pl.pallas_call(
        _merge_kernel,
        grid=(n // tm, nj),
        in_specs=[pl.BlockSpec((tm, GDN_HK), lambda i, j: (i, 0)),
                  pl.BlockSpec((tm, GDN_HK), lambda i, j: (i, SEG_Z // GDN_HK)),
                  pl.BlockSpec((tm, MLA_HEADS * V_HEAD), lambda i, j: (i, 0)),
                  pl.BlockSpec((tm, tn), lambda i, j: (i, SEG_GA // tn + j)),
                  pl.BlockSpec((tm, tn), lambda i, j: (i, SEG_GB // tn + j)),
                  pl.BlockSpec((1, GDN_DV), lambda i, j: (0, 0)),
                  pl.BlockSpec((MLA_HEADS * V_HEAD, tn), lambda i, j: (0, j)),
                  pl.BlockSpec((GDN_HK, tn), lambda i, j: (0, j))],
        out_specs=pl.BlockSpec((tm, tn), lambda i, j: (i, j)),
        out_shape=jax.ShapeDtypeStruct((n, D_MODEL), BF16),
        scratch_shapes=[pltpu.VMEM((tm, GDN_HK), BF16)],
        compiler_params=_cparams(("arbitrary", "arbitrary")),
        name="gated_merge",
    )(o_g, proj, oa, proj, proj, gdn_norm, w_pa_b, w_pb_b)


def _out_proj_kernel(mp_ref, xp_ref, ms_ref, xs_ref, wo_ref, gf_ref, wrt_ref, h_ref, hn_ref, lg_ref, *, np_tiles):
    i = pl.program_id(0)

    def rows(m_ref, x_ref, n):
        h = x_ref[...] + jnp.dot(m_ref[...], wo_ref[...], preferred_element_type=F32)
        h_ref[0:n, :] = h
        hn = _rms_rows(h, gf_ref[...]).astype(BF16)
        hn_ref[0:n, :] = hn
        lg_ref[0:n, :] = jnp.dot(hn, wrt_ref[...], preferred_element_type=F32)

    @pl.when(i < np_tiles)
    def _():
        rows(mp_ref, xp_ref, mp_ref.shape[0])

    @pl.when(i == np_tiles)
    def _():
        rows(ms_ref, xs_ref, ms_ref.shape[0])


def _out_proj(m_p, x_p, m_s, x_s, w_o_b, norm_ffn, w_rt_b, *, tm):
    n_p, n_s = m_p.shape[0], m_s.shape[0]
    tm = _tile(n_p, tm, 16)
    np_tiles = n_p // tm
    assert n_s <= tm
    const = lambda a: pl.BlockSpec(a.shape, lambda i: (0,) * a.ndim)
    p_spec = pl.BlockSpec((tm, D_MODEL), lambda i: (jnp.minimum(i, np_tiles - 1), 0))
    out = lambda w: pl.BlockSpec((tm, w), lambda i: (i, 0))
    return pl.pallas_call(
        functools.partial(_out_proj_kernel, np_tiles=np_tiles),
        grid=(np_tiles + 1,),
        in_specs=[p_spec, p_spec, const(m_s), const(x_s), const(w_o_b), const(norm_ffn), const(w_rt_b)],
        out_specs=[out(D_MODEL), out(D_MODEL), out(LANES)],
        out_shape=[jax.ShapeDtypeStruct((n_p + n_s, D_MODEL), F32),
                   jax.ShapeDtypeStruct((n_p + n_s, D_MODEL), BF16),
                   jax.ShapeDtypeStruct((n_p + n_s, LANES), F32)],
        compiler_params=_cparams(("arbitrary",)),
        name="out_proj_router",
    )(m_p, x_p, m_s, x_s, w_o_b, norm_ffn, w_rt_b)


def _expert_weights(sched_ref, w_hbm, wbuf, wb_ref, sem):
    i = pl.program_id(0)

    def fetch(expert, slot):
        return pltpu.make_async_copy(w_hbm.at[expert], wbuf.at[slot], sem.at[slot])

    @pl.when(i == 0)
    def _():
        fetch(sched_ref[0, 0], 0).start()

    @pl.when(sched_ref[2, i] != 0)
    def _():
        slot = sched_ref[3, i]
        fetch(sched_ref[0, i], slot).wait()

        @pl.when(sched_ref[4, i] >= 0)
        def _():
            fetch(sched_ref[4, i], 1 - slot).start()

        wb_ref[...] = wbuf[slot].astype(BF16)


def _gmm_kernel(sched_ref, x_ref, w_hbm, o_ref, wbuf, wb_ref, sem):
    i = pl.program_id(0)
    _expert_weights(sched_ref, w_hbm, wbuf, wb_ref, sem)

    @pl.when(sched_ref[1, i] != 0)
    def _():
        o_ref[...] = jnp.dot(x_ref[...], wb_ref[...], preferred_element_type=F32).astype(o_ref.dtype)

    @pl.when(sched_ref[1, i] == 0)
    def _():
        o_ref[...] = jnp.zeros(o_ref.shape, o_ref.dtype)


def _gmm_down_kernel(sched_ref, g_ref, u_ref, w_hbm, o_ref, wbuf, wb_ref, sem):
    i = pl.program_id(0)
    _expert_weights(sched_ref, w_hbm, wbuf, wb_ref, sem)

    @pl.when(sched_ref[1, i] != 0)
    def _():
        gate = g_ref[...]
        act = gate * _sigmoid(gate) * u_ref[...]
        o_ref[...] = jnp.dot(act.astype(BF16), wb_ref[...], preferred_element_type=F32)

    @pl.when(sched_ref[1, i] == 0)
    def _():
        o_ref[...] = jnp.zeros(o_ref.shape, o_ref.dtype)


def _expert_schedule(bexp, used):
    nb = bexp.shape[0]
    first = jnp.concatenate([jnp.ones((1,), jnp.int32), (bexp[1:] != bexp[:-1]).astype(jnp.int32)])
    run = jnp.cumsum(first) - 1
    run_expert = jnp.zeros((nb,), jnp.int32).at[run].set(bexp)
    nxt = jnp.where(run + 1 <= run[-1], run_expert[jnp.minimum(run + 1, nb - 1)], -1)
    return jnp.stack([bexp, used, first, run % 2, nxt]).astype(jnp.int32)


def _grouped_call(kern, sched, row_inputs, w, *, name):
    rows = row_inputs[0].shape[0]
    k, n = w.shape[1], w.shape[2]
    nb = rows // MOE_ROWS
    grid_spec = pltpu.PrefetchScalarGridSpec(
        num_scalar_prefetch=1,
        grid=(nb,),
        in_specs=[pl.BlockSpec((MOE_ROWS, a.shape[1]), lambda i, sc: (i, 0)) for a in row_inputs]
        + [pl.BlockSpec(memory_space=pl.ANY)],
        out_specs=pl.BlockSpec((MOE_ROWS, n), lambda i, sc: (i, 0)),
        scratch_shapes=[pltpu.VMEM((2, k, n), F32), pltpu.VMEM((k, n), BF16), pltpu.SemaphoreType.DMA((2,))])
    return pl.pallas_call(
        kern, grid_spec=grid_spec,
        out_shape=jax.ShapeDtypeStruct((rows, n), F32),
        compiler_params=_cparams(("arbitrary",)),
        name=name,
    )(sched, *row_inputs, w)


def _l2n(x):
    return x * lax.rsqrt(jnp.sum(x * x, axis=-1, keepdims=True) + EPS)


def _rms(x, g):
    return x * lax.rsqrt(jnp.mean(x * x, axis=-1, keepdims=True) + EPS) * g


def _rope_tables(pos):
    inv = ROPE_THETA ** (-jnp.arange(0, QK_ROPE, 2, dtype=F32) / QK_ROPE)
    ang = pos[:, None] * inv[None, :]
    cos, sin = jnp.cos(ang), jnp.sin(ang)
    return jnp.concatenate([cos, cos, cos, cos], axis=1), jnp.concatenate([-sin, sin, -sin, sin], axis=1)


def _reorder_w_in(w_in):
    o = 0
    wq = w_in[:, o:o + MLA_HEADS * QK_HEAD].reshape(D_MODEL, MLA_HEADS, QK_HEAD); o += MLA_HEADS * QK_HEAD
    wckv = w_in[:, o:o + KV_LORA]; o += KV_LORA
    wkr = w_in[:, o:o + QK_ROPE]; o += QK_ROPE
    wqkv = w_in[:, o:o + GDN_QKV_W]; o += GDN_QKV_W
    wa = w_in[:, o:o + GDN_HEADS]; o += GDN_HEADS
    wb = w_in[:, o:o + GDN_HEADS]; o += GDN_HEADS
    wz = w_in[:, o:o + GDN_HEADS * GDN_DV]; o += GDN_HEADS * GDN_DV
    wga = w_in[:, o:o + D_MODEL]; o += D_MODEL
    wgb = w_in[:, o:o + D_MODEL]
    pad = jnp.zeros((D_MODEL, D_PROJ - SEG_MISC - QK_ROPE - 2 * GDN_HEADS), w_in.dtype)
    return jnp.concatenate([
        wz, wga, wgb, wqkv, wq[:, :, :QK_NOPE].reshape(D_MODEL, -1), wq[:, :, QK_NOPE:].reshape(D_MODEL, -1),
        wckv, wkr, wa, wb, pad], axis=1).astype(BF16)


def _lane_vec(v, offset):
    return jnp.zeros((1, LANES), F32).at[0, offset:offset + v.shape[0]].set(v.astype(F32))


def _hier_moe(h, hn, logits, b_rg, b_re, w_gate, w_up, w_down):
    t = h.shape[0]
    n_rt = N_GROUPS + N_EXPERTS
    g_logits = logits[:, :N_GROUPS] + b_rg.astype(F32)
    p_group = jax.nn.softmax(g_logits, axis=-1)
    grp = jnp.argmax(g_logits, axis=-1).astype(jnp.int32)
    p_grp_sel = jnp.take_along_axis(p_group, grp[:, None], axis=-1)[:, 0]
    e_logits = (logits[:, N_GROUPS:n_rt] + b_re.astype(F32)).reshape(t, N_GROUPS, EXP_PER_GROUP)
    e_sel = jnp.take_along_axis(e_logits, grp[:, None, None], axis=1)[:, 0]
    top_w, top_i = lax.top_k(jax.nn.softmax(e_sel, axis=-1), TOP_K)
    top_w = top_w / jnp.sum(top_w, axis=-1, keepdims=True) * p_grp_sel[:, None]
    expert = grp[:, None] * EXP_PER_GROUP + top_i.astype(jnp.int32)
    n_assign = t * TOP_K
    e_flat = expert.reshape(n_assign)
    w_flat = top_w.reshape(n_assign)
    onehot = (e_flat[:, None] == jnp.arange(N_EXPERTS, dtype=jnp.int32)[None, :]).astype(jnp.int32)
    csum = jnp.cumsum(onehot, axis=0)
    rank = jnp.take_along_axis(csum, e_flat[:, None], axis=1)[:, 0] - 1
    counts = csum[-1]
    pcounts = (counts + MOE_ROWS - 1) // MOE_ROWS * MOE_ROWS
    pends = jnp.cumsum(pcounts)
    pstarts = pends - pcounts
    dest = pstarts[e_flat] + rank
    n_blocks = -(-n_assign // MOE_ROWS) + N_EXPERTS
    n_rows = n_blocks * MOE_ROWS
    tok = jnp.arange(n_assign, dtype=jnp.int32) // TOP_K
    src = jnp.zeros((n_rows,), jnp.int32).at[dest].set(tok)
    xb = hn[src]
    bstart = jnp.arange(n_blocks, dtype=jnp.int32) * MOE_ROWS
    bexp = jnp.minimum(jnp.searchsorted(pends, bstart, side='right'), N_EXPERTS - 1).astype(jnp.int32)
    used = (bstart < pends[-1]).astype(jnp.int32)
    sched = _expert_schedule(bexp, used)
    gate = _grouped_call(_gmm_kernel, sched, [xb], w_gate, name="moe_gate")
    up = _grouped_call(_gmm_kernel, sched, [xb], w_up, name="moe_up")
    yb = _grouped_call(_gmm_down_kernel, sched, [gate, up], w_down, name="moe_down")
    y = (yb[dest] * w_flat[:, None]).reshape(t, TOP_K, D_MODEL)
    return h + (y[:, 0] + y[:, 1])


def kernel(x_prompt, x_sample, cache_ckv, cache_krope, state_ssm, state_conv, page_table, norm_attn, w_in, norm_ckv, w_uk, w_uv, q_gain_nope, q_gain_rope, k_gain_nope, k_gain_rope, conv_w, gdn_a_log, gdn_dt_bias, gdn_norm, w_pa, w_pb, w_o, norm_ffn, w_router_group, b_router_group, w_router_expert, b_router_expert, w_gate, w_up, w_down):
    depth = w_in.shape[0]
    bsz, seq, _ = x_prompt.shape
    n_s, s_new, _ = x_sample.shape
    assert s_new == 1
    n_p = bsz * seq
    n_all = n_p + n_s
    n_past = page_table.shape[1] * PAGE_SIZE
    cos_p, sin_p = _rope_tables(jnp.tile(jnp.arange(seq, dtype=F32), bsz))
    cos_s, sin_s = _rope_tables(jnp.full((n_s,), n_past, F32))
    tq = _tile(seq, 512, LANES)
    xp = x_prompt.reshape(n_p, D_MODEL)
    xs = x_sample.reshape(n_s, D_MODEL)
    outs = [[] for _ in range(8)]
    row = lambda v: v.astype(F32)[None, :]
    for l in range(depth):
        w_in_r = _reorder_w_in(w_in[l])
        w_uk_b = w_uk[l].astype(BF16)
        w_uvt_b = w_uv[l].T.astype(BF16)
        wukt = w_uk_b.T
        w_pa_b, w_pb_b, w_o_b = w_pa[l].astype(BF16), w_pb[l].astype(BF16), w_o[l].astype(BF16)
        w_rt_b = jnp.concatenate([w_router_group[l], w_router_expert[l],
                                  jnp.zeros((D_MODEL, LANES - N_GROUPS - N_EXPERTS), F32)], axis=1).astype(BF16)
        gains = (row(q_gain_nope[l]), jnp.tile(row(q_gain_rope[l]), (1, 2)), row(k_gain_nope[l]),
                 jnp.tile(row(k_gain_rope[l]), (1, 2)), row(norm_ckv[l]))
        alog_l = _lane_vec(gdn_a_log[l], MISC_A)
        dtb_l = _lane_vec(gdn_dt_bias[l], MISC_A)
        proj_p = _in_proj(xp, row(norm_attn[l]), w_in_r, tm=1024)
        proj_s = _in_proj(xs, row(norm_attn[l]), w_in_r, tm=n_s)

        qcat_p, kcat_p, c_p, kr_p, vt_p = _mla_prep(proj_p, cos_p, sin_p, *gains, w_uk_b, w_uvt_b,
                                                    tm=tq, with_vt=True, with_qg=False)
        oa_p = _flash_attention(qcat_p, kcat_p, vt_p, batch=bsz, seq=seq, tq=tq)

        qcat_s, kcat_s, c_s, kr_s, qg_s = _mla_prep(proj_s, cos_s, sin_s, *gains, w_uk_b, w_uvt_b,
                                                    tm=n_s, with_vt=False, with_qg=True)
        qa = _headwise_matmul(qg_s, wukt, heads=MLA_HEADS, b_rows_by_head=True, name="q_absorb")
        o_lat = _paged_attention(page_table, cache_ckv[l], jnp.swapaxes(cache_krope[l], 1, 2), wukt,
                                 qa.astype(BF16).reshape(n_s, MLA_HEADS, KV_LORA),
                                 qcat_s.reshape(n_s, MLA_HEADS, QK_PAD), kcat_s.reshape(n_s, MLA_HEADS, QK_PAD),
                                 c_s[:, None, :])
        oa_s = _headwise_matmul(o_lat.astype(BF16).reshape(n_s, MLA_HEADS * KV_LORA), w_uv[l].astype(BF16),
                                heads=MLA_HEADS, b_rows_by_head=False, name="v_absorb").astype(BF16)

        qkv_p, kt_p, gates_p = _gdn_prep(proj_p, conv_w[l], alog_l, dtb_l, batch=bsz, seq=seq, tm=512)
        o_g, ssm_p = _delta_rule(qkv_p, kt_p, gates_p, batch=bsz, seq=seq)
        conv_p = proj_p.reshape(bsz, seq, D_PROJ)[:, seq - (CONV_W - 1):, SEG_QKV:SEG_QKV + GDN_QKV_W]

        qkv_raw_s = proj_s[:, SEG_QKV:SEG_QKV + GDN_QKV_W]
        xpad_s = jnp.concatenate([state_conv[l].astype(F32), qkv_raw_s[:, None, :]], axis=1)
        acc_s = xpad_s[:, 0] * conv_w[l, 0]
        for w in range(1, CONV_W):
            acc_s = acc_s + xpad_s[:, w] * conv_w[l, w]
        qkv_s = jax.nn.silu(acc_s)
        conv_s = xpad_s[:, 1:]
        q_s = _l2n(qkv_s[:, :GDN_HK].reshape(n_s, GDN_HEADS, GDN_DK)) * (GDN_DK ** -0.5)
        k_s = _l2n(qkv_s[:, GDN_HK:2 * GDN_HK].reshape(n_s, GDN_HEADS, GDN_DK))
        v_s = qkv_s[:, 2 * GDN_HK:].reshape(n_s, GDN_HEADS, GDN_DV)
        a_s = proj_s[:, SEG_MISC + MISC_A:SEG_MISC + MISC_A + GDN_HEADS]
        b_s = proj_s[:, SEG_MISC + MISC_B:SEG_MISC + MISC_B + GDN_HEADS]
        g_s = -jnp.exp(gdn_a_log[l].astype(F32)) * jax.nn.softplus(a_s + gdn_dt_bias[l].astype(F32))
        beta_s = jax.nn.sigmoid(b_s)
        eg = jnp.exp(g_s)[..., None]
        s0 = state_ssm[l].astype(F32)
        kcd = k_s * beta_s[..., None] * eg
        v_new = v_s * beta_s[..., None] - jnp.einsum('bhk,bhkv->bhv', kcd, s0)
        qk = jnp.einsum('bhd,bhd->bh', q_s, k_s)[..., None]
        o_s = jnp.einsum('bhk,bhkv->bhv', q_s * eg, s0) + qk * v_new
        ssm_s = s0 * eg[..., None] + jnp.einsum('bhk,bhv->bhkv', k_s, v_new)

        m_p = _merge(o_g, oa_p, proj_p, row(gdn_norm[l]), w_pa_b, w_pb_b, tm=512)
        m_s = _merge(o_s.reshape(n_s, GDN_HK), oa_s, proj_s, row(gdn_norm[l]), w_pa_b, w_pb_b, tm=n_s)
        h_all, hn_all, lg_all = _out_proj(m_p, xp, m_s, xs, w_o_b, row(norm_ffn[l]), w_rt_b, tm=512)
        y_all = _hier_moe(h_all, hn_all, lg_all, b_router_group[l], b_router_expert[l],
                          w_gate[l], w_up[l], w_down[l])
        xp, xs = y_all[:n_p], y_all[n_p:]

        for lst, val in zip(outs, (c_p.reshape(bsz, seq, KV_LORA), kr_p[:, :QK_ROPE].reshape(bsz, seq, QK_ROPE),
                                   c_s.reshape(n_s, 1, KV_LORA), kr_s[:, :QK_ROPE].reshape(n_s, 1, QK_ROPE),
                                   ssm_p, conv_p, ssm_s, conv_s)):
            lst.append(val)
    return (xp.reshape(bsz, seq, D_MODEL), xs.reshape(n_s, 1, D_MODEL)) + tuple(jnp.stack(o) for o in outs)
```

```python
import functools

import jax
import jax.numpy as jnp
from jax import lax
from jax.experimental import pallas as pl
from jax.experimental.pallas import tpu as pltpu

F32 = jnp.float32
BF16 = jnp.bfloat16

D_MODEL = 2048
PAGE_SIZE = 128
MLA_HEADS = 16
QK_NOPE = 128
QK_ROPE = 64
QK_HEAD = QK_NOPE + QK_ROPE
QK_PAD = 256
V_HEAD = 128
KV_LORA = 512
ROPE_THETA = 10000.0
GDN_HEADS = 16
GDN_DK = 128
GDN_DV = 128
GDN_HK = GDN_HEADS * GDN_DK
GDN_QKV_W = GDN_HEADS * (2 * GDN_DK + GDN_DV)
CONV_W = 4
GDN_CHUNK = 128
N_GROUPS = 8
EXP_PER_GROUP = 8
N_EXPERTS = N_GROUPS * EXP_PER_GROUP
TOP_K = 2
D_EXPERT = 1408
MOE_ROWS = 128
EPS = 1e-6
LANES = 128
SUBLANES = 8

V7X_VMEM_LIMIT = 56 * 1024 * 1024

SEG_Z = 0
SEG_GA = SEG_Z + GDN_HEADS * GDN_DV
SEG_GB = SEG_GA + D_MODEL
SEG_QKV = SEG_GB + D_MODEL
SEG_QN = SEG_QKV + GDN_QKV_W
SEG_QR = SEG_QN + MLA_HEADS * QK_NOPE
SEG_CKV = SEG_QR + MLA_HEADS * QK_ROPE
SEG_MISC = SEG_CKV + KV_LORA
PROJ_TN = 1024
D_PROJ = -(-(SEG_MISC + LANES) // PROJ_TN) * PROJ_TN
MISC_A = QK_ROPE
MISC_B = QK_ROPE + GDN_HEADS


def _cparams(sem, vmem=V7X_VMEM_LIMIT):
    return pltpu.CompilerParams(dimension_semantics=sem, vmem_limit_bytes=vmem)


def _tile(dim, cap, align):
    if dim <= cap:
        return dim
    for t in range(cap - cap % align, 0, -align):
        if dim % t == 0:
            return t
    raise ValueError(f"no {align}-aligned tile <= {cap} divides {dim}")


def _nt_dot(a, b):
    return lax.dot_general(a, b, (((1,), (1,)), ((), ())), preferred_element_type=F32)


def _rms_rows(x, gain):
    return x * lax.rsqrt(jnp.mean(x * x, axis=-1, keepdims=True) + EPS) * gain


def _sigmoid(x):
    return 1.0 / (1.0 + jnp.exp(-x))


def _in_proj_kernel(x_ref, g_ref, w_ref, o_ref, u_ref):
    @pl.when(pl.program_id(1) == 0)
    def _():
        u_ref[...] = _rms_rows(x_ref[...], g_ref[...]).astype(BF16)

    o_ref[...] = jnp.dot(u_ref[...], w_ref[...], preferred_element_type=F32)


def _in_proj(x, gain, w, *, tm):
    m, d = x.shape
    n = w.shape[1]
    tm = _tile(m, tm, 16)
    tn = PROJ_TN
    assert n % tn == 0
    return pl.pallas_call(
        _in_proj_kernel,
        grid=(m // tm, n // tn),
        in_specs=[pl.BlockSpec((tm, d), lambda i, j: (i, 0)),
                  pl.BlockSpec((1, d), lambda i, j: (0, 0)),
                  pl.BlockSpec((d, tn), lambda i, j: (0, j))],
        out_specs=pl.BlockSpec((tm, tn), lambda i, j: (i, j)),
        out_shape=jax.ShapeDtypeStruct((m, n), F32),
        scratch_shapes=[pltpu.VMEM((tm, d), BF16)],
        compiler_params=_cparams(("arbitrary", "arbitrary")),
        name="in_proj",
    )(x, gain, w)


def _mm_kernel(a_ref, b_ref, o_ref):
    o_ref[...] = jnp.dot(a_ref[...], b_ref[...], preferred_element_type=F32).astype(o_ref.dtype)


def _headwise_matmul(a, b, *, heads, b_rows_by_head, name):
    m = a.shape[0]
    k = a.shape[1] // heads
    if b_rows_by_head:
        n = b.shape[1]
        b_spec = pl.BlockSpec((k, n), lambda h: (h, 0))
    else:
        n = b.shape[1] // heads
        b_spec = pl.BlockSpec((k, n), lambda h: (0, h))
    return pl.pallas_call(
        _mm_kernel,
        grid=(heads,),
        in_specs=[pl.BlockSpec((m, k), lambda h: (0, h)), b_spec],
        out_specs=pl.BlockSpec((m, n), lambda h: (0, h)),
        out_shape=jax.ShapeDtypeStruct((m, heads * n), F32),
        compiler_params=_cparams(("arbitrary",)),
        name=name,
    )(a, b)


def _swap_halves(y, lane):
    half = QK_ROPE // 2
    return jnp.where(lane % QK_ROPE < half, pltpu.roll(y, LANES - half, axis=1), pltpu.roll(y, half, axis=1))


def _rope_pair(x, gain2, cos4, sin4, lane):
    lo = lane < QK_ROPE
    ss = x * x
    s_lo = jnp.sum(jnp.where(lo, ss, 0.0), axis=-1, keepdims=True)
    s_hi = jnp.sum(jnp.where(lo, 0.0, ss), axis=-1, keepdims=True)
    r = jnp.where(lo, lax.rsqrt(s_lo * (1.0 / QK_ROPE) + EPS), lax.rsqrt(s_hi * (1.0 / QK_ROPE) + EPS))
    y = x * r * gain2
    return y * cos4 + _swap_halves(y, lane) * sin4


def _mla_prep_kernel(qn_ref, qr_ref, ckv_ref, misc_ref, cos_ref, sin_ref, gqn_ref, gqr_ref, gkn_ref, gkr_ref,
                     gc_ref, wuk_ref, *rest, with_vt, with_qg):
    rest = list(rest)
    wuvt_ref = rest.pop(0) if with_vt else None
    qcat_ref, kcat_ref, c_ref, kr_ref = rest[:4]
    rest = rest[4:]
    vt_ref = rest.pop(0) if with_vt else None
    qg_ref = rest.pop(0) if with_qg else None
    rows = qn_ref.shape[0]
    lane = lax.broadcasted_iota(jnp.int32, (rows, LANES), 1)
    lo = lane < QK_ROPE
    cos4 = cos_ref[...]
    sin4 = sin_ref[...]

    c = _rms_rows(ckv_ref[...], gc_ref[...])
    c_ref[...] = c
    cb = c.astype(BF16)
    kr = jnp.where(lo, _rope_pair(misc_ref[...], gkr_ref[...], cos4, sin4, lane), 0.0)
    kr_ref[...] = kr
    krb = kr.astype(BF16)

    gqn = gqn_ref[...]
    gkn = gkn_ref[...]
    for h in range(MLA_HEADS):
        qn = _rms_rows(qn_ref[:, h * QK_NOPE:(h + 1) * QK_NOPE], gqn)
        qcat_ref[:, h * QK_PAD:h * QK_PAD + QK_NOPE] = qn.astype(BF16)
        if with_qg:
            qg_ref[:, h * QK_NOPE:(h + 1) * QK_NOPE] = (qn * gkn).astype(BF16)
    gqr = gqr_ref[...]
    for j in range(MLA_HEADS // 2):
        rot = _rope_pair(qr_ref[:, j * LANES:(j + 1) * LANES], gqr, cos4, sin4, lane)
        even = jnp.where(lo, rot, 0.0)
        odd = jnp.where(lo, pltpu.roll(rot, QK_ROPE, axis=1), 0.0)
        qcat_ref[:, (2 * j) * QK_PAD + QK_NOPE:(2 * j + 1) * QK_PAD] = even.astype(BF16)
        qcat_ref[:, (2 * j + 1) * QK_PAD + QK_NOPE:(2 * j + 2) * QK_PAD] = odd.astype(BF16)

    kn = jnp.dot(cb, wuk_ref[...], preferred_element_type=F32)
    for h in range(MLA_HEADS):
        kh = _rms_rows(kn[:, h * QK_NOPE:(h + 1) * QK_NOPE], gkn)
        kcat_ref[:, h * QK_PAD:h * QK_PAD + QK_NOPE] = kh.astype(BF16)
        kcat_ref[:, h * QK_PAD + QK_NOPE:(h + 1) * QK_PAD] = krb
    if with_vt:
        vt_ref[...] = _nt_dot(wuvt_ref[...], cb).astype(BF16)


def _mla_prep(proj, cos4, sin4, gqn, gqr2, gkn, gkr2, gckv, w_uk_b, w_uvt_b, *, tm, with_vt, with_qg):
    n = proj.shape[0]
    tm = _tile(n, tm, 16)
    nt = n // tm
    hq = MLA_HEADS * QK_NOPE
    row = lambda w, off: pl.BlockSpec((tm, w), lambda i, off=off, w=w: (i, off // w))
    const = lambda a: pl.BlockSpec(a.shape, lambda i: (0,) * a.ndim)
    in_specs = [row(hq, SEG_QN), row(MLA_HEADS * QK_ROPE, SEG_QR), row(KV_LORA, SEG_CKV), row(LANES, SEG_MISC),
                pl.BlockSpec((tm, LANES), lambda i: (i, 0)), pl.BlockSpec((tm, LANES), lambda i: (i, 0)),
                const(gqn), const(gqr2), const(gkn), const(gkr2), const(gckv), const(w_uk_b)]
    args = [proj, proj, proj, proj, cos4, sin4, gqn, gqr2, gkn, gkr2, gckv, w_uk_b]
    if with_vt:
        in_specs.append(const(w_uvt_b))
        args.append(w_uvt_b)
    out_specs = [pl.BlockSpec((tm, MLA_HEADS * QK_PAD), lambda i: (i, 0)),
                 pl.BlockSpec((tm, MLA_HEADS * QK_PAD), lambda i: (i, 0)),
                 pl.BlockSpec((tm, KV_LORA), lambda i: (i, 0)),
                 pl.BlockSpec((tm, LANES), lambda i: (i, 0))]
    out_shape = [jax.ShapeDtypeStruct((n, MLA_HEADS * QK_PAD), BF16),
                 jax.ShapeDtypeStruct((n, MLA_HEADS * QK_PAD), BF16),
                 jax.ShapeDtypeStruct((n, KV_LORA), F32),
                 jax.ShapeDtypeStruct((n, LANES), F32)]
    if with_vt:
        out_specs.append(pl.BlockSpec((None, MLA_HEADS * V_HEAD, tm), lambda i: (i, 0, 0)))
        out_shape.append(jax.ShapeDtypeStruct((nt, MLA_HEADS * V_HEAD, tm), BF16))
    if with_qg:
        out_specs.append(pl.BlockSpec((tm, hq), lambda i: (i, 0)))
        out_shape.append(jax.ShapeDtypeStruct((n, hq), BF16))
    return pl.pallas_call(
        functools.partial(_mla_prep_kernel, with_vt=with_vt, with_qg=with_qg),
        grid=(nt,), in_specs=in_specs, out_specs=out_specs, out_shape=out_shape,
        compiler_params=_cparams(("arbitrary",)),
        name="mla_prep",
    )(*args)


def _flash_kernel(q_ref, k_ref, vt_ref, o_ref, m_ref, l_ref, acc_ref, *, tq, scale):
    qi = pl.program_id(2)
    q = q_ref[...]
    m_ref[...] = jnp.full(m_ref.shape, -jnp.inf, F32)
    l_ref[...] = jnp.zeros(l_ref.shape, F32)
    acc_ref[...] = jnp.zeros(acc_ref.shape, F32)

    c2 = scale * 1.4426950408889634

    def block(ki, nblk, masked):
        start = pl.multiple_of(ki * tq, tq)
        k = k_ref[pl.ds(start, nblk * tq), :]
        s = _nt_dot(k, q)
        if masked:
            key = lax.broadcasted_iota(jnp.int32, s.shape, 0)
            qry = lax.broadcasted_iota(jnp.int32, s.shape, 1)
            s = jnp.where(key <= qry, s, -jnp.inf)
        m_old = m_ref[...]
        m_new = jnp.maximum(m_old, jnp.max(s, axis=0, keepdims=True))
        alpha = jnp.exp2((m_old - m_new) * c2)
        p32 = jnp.exp2((s - m_new) * c2)
        l_ref[...] = l_ref[...] * alpha + jnp.sum(p32, axis=0, keepdims=True)
        p = p32.astype(BF16)
        pv = jnp.dot(vt_ref[ki], p[0:tq], preferred_element_type=F32)
        for j in range(1, nblk):
            pv = pv + jnp.dot(vt_ref[ki + j], p[j * tq:(j + 1) * tq], preferred_element_type=F32)
        acc_ref[...] = acc_ref[...] * alpha + pv
        m_ref[...] = m_new

    def body(kp, carry):
        block(2 * kp, 2, False)
        return carry

    lax.fori_loop(0, qi // 2, body, 0)

    @pl.when(qi % 2 == 1)
    def _():
        block(qi - 1, 1, False)

    block(qi, 1, True)
    o_ref[...] = (acc_ref[...] / l_ref[...]).T.astype(o_ref.dtype)


def _flash_attention(q_cat, k_cat, vt, *, batch, seq, tq):
    nq = seq // tq
    kern = functools.partial(_flash_kernel, tq=tq, scale=QK_HEAD ** -0.5)
    return pl.pallas_call(
        kern,
        grid=(batch, MLA_HEADS, nq),
        in_specs=[pl.BlockSpec((tq, QK_PAD), lambda b, h, i: (b * nq + i, h)),
                  pl.BlockSpec((seq, QK_PAD), lambda b, h, i: (b, h)),
                  pl.BlockSpec((nq, V_HEAD, tq), lambda b, h, i: (b, h, 0))],
        out_specs=pl.BlockSpec((tq, V_HEAD), lambda b, h, i: (b * nq + i, h)),
        out_shape=jax.ShapeDtypeStruct((batch * seq, MLA_HEADS * V_HEAD), BF16),
        scratch_shapes=[pltpu.VMEM((1, tq), F32), pltpu.VMEM((1, tq), F32), pltpu.VMEM((V_HEAD, tq), F32)],
        compiler_params=_cparams(("arbitrary", "arbitrary", "arbitrary")),
        name="mla_prompt_attention",
    )(q_cat, k_cat, vt)


PAGES_PER_STEP = 16
PAGE_GROUPS = 2


def _paged_kernel(pt_ref, *refs, scale, steps):
    npg = PAGES_PER_STEP
    c_refs = refs[:npg]
    krt_refs = refs[npg:2 * npg]
    (wukt_ref, qa_ref, qcat_ref, qcat_prev_ref, kcat_prev_ref, cn_prev_ref, o_ref,
     lhs_ref, c_scr, krt_scr, s_scr, m_ref, l_ref, acc_ref) = refs[2 * npg:]
    t = pl.program_id(0)
    n_hd = MLA_HEADS * QK_NOPE
    span = npg * PAGE_SIZE // PAGE_GROUPS
    prev_chunk = (t + steps - 1) % steps

    @pl.when(t == 0)
    def _():
        lhs_ref[0:n_hd, :] = wukt_ref[...]
        c_scr[1] = jnp.zeros(c_scr.shape[1:], BF16)
        s_scr[1] = jnp.zeros(s_scr.shape[1:], F32)
        m_ref[...] = jnp.zeros(m_ref.shape, F32)
        l_ref[...] = jnp.zeros(l_ref.shape, F32)
        acc_ref[...] = jnp.zeros(acc_ref.shape, F32)

    @pl.when(t % steps == 0)
    def _():
        lhs_ref[n_hd:n_hd + MLA_HEADS, :] = qa_ref[...]

    def step(cur, prv):
        first = prev_chunk == 0
        s_prev = s_scr[prv]
        m_old = jnp.where(first, -jnp.inf, m_ref[...])
        l_old = jnp.where(first, 0.0, l_ref[...])
        acc_old = jnp.where(first, 0.0, acc_ref[...])
        m_new = jnp.maximum(m_old, jnp.max(s_prev, axis=-1, keepdims=True))
        alpha = jnp.exp(m_old - m_new)
        pr = jnp.exp(s_prev - m_new)
        l_ref[...] = l_old * alpha + jnp.sum(pr, axis=-1, keepdims=True)
        acc_ref[...] = acc_old * alpha + jnp.dot(pr.astype(BF16), c_scr[prv], preferred_element_type=F32)
        m_ref[...] = m_new
        for i in range(npg):
            c_scr[cur, i * PAGE_SIZE:(i + 1) * PAGE_SIZE, :] = c_refs[i][...].astype(BF16)
            krt_scr[:, i * PAGE_SIZE:(i + 1) * PAGE_SIZE] = krt_refs[i][...].astype(BF16)
        qr = qcat_ref[:, QK_NOPE:QK_NOPE + QK_ROPE]
        for g in range(PAGE_GROUPS):
            c = c_scr[cur, g * span:(g + 1) * span, :]
            res = _nt_dot(lhs_ref[...], c)
            kt = res[0:n_hd, :].reshape(MLA_HEADS, QK_NOPE, span)
            inv = lax.rsqrt(jnp.sum(kt * kt, axis=1) * (1.0 / QK_NOPE) + EPS)
            s_rope = jnp.dot(qr, krt_scr[:, g * span:(g + 1) * span], preferred_element_type=F32)
            s_scr[cur, :, g * span:(g + 1) * span] = (res[n_hd:n_hd + MLA_HEADS, :] * inv + s_rope) * scale

    @pl.when(t % 2 == 0)
    def _():
        step(0, 1)

    @pl.when(t % 2 == 1)
    def _():
        step(1, 0)

    @pl.when((prev_chunk == steps - 1) & (t > 0))
    def _():
        s_new = jnp.sum(qcat_prev_ref[...].astype(F32) * kcat_prev_ref[...].astype(F32), axis=-1,
                        keepdims=True) * scale
        m_old = m_ref[...]
        m_fin = jnp.maximum(m_old, s_new)
        a2 = jnp.exp(m_old - m_fin)
        p_new = jnp.exp(s_new - m_fin)
        l_fin = l_ref[...] * a2 + p_new
        cn = cn_prev_ref[...].astype(BF16).astype(F32)
        o_ref[...] = (acc_ref[...] * a2 + p_new * cn) / l_fin


def _paged_attention(page_table, ckv_pool, krt_pool, wukt, qa, q_cat, k_cat, cn):
    n_s, n_pages = page_table.shape
    npg = PAGES_PER_STEP
    assert n_pages % npg == 0
    steps = n_pages // npg
    n_chunks = n_s * steps
    span = npg * PAGE_SIZE
    pt_flat = page_table.reshape(n_s * n_pages)

    def page_spec(shape, i):
        return pl.BlockSpec((None,) + shape,
                            lambda t, pt, i=i: (pt[jnp.minimum(t, n_chunks - 1) * npg + i], 0, 0))

    def cur_sample(shape):
        return pl.BlockSpec((None,) + shape, lambda t, pt: (jnp.minimum(t // steps, n_s - 1), 0, 0))

    def prev_sample(shape):
        return pl.BlockSpec((None,) + shape, lambda t, pt: (jnp.maximum(t - 1, 0) // steps, 0, 0))

    in_specs = ([page_spec((PAGE_SIZE, KV_LORA), i) for i in range(npg)]
                + [page_spec((QK_ROPE, PAGE_SIZE), i) for i in range(npg)]
                + [pl.BlockSpec(wukt.shape, lambda t, pt: (0, 0)),
                   cur_sample((MLA_HEADS, KV_LORA)), cur_sample((MLA_HEADS, QK_PAD)),
                   prev_sample((MLA_HEADS, QK_PAD)), prev_sample((MLA_HEADS, QK_PAD)), prev_sample((1, KV_LORA))])
    grid_spec = pltpu.PrefetchScalarGridSpec(
        num_scalar_prefetch=1,
        grid=(n_chunks + 1,),
        in_specs=in_specs,
        out_specs=prev_sample((MLA_HEADS, KV_LORA)),
        scratch_shapes=[pltpu.VMEM((MLA_HEADS * QK_NOPE + MLA_HEADS, KV_LORA), BF16),
                        pltpu.VMEM((2, span, KV_LORA), BF16), pltpu.VMEM((QK_ROPE, span), BF16),
                        pltpu.VMEM((2, MLA_HEADS, span), F32),
                        pltpu.VMEM((MLA_HEADS, 1), F32), pltpu.VMEM((MLA_HEADS, 1), F32),
                        pltpu.VMEM((MLA_HEADS, KV_LORA), F32)])
    return pl.pallas_call(
        functools.partial(_paged_kernel, scale=QK_HEAD ** -0.5, steps=steps),
        grid_spec=grid_spec,
        out_shape=jax.ShapeDtypeStruct((n_s, MLA_HEADS, KV_LORA), F32),
        compiler_params=_cparams(("arbitrary",)),
        name="mla_sample_attention",
    )(pt_flat, *([ckv_pool] * npg), *([krt_pool] * npg), wukt, qa, q_cat, q_cat, k_cat, cn)


def _gdn_prep_kernel(x_ref, halo_ref, misc_ref, cw_ref, alog_ref, dtb_ref, qkv_ref, kt_ref, gates_ref, xs_ref,
                     *, tiles_per_seq):
    i = pl.program_id(0)
    s = pl.program_id(1)
    tm = x_ref.shape[0]
    first = (i % tiles_per_seq) == 0

    @pl.when(first)
    def _():
        xs_ref[0:SUBLANES, :] = jnp.zeros((SUBLANES, xs_ref.shape[1]), F32)

    @pl.when(jnp.logical_not(first))
    def _():
        xs_ref[0:SUBLANES, :] = halo_ref[...]

    xs_ref[SUBLANES:SUBLANES + tm, :] = x_ref[...]
    acc = xs_ref[pl.ds(SUBLANES - (CONV_W - 1), tm), :] * cw_ref[0:1, :]
    for w in range(1, CONV_W):
        acc = acc + xs_ref[pl.ds(SUBLANES - (CONV_W - 1) + w, tm), :] * cw_ref[w:w + 1, :]
    y = acc * _sigmoid(acc)

    @pl.when(s == 2)
    def _():
        qkv_ref[...] = y

    @pl.when(s < 2)
    def _():
        scale = jnp.where(s == 0, GDN_DK ** -0.5, 1.0).astype(F32)
        for h in range(GDN_HEADS):
            yh = y[:, h * GDN_DK:(h + 1) * GDN_DK]
            yn = yh * (lax.rsqrt(jnp.sum(yh * yh, axis=-1, keepdims=True) + EPS) * scale)
            qkv_ref[:, h * GDN_DK:(h + 1) * GDN_DK] = yn

    @pl.when(s == 1)
    def _():
        kt_ref[...] = qkv_ref[...].T

    @pl.when(s == 0)
    def _():
        x = misc_ref[...] + dtb_ref[...]
        softplus = jnp.maximum(x, 0.0) + jnp.log(1.0 + jnp.exp(-jnp.abs(x)))
        lane = lax.broadcasted_iota(jnp.int32, x.shape, 1)
        is_beta = (lane >= MISC_B) & (lane < MISC_B + GDN_HEADS)
        gates_ref[...] = jnp.where(is_beta, _sigmoid(misc_ref[...]), -jnp.exp(alog_ref[...]) * softplus)


def _gdn_prep(proj, conv_w, alog_l, dtb_l, *, batch, seq, tm):
    n = batch * seq
    tm = _tile(seq, tm, LANES)
    tps = seq // tm
    sec0 = SEG_QKV // GDN_HK
    halo_blocks = tm // SUBLANES
    return pl.pallas_call(
        functools.partial(_gdn_prep_kernel, tiles_per_seq=tps),
        grid=(n // tm, 3),
        in_specs=[pl.BlockSpec((tm, GDN_HK), lambda i, s: (i, sec0 + s)),
                  pl.BlockSpec((SUBLANES, GDN_HK), lambda i, s: (jnp.maximum(i * halo_blocks - 1, 0), sec0 + s)),
                  pl.BlockSpec((tm, LANES), lambda i, s: (i, SEG_MISC // LANES)),
                  pl.BlockSpec((CONV_W, GDN_HK), lambda i, s: (0, s)),
                  pl.BlockSpec((1, LANES), lambda i, s: (0, 0)),
                  pl.BlockSpec((1, LANES), lambda i, s: (0, 0))],
        out_specs=[pl.BlockSpec((tm, GDN_HK), lambda i, s: (i, s)),
                   pl.BlockSpec((None, GDN_HK, tm), lambda i, s: (i // tps, 0, i % tps)),
                   pl.BlockSpec((tm, LANES), lambda i, s: (i, 0))],
        out_shape=[jax.ShapeDtypeStruct((n, 3 * GDN_HK), F32),
                   jax.ShapeDtypeStruct((batch, GDN_HK, seq), F32),
                   jax.ShapeDtypeStruct((n, LANES), F32)],
        scratch_shapes=[pltpu.VMEM((tm + SUBLANES, GDN_HK), F32)],
        compiler_params=_cparams(("arbitrary", "arbitrary")),
        name="gdn_prep",
    )(proj, proj, proj, conv_w, alog_l, dtb_l)


def _split3(x):
    hi = x.astype(BF16)
    r1 = x - hi.astype(F32)
    mid = r1.astype(BF16)
    lo = (r1 - mid.astype(F32)).astype(BF16)
    return hi, mid, lo


def _bdot(a, b):
    return jnp.dot(a.astype(BF16), b.astype(BF16), preferred_element_type=F32)


def _delta_kernel(q_ref, k_ref, v_ref, kt_ref, gates_ref, o_ref, s_out_ref, s_ref):
    c = GDN_CHUNK
    n = pl.program_id(1)
    heads = range(GDN_HEADS)

    @pl.when(n == 0)
    def _():
        s_ref[...] = jnp.zeros(s_ref.shape, F32)

    row = lax.broadcasted_iota(jnp.int32, (c, c), 0)
    col = lax.broadcasted_iota(jnp.int32, (c, c), 1)
    incl = row >= col
    strict = row > col
    eye = (row == col).astype(F32)
    tril = incl.astype(BF16)
    triu = (row <= col).astype(BF16)
    gates = gates_ref[...]
    gates_t = gates.T
    gc_all = sum(jnp.dot(tril, part, preferred_element_type=F32) for part in _split3(gates))
    gct_all = sum(jnp.dot(part, triu, preferred_element_type=F32) for part in _split3(gates_t))

    sl = [slice(h * GDN_DK, (h + 1) * GDN_DK) for h in heads]
    gcol = [gc_all[:, MISC_A + h:MISC_A + h + 1] for h in heads]
    grow = [gct_all[MISC_A + h:MISC_A + h + 1, :] for h in heads]
    glast = [gct_all[MISC_A + h:MISC_A + h + 1, c - 1:c] for h in heads]
    bcol = [gates[:, MISC_B + h:MISC_B + h + 1] for h in heads]
    eg = [jnp.exp(g) for g in gcol]
    e = [jnp.exp(jnp.where(incl, gcol[h] - grow[h], 0.0)) for h in heads]
    kb = [k_ref[:, sl[h]] * bcol[h] for h in heads]
    ktb = [kt_ref[sl[h], :].astype(BF16) for h in heads]
    x = [-(jnp.dot(kb[h].astype(BF16), ktb[h], preferred_element_type=F32) * jnp.where(strict, e[h], 0.0))
         for h in heads]
    t = [eye + x[h] for h in heads]
    for _ in range(6):
        x = [_bdot(x[h], x[h]) for h in heads]
        t = [t[h] + _bdot(t[h], x[h]) for h in heads]
    rhs = [jnp.concatenate([v_ref[:, sl[h]] * bcol[h], kb[h] * eg[h]], axis=1) for h in heads]
    sol = [_bdot(t[h], rhs[h]) for h in heads]
    attn = [jnp.dot(q_ref[:, sl[h]].astype(BF16), ktb[h], preferred_element_type=F32) * jnp.where(incl, e[h], 0.0)
            for h in heads]
    s_old = [s_ref[h] for h in heads]
    lhs = [jnp.concatenate([sol[h][:, GDN_DV:], q_ref[:, sl[h]] * eg[h]], axis=0) for h in heads]
    prod = [_bdot(lhs[h], s_old[h]) for h in heads]
    v_new = [sol[h][:, :GDN_DV] - prod[h][:c] for h in heads]
    for h in heads:
        o_ref[:, sl[h]] = prod[h][c:] + _bdot(attn[h], v_new[h])
    for h in heads:
        kdt = kt_ref[sl[h], :] * jnp.exp(glast[h] - grow[h])
        s_ref[h] = s_old[h] * jnp.exp(glast[h]) + _bdot(kdt, v_new[h])

    @pl.when(n == pl.num_programs(1) - 1)
    def _():
        s_out_ref[...] = s_ref[...]


def _delta_rule(qkv, kt, gates, *, batch, seq):
    c = GDN_CHUNK
    assert seq % c == 0
    nc = seq // c
    sec = lambda s: pl.BlockSpec((c, GDN_HK), lambda b, n, s=s: (b * nc + n, s))
    return pl.pallas_call(
        _delta_kernel,
        grid=(batch, nc),
        in_specs=[sec(0), sec(1), sec(2),
                  pl.BlockSpec((None, GDN_HK, c), lambda b, n: (b, 0, n)),
                  pl.BlockSpec((c, LANES), lambda b, n: (b * nc + n, 0))],
        out_specs=[pl.BlockSpec((c, GDN_HK), lambda b, n: (b * nc + n, 0)),
                   pl.BlockSpec((None, GDN_HEADS, GDN_DK, GDN_DV), lambda b, n: (b, 0, 0, 0))],
        out_shape=[jax.ShapeDtypeStruct((batch * seq, GDN_HK), F32),
                   jax.ShapeDtypeStruct((batch, GDN_HEADS, GDN_DK, GDN_DV), F32)],
        scratch_shapes=[pltpu.VMEM((GDN_HEADS, GDN_DK, GDN_DV), F32)],
        compiler_params=_cparams(("arbitrary", "arbitrary")),
        name="gdn_delta_rule",
    )(qkv, qkv, qkv, kt, gates)


def _merge_kernel(og_ref, z_ref, oa_ref, ga_ref, gb_ref, gn_ref, wpa_ref, wpb_ref, m_ref, ob_ref):
    @pl.when(pl.program_id(1) == 0)
    def _():
        gn = gn_ref[...]
        for h in range(GDN_HEADS):
            sl = slice(h * GDN_DV, (h + 1) * GDN_DV)
            z = z_ref[:, sl]
            ob_ref[:, sl] = (_rms_rows(og_ref[:, sl], gn) * (z * _sigmoid(z))).astype(BF16)

    pa = jnp.dot(oa_ref[...], wpa_ref[...], preferred_element_type=F32)
    pb = jnp.dot(ob_ref[...], wpb_ref[...], preferred_element_type=F32)
    m_ref[...] = (_sigmoid(ga_ref[...]) * pa + _sigmoid(gb_ref[...]) * pb).astype(BF16)


def _merge(o_g, oa, proj, gdn_norm, w_pa_b, w_pb_b, *, tm):
    n = o_g.shape[0]
    tm = _tile(n, tm, 16)
    tn = 1024
    nj = D_MODEL // tn
    return pl.pallas_call(
        _merge_kernel,
        grid=(n // tm, nj),
        in_specs=[pl.BlockSpec((tm, GDN_HK), lambda i, j: (i, 0)),
                  pl.BlockSpec((tm, GDN_HK), lambda i, j: (i, SEG_Z // GDN_HK)),
                  pl.BlockSpec((tm, MLA_HEADS * V_HEAD), lambda i, j: (i, 0)),
                  pl.BlockSpec((tm, tn), lambda i, j: (i, SEG_GA // tn + j)),
                  pl.BlockSpec((tm, tn), lambda i, j: (i, SEG_GB // tn + j)),
                  pl.BlockSpec((1, GDN_DV), lambda i, j: (0, 0)),
                  pl.BlockSpec((MLA_HEADS * V_HEAD, tn), lambda i, j: (0, j)),
                  pl.BlockSpec((GDN_HK, tn), lambda i, j: (0, j))],
        out_specs=pl.BlockSpec((tm, tn), lambda i, j: (i, j)),
        out_shape=jax.ShapeDtypeStruct((n, D_MODEL), BF16),
        scratch_shapes=[pltpu.VMEM((tm, GDN_HK), BF16)],
        compiler_params=_cparams(("arbitrary", "arbitrary")),
        name="gated_merge",
    )(o_g, proj, oa, proj, proj, gdn_norm, w_pa_b, w_pb_b)


def _out_proj_kernel(mp_ref, xp_ref, ms_ref, xs_ref, wo_ref, gf_ref, wrt_ref, h_ref, hn_ref, lg_ref, *, np_tiles):
    i = pl.program_id(0)

    def rows(m_ref, x_ref, n):
        h = x_ref[...] + jnp.dot(m_ref[...], wo_ref[...], preferred_element_type=F32)
        h_ref[0:n, :] = h
        hn = _rms_rows(h, gf_ref[...])
        hn_ref[0:n, :] = hn
        lg_ref[0:n, :] = jnp.dot(hn.astype(BF16), wrt_ref[...], preferred_element_type=F32)

    @pl.when(i < np_tiles)
    def _():
        rows(mp_ref, xp_ref, mp_ref.shape[0])

    @pl.when(i == np_tiles)
    def _():
        rows(ms_ref, xs_ref, ms_ref.shape[0])


def _out_proj(m_p, x_p, m_s, x_s, w_o_b, norm_ffn, w_rt_b, *, tm):
    n_p, n_s = m_p.shape[0], m_s.shape[0]
    tm = _tile(n_p, tm, 16)
    np_tiles = n_p // tm
    assert n_s <= tm
    const = lambda a: pl.BlockSpec(a.shape, lambda i: (0,) * a.ndim)
    p_spec = pl.BlockSpec((tm, D_MODEL), lambda i: (jnp.minimum(i, np_tiles - 1), 0))
    out = lambda w: pl.BlockSpec((tm, w), lambda i: (i, 0))
    return pl.pallas_call(
        functools.partial(_out_proj_kernel, np_tiles=np_tiles),
        grid=(np_tiles + 1,),
        in_specs=[p_spec, p_spec, const(m_s), const(x_s), const(w_o_b), const(norm_ffn), const(w_rt_b)],
        out_specs=[out(D_MODEL), out(D_MODEL), out(LANES)],
        out_shape=[jax.ShapeDtypeStruct((n_p + n_s, D_MODEL), F32),
                   jax.ShapeDtypeStruct((n_p + n_s, D_MODEL), F32),
                   jax.ShapeDtypeStruct((n_p + n_s, LANES), F32)],
        compiler_params=_cparams(("arbitrary",)),
        name="out_proj_router",
    )(m_p, x_p, m_s, x_s, w_o_b, norm_ffn, w_rt_b)


def _expert_weights(sched_ref, w_hbm, wbuf, wb_ref, sem):
    i = pl.program_id(0)

    def fetch(expert, slot):
        return pltpu.make_async_copy(w_hbm.at[expert], wbuf.at[slot], sem.at[slot])

    @pl.when(i == 0)
    def _():
        fetch(sched_ref[0, 0], 0).start()

    @pl.when(sched_ref[2, i] != 0)
    def _():
        slot = sched_ref[3, i]
        fetch(sched_ref[0, i], slot).wait()

        @pl.when(sched_ref[4, i] >= 0)
        def _():
            fetch(sched_ref[4, i], 1 - slot).start()

        wb_ref[...] = wbuf[slot].astype(BF16)


def _gmm_kernel(sched_ref, x_ref, w_hbm, o_ref, wbuf, wb_ref, sem):
    i = pl.program_id(0)
    _expert_weights(sched_ref, w_hbm, wbuf, wb_ref, sem)

    @pl.when(sched_ref[1, i] != 0)
    def _():
        o_ref[...] = jnp.dot(x_ref[...].astype(BF16), wb_ref[...], preferred_element_type=F32)

    @pl.when(sched_ref[1, i] == 0)
    def _():
        o_ref[...] = jnp.zeros(o_ref.shape, o_ref.dtype)


def _gmm_down_kernel(sched_ref, g_ref, u_ref, w_hbm, o_ref, wbuf, wb_ref, sem):
    i = pl.program_id(0)
    _expert_weights(sched_ref, w_hbm, wbuf, wb_ref, sem)

    @pl.when(sched_ref[1, i] != 0)
    def _():
        gate = g_ref[...]
        act = gate * _sigmoid(gate) * u_ref[...]
        o_ref[...] = jnp.dot(act.astype(BF16), wb_ref[...], preferred_element_type=F32)

    @pl.when(sched_ref[1, i] == 0)
    def _():
        o_ref[...] = jnp.zeros(o_ref.shape, o_ref.dtype)


def _expert_schedule(bexp, used):
    nb = bexp.shape[0]
    first = jnp.concatenate([jnp.ones((1,), jnp.int32), (bexp[1:] != bexp[:-1]).astype(jnp.int32)])
    run = jnp.cumsum(first) - 1
    run_expert = jnp.zeros((nb,), jnp.int32).at[run].set(bexp)
    nxt = jnp.where(run + 1 <= run[-1], run_expert[jnp.minimum(run + 1, nb - 1)], -1)
    return jnp.stack([bexp, used, first, run % 2, nxt]).astype(jnp.int32)


def _grouped_call(kern, sched, row_inputs, w, *, name):
    rows = row_inputs[0].shape[0]
    k, n = w.shape[1], w.shape[2]
    nb = rows // MOE_ROWS
    grid_spec = pltpu.PrefetchScalarGridSpec(
        num_scalar_prefetch=1,
        grid=(nb,),
        in_specs=[pl.BlockSpec((MOE_ROWS, a.shape[1]), lambda i, sc: (i, 0)) for a in row_inputs]
        + [pl.BlockSpec(memory_space=pl.ANY)],
        out_specs=pl.BlockSpec((MOE_ROWS, n), lambda i, sc: (i, 0)),
        scratch_shapes=[pltpu.VMEM((2, k, n), F32), pltpu.VMEM((k, n), BF16), pltpu.SemaphoreType.DMA((2,))])
    return pl.pallas_call(
        kern, grid_spec=grid_spec,
        out_shape=jax.ShapeDtypeStruct((rows, n), F32),
        compiler_params=_cparams(("arbitrary",)),
        name=name,
    )(sched, *row_inputs, w)


def _l2n(x):
    return x * lax.rsqrt(jnp.sum(x * x, axis=-1, keepdims=True) + EPS)


def _rms(x, g):
    return x * lax.rsqrt(jnp.mean(x * x, axis=-1, keepdims=True) + EPS) * g


def _rope_tables(pos):
    inv = ROPE_THETA ** (-jnp.arange(0, QK_ROPE, 2, dtype=F32) / QK_ROPE)
    ang = pos[:, None] * inv[None, :]
    cos, sin = jnp.cos(ang), jnp.sin(ang)
    return jnp.concatenate([cos, cos, cos, cos], axis=1), jnp.concatenate([-sin, sin, -sin, sin], axis=1)


def _w_in_segments():
    segs = []
    for h in range(MLA_HEADS):
        segs.append((h * QK_HEAD, QK_NOPE, SEG_QN + h * QK_NOPE))
        segs.append((h * QK_HEAD + QK_NOPE, QK_ROPE, SEG_QR + h * QK_ROPE))
    o = MLA_HEADS * QK_HEAD
    segs.append((o, KV_LORA, SEG_CKV)); o += KV_LORA
    segs.append((o, QK_ROPE, SEG_MISC)); o += QK_ROPE
    segs.append((o, GDN_QKV_W, SEG_QKV)); o += GDN_QKV_W
    segs.append((o, GDN_HEADS, SEG_MISC + MISC_A)); o += GDN_HEADS
    segs.append((o, GDN_HEADS, SEG_MISC + MISC_B)); o += GDN_HEADS
    segs.append((o, GDN_HEADS * GDN_DV, SEG_Z)); o += GDN_HEADS * GDN_DV
    segs.append((o, D_MODEL, SEG_GA)); o += D_MODEL
    segs.append((o, D_MODEL, SEG_GB))
    return segs


def _reorder_kernel(w_ref, o_ref):
    for src, width, dst in _w_in_segments():
        o_ref[:, dst:dst + width] = w_ref[:, src:src + width].astype(BF16)
    tail = SEG_MISC + MISC_B + GDN_HEADS
    o_ref[:, tail:D_PROJ] = jnp.zeros((o_ref.shape[0], D_PROJ - tail), BF16)


def _reorder_w_in_pallas(w_in):
    d, d_in = w_in.shape
    tr = 128
    return pl.pallas_call(
        _reorder_kernel,
        grid=(d // tr,),
        in_specs=[pl.BlockSpec((tr, d_in), lambda i: (i, 0))],
        out_specs=pl.BlockSpec((tr, D_PROJ), lambda i: (i, 0)),
        out_shape=jax.ShapeDtypeStruct((d, D_PROJ), BF16),
        compiler_params=_cparams(("arbitrary",)),
        name="w_in_reorder",
    )(w_in)


def _reorder_w_in(w_in):
    o = 0
    wq = w_in[:, o:o + MLA_HEADS * QK_HEAD].reshape(D_MODEL, MLA_HEADS, QK_HEAD); o += MLA_HEADS * QK_HEAD
    wckv = w_in[:, o:o + KV_LORA]; o += KV_LORA
    wkr = w_in[:, o:o + QK_ROPE]; o += QK_ROPE
    wqkv = w_in[:, o:o + GDN_QKV_W]; o += GDN_QKV_W
    wa = w_in[:, o:o + GDN_HEADS]; o += GDN_HEADS
    wb = w_in[:, o:o + GDN_HEADS]; o += GDN_HEADS
    wz = w_in[:, o:o + GDN_HEADS * GDN_DV]; o += GDN_HEADS * GDN_DV
    wga = w_in[:, o:o + D_MODEL]; o += D_MODEL
    wgb = w_in[:, o:o + D_MODEL]
    pad = jnp.zeros((D_MODEL, D_PROJ - SEG_MISC - QK_ROPE - 2 * GDN_HEADS), w_in.dtype)
    return jnp.concatenate([
        wz, wga, wgb, wqkv, wq[:, :, :QK_NOPE].reshape(D_MODEL, -1), wq[:, :, QK_NOPE:].reshape(D_MODEL, -1),
        wckv, wkr, wa, wb, pad], axis=1).astype(BF16)


def _lane_vec(v, offset):
    return jnp.zeros((1, LANES), F32).at[0, offset:offset + v.shape[0]].set(v.astype(F32))


def _hier_moe(h, hn, logits, b_rg, b_re, w_gate, w_up, w_down):
    t = h.shape[0]
    n_rt = N_GROUPS + N_EXPERTS
    g_logits = logits[:, :N_GROUPS] + b_rg.astype(F32)
    p_group = jax.nn.softmax(g_logits, axis=-1)
    grp = jnp.argmax(g_logits, axis=-1).astype(jnp.int32)
    p_grp_sel = jnp.take_along_axis(p_group, grp[:, None], axis=-1)[:, 0]
    e_logits = (logits[:, N_GROUPS:n_rt] + b_re.astype(F32)).reshape(t, N_GROUPS, EXP_PER_GROUP)
    e_sel = jnp.take_along_axis(e_logits, grp[:, None, None], axis=1)[:, 0]
    top_w, top_i = lax.top_k(jax.nn.softmax(e_sel, axis=-1), TOP_K)
    top_w = top_w / jnp.sum(top_w, axis=-1, keepdims=True) * p_grp_sel[:, None]
    expert = grp[:, None] * EXP_PER_GROUP + top_i.astype(jnp.int32)
    n_assign = t * TOP_K
    e_flat = expert.reshape(n_assign)
    w_flat = top_w.reshape(n_assign)
    onehot = (e_flat[:, None] == jnp.arange(N_EXPERTS, dtype=jnp.int32)[None, :]).astype(jnp.int32)
    csum = jnp.cumsum(onehot, axis=0)
    rank = jnp.take_along_axis(csum, e_flat[:, None], axis=1)[:, 0] - 1
    counts = csum[-1]
    pcounts = (counts + MOE_ROWS - 1) // MOE_ROWS * MOE_ROWS
    pends = jnp.cumsum(pcounts)
    pstarts = pends - pcounts
    dest = pstarts[e_flat] + rank
    n_blocks = -(-n_assign // MOE_ROWS) + N_EXPERTS
    n_rows = n_blocks * MOE_ROWS
    tok = jnp.arange(n_assign, dtype=jnp.int32) // TOP_K
    src = jnp.zeros((n_rows,), jnp.int32).at[dest].set(tok)
    xb = hn[src]
    bstart = jnp.arange(n_blocks, dtype=jnp.int32) * MOE_ROWS
    bexp = jnp.minimum(jnp.searchsorted(pends, bstart, side='right'), N_EXPERTS - 1).astype(jnp.int32)
    used = (bstart < pends[-1]).astype(jnp.int32)
    sched = _expert_schedule(bexp, used)
    gate = _grouped_call(_gmm_kernel, sched, [xb], w_gate, name="moe_gate")
    up = _grouped_call(_gmm_kernel, sched, [xb], w_up, name="moe_up")
    yb = _grouped_call(_gmm_down_kernel, sched, [gate, up], w_down, name="moe_down")
    dest2 = dest.reshape(t, TOP_K)
    y = h
    for k in range(TOP_K):
        y = y + yb[dest2[:, k]] * top_w[:, k:k + 1]
    return y


def kernel(x_prompt, x_sample, cache_ckv, cache_krope, state_ssm, state_conv, page_table, norm_attn, w_in, norm_ckv, w_uk, w_uv, q_gain_nope, q_gain_rope, k_gain_nope, k_gain_rope, conv_w, gdn_a_log, gdn_dt_bias, gdn_norm, w_pa, w_pb, w_o, norm_ffn, w_router_group, b_router_group, w_router_expert, b_router_expert, w_gate, w_up, w_down):
    depth = w_in.shape[0]
    bsz, seq, _ = x_prompt.shape
    n_s, s_new, _ = x_sample.shape
    assert s_new == 1
    n_p = bsz * seq
    n_all = n_p + n_s
    n_past = page_table.shape[1] * PAGE_SIZE
    cos_p, sin_p = _rope_tables(jnp.tile(jnp.arange(seq, dtype=F32), bsz))
    cos_s, sin_s = _rope_tables(jnp.full((n_s,), n_past, F32))
    tq = _tile(seq, 512, LANES)
    xp = x_prompt.reshape(n_p, D_MODEL)
    xs = x_sample.reshape(n_s, D_MODEL)
    outs = [[] for _ in range(8)]
    row = lambda v: v.astype(F32)[None, :]
    for l in range(depth):
        w_in_r = _reorder_w_in_pallas(w_in[l])
        w_uk_b = w_uk[l].astype(BF16)
        w_uvt_b = w_uv[l].T.astype(BF16)
        wukt = w_uk_b.T
        w_pa_b, w_pb_b, w_o_b = w_pa[l].astype(BF16), w_pb[l].astype(BF16), w_o[l].astype(BF16)
        w_rt_b = jnp.concatenate([w_router_group[l], w_router_expert[l],
                                  jnp.zeros((D_MODEL, LANES - N_GROUPS - N_EXPERTS), F32)], axis=1).astype(BF16)
        gains = (row(q_gain_nope[l]), jnp.tile(row(q_gain_rope[l]), (1, 2)), row(k_gain_nope[l]),
                 jnp.tile(row(k_gain_rope[l]), (1, 2)), row(norm_ckv[l]))
        alog_l = _lane_vec(gdn_a_log[l], MISC_A)
        dtb_l = _lane_vec(gdn_dt_bias[l], MISC_A)
        proj_p = _in_proj(xp, row(norm_attn[l]), w_in_r, tm=1024)
        proj_s = _in_proj(xs, row(norm_attn[l]), w_in_r, tm=n_s)

        qcat_p, kcat_p, c_p, kr_p, vt_p = _mla_prep(proj_p, cos_p, sin_p, *gains, w_uk_b, w_uvt_b,
                                                    tm=tq, with_vt=True, with_qg=False)
        oa_p = _flash_attention(qcat_p, kcat_p, vt_p, batch=bsz, seq=seq, tq=tq)

        qcat_s, kcat_s, c_s, kr_s, qg_s = _mla_prep(proj_s, cos_s, sin_s, *gains, w_uk_b, w_uvt_b,
                                                    tm=n_s, with_vt=False, with_qg=True)
        qa = _headwise_matmul(qg_s, wukt, heads=MLA_HEADS, b_rows_by_head=True, name="q_absorb")
        o_lat = _paged_attention(page_table, cache_ckv[l], jnp.swapaxes(cache_krope[l], 1, 2), wukt,
                                 qa.astype(BF16).reshape(n_s, MLA_HEADS, KV_LORA),
                                 qcat_s.reshape(n_s, MLA_HEADS, QK_PAD), kcat_s.reshape(n_s, MLA_HEADS, QK_PAD),
                                 c_s[:, None, :])
        oa_s = _headwise_matmul(o_lat.astype(BF16).reshape(n_s, MLA_HEADS * KV_LORA), w_uv[l].astype(BF16),
                                heads=MLA_HEADS, b_rows_by_head=False, name="v_absorb").astype(BF16)

        qkv_p, kt_p, gates_p = _gdn_prep(proj_p, conv_w[l], alog_l, dtb_l, batch=bsz, seq=seq, tm=512)
        o_g, ssm_p = _delta_rule(qkv_p, kt_p, gates_p, batch=bsz, seq=seq)
        conv_p = proj_p.reshape(bsz, seq, D_PROJ)[:, seq - (CONV_W - 1):, SEG_QKV:SEG_QKV + GDN_QKV_W]

        qkv_raw_s = proj_s[:, SEG_QKV:SEG_QKV + GDN_QKV_W]
        xpad_s = jnp.concatenate([state_conv[l].astype(F32), qkv_raw_s[:, None, :]], axis=1)
        acc_s = xpad_s[:, 0] * conv_w[l, 0]
        for w in range(1, CONV_W):
            acc_s = acc_s + xpad_s[:, w] * conv_w[l, w]
        qkv_s = jax.nn.silu(acc_s)
        conv_s = xpad_s[:, 1:]
        q_s = _l2n(qkv_s[:, :GDN_HK].reshape(n_s, GDN_HEADS, GDN_DK)) * (GDN_DK ** -0.5)
        k_s = _l2n(qkv_s[:, GDN_HK:2 * GDN_HK].reshape(n_s, GDN_HEADS, GDN_DK))
        v_s = qkv_s[:, 2 * GDN_HK:].reshape(n_s, GDN_HEADS, GDN_DV)
        a_s = proj_s[:, SEG_MISC + MISC_A:SEG_MISC + MISC_A + GDN_HEADS]
        b_s = proj_s[:, SEG_MISC + MISC_B:SEG_MISC + MISC_B + GDN_HEADS]
        g_s = -jnp.exp(gdn_a_log[l].astype(F32)) * jax.nn.softplus(a_s + gdn_dt_bias[l].astype(F32))
        beta_s = jax.nn.sigmoid(b_s)
        eg = jnp.exp(g_s)[..., None]
        s0 = state_ssm[l].astype(F32)
        kcd = k_s * beta_s[..., None] * eg
        v_new = v_s * beta_s[..., None] - jnp.einsum('bhk,bhkv->bhv', kcd, s0)
        qk = jnp.einsum('bhd,bhd->bh', q_s, k_s)[..., None]
        o_s = jnp.einsum('bhk,bhkv->bhv', q_s * eg, s0) + qk * v_new
        ssm_s = s0 * eg[..., None] + jnp.einsum('bhk,bhv->bhkv', k_s, v_new)

        m_p = _merge(o_g, oa_p, proj_p, row(gdn_norm[l]), w_pa_b, w_pb_b, tm=512)
        m_s = _merge(o_s.reshape(n_s, GDN_HK), oa_s, proj_s, row(gdn_norm[l]), w_pa_b, w_pb_b, tm=n_s)
        h_all, hn_all, lg_all = _out_proj(m_p, xp, m_s, xs, w_o_b, row(norm_ffn[l]), w_rt_b, tm=512)
        y_all = _hier_moe(h_all, hn_all, lg_all, b_router_group[l], b_router_expert[l],
                          w_gate[l], w_up[l], w_down[l])
        xp, xs = y_all[:n_p], y_all[n_p:]

        for lst, val in zip(outs, (c_p.reshape(bsz, seq, KV_LORA), kr_p[:, :QK_ROPE].reshape(bsz, seq, QK_ROPE),
                                   c_s.reshape(n_s, 1, KV_LORA), kr_s[:, :QK_ROPE].reshape(n_s, 1, QK_ROPE),
                                   ssm_p, conv_p, ssm_s, conv_s)):
            lst.append(val)
    return (xp.reshape(bsz, seq, D_MODEL), xs.reshape(n_s, 1, D_MODEL)) + tuple(jnp.stack(o) for o in outs)
```

```python
import functools

import jax
import jax.numpy as jnp
from jax import lax
from jax.experimental import pallas as pl
from jax.experimental.pallas import tpu as pltpu

F32 = jnp.float32
BF16 = jnp.bfloat16

D_MODEL = 2048
PAGE_SIZE = 128
MLA_HEADS = 16
QK_NOPE = 128
QK_ROPE = 64
QK_HEAD = QK_NOPE + QK_ROPE
QK_PAD = 256
V_HEAD = 128
KV_LORA = 512
ROPE_THETA = 10000.0
GDN_HEADS = 16
GDN_DK = 128
GDN_DV = 128
GDN_HK = GDN_HEADS * GDN_DK
GDN_QKV_W = GDN_HEADS * (2 * GDN_DK + GDN_DV)
CONV_W = 4
GDN_CHUNK = 128
N_GROUPS = 8
EXP_PER_GROUP = 8
N_EXPERTS = N_GROUPS * EXP_PER_GROUP
TOP_K = 2
D_EXPERT = 1408
MOE_ROWS = 128
MOE_SPLITS = 4
EPS = 1e-6
LANES = 128
SUBLANES = 8

V7X_VMEM_LIMIT = 56 * 1024 * 1024

SEG_Z = 0
SEG_GA = SEG_Z + GDN_HEADS * GDN_DV
SEG_GB = SEG_GA + D_MODEL
SEG_QKV = SEG_GB + D_MODEL
SEG_QN = SEG_QKV + GDN_QKV_W
SEG_QR = SEG_QN + MLA_HEADS * QK_NOPE
SEG_CKV = SEG_QR + MLA_HEADS * QK_ROPE
SEG_MISC = SEG_CKV + KV_LORA
PROJ_TN = 1024
D_PROJ = -(-(SEG_MISC + LANES) // PROJ_TN) * PROJ_TN
MISC_A = QK_ROPE
MISC_B = QK_ROPE + GDN_HEADS


def _cparams(sem, vmem=V7X_VMEM_LIMIT):
    return pltpu.CompilerParams(dimension_semantics=sem, vmem_limit_bytes=vmem)


def _tile(dim, cap, align):
    if dim <= cap:
        return dim
    for t in range(cap - cap % align, 0, -align):
        if dim % t == 0:
            return t
    raise ValueError(f"no {align}-aligned tile <= {cap} divides {dim}")


def _nt_dot(a, b):
    return lax.dot_general(a, b, (((1,), (1,)), ((), ())), preferred_element_type=F32)


def _rms_rows(x, gain):
    return x * lax.rsqrt(jnp.mean(x * x, axis=-1, keepdims=True) + EPS) * gain


def _sigmoid(x):
    return 1.0 / (1.0 + jnp.exp(-x))


def _in_proj_kernel(x_ref, g_ref, w_ref, o_ref, u_ref):
    @pl.when(pl.program_id(1) == 0)
    def _():
        u_ref[...] = _rms_rows(x_ref[...], g_ref[...]).astype(BF16)

    o_ref[...] = jnp.dot(u_ref[...], w_ref[...], preferred_element_type=F32)


def _in_proj(x, gain, w, *, tm):
    m, d = x.shape
    n = w.shape[1]
    tm = _tile(m, tm, 16)
    tn = PROJ_TN
    assert n % tn == 0
    return pl.pallas_call(
        _in_proj_kernel,
        grid=(m // tm, n // tn),
        in_specs=[pl.BlockSpec((tm, d), lambda i, j: (i, 0)),
                  pl.BlockSpec((1, d), lambda i, j: (0, 0)),
                  pl.BlockSpec((d, tn), lambda i, j: (0, j))],
        out_specs=pl.BlockSpec((tm, tn), lambda i, j: (i, j)),
        out_shape=jax.ShapeDtypeStruct((m, n), F32),
        scratch_shapes=[pltpu.VMEM((tm, d), BF16)],
        compiler_params=_cparams(("arbitrary", "arbitrary")),
        name="in_proj",
    )(x, gain, w)


def _mm_kernel(a_ref, b_ref, o_ref):
    o_ref[...] = jnp.dot(a_ref[...], b_ref[...], preferred_element_type=F32).astype(o_ref.dtype)


def _headwise_matmul(a, b, *, heads, b_rows_by_head, name):
    m = a.shape[0]
    k = a.shape[1] // heads
    if b_rows_by_head:
        n = b.shape[1]
        b_spec = pl.BlockSpec((k, n), lambda h: (h, 0))
    else:
        n = b.shape[1] // heads
        b_spec = pl.BlockSpec((k, n), lambda h: (0, h))
    return pl.pallas_call(
        _mm_kernel,
        grid=(heads,),
        in_specs=[pl.BlockSpec((m, k), lambda h: (0, h)), b_spec],
        out_specs=pl.BlockSpec((m, n), lambda h: (0, h)),
        out_shape=jax.ShapeDtypeStruct((m, heads * n), F32),
        compiler_params=_cparams(("arbitrary",)),
        name=name,
    )(a, b)


def _swap_halves(y, lane):
    half = QK_ROPE // 2
    return jnp.where(lane % QK_ROPE < half, pltpu.roll(y, LANES - half, axis=1), pltpu.roll(y, half, axis=1))


def _rope_pair(x, gain2, cos4, sin4, lane):
    lo = lane < QK_ROPE
    ss = x * x
    s_lo = jnp.sum(jnp.where(lo, ss, 0.0), axis=-1, keepdims=True)
    s_hi = jnp.sum(jnp.where(lo, 0.0, ss), axis=-1, keepdims=True)
    r = jnp.where(lo, lax.rsqrt(s_lo * (1.0 / QK_ROPE) + EPS), lax.rsqrt(s_hi * (1.0 / QK_ROPE) + EPS))
    y = x * r * gain2
    return y * cos4 + _swap_halves(y, lane) * sin4


def _mla_prep_kernel(qn_ref, qr_ref, ckv_ref, misc_ref, cos_ref, sin_ref, gqn_ref, gqr_ref, gkn_ref, gkr_ref,
                     gc_ref, wuk_ref, *rest, with_vt, with_qg):
    rest = list(rest)
    wuvt_ref = rest.pop(0) if with_vt else None
    qcat_ref, kcat_ref, c_ref, kr_ref = rest[:4]
    rest = rest[4:]
    vt_ref = rest.pop(0) if with_vt else None
    qg_ref = rest.pop(0) if with_qg else None
    rows = qn_ref.shape[0]
    lane = lax.broadcasted_iota(jnp.int32, (rows, LANES), 1)
    lo = lane < QK_ROPE
    cos4 = cos_ref[...]
    sin4 = sin_ref[...]

    c = _rms_rows(ckv_ref[...], gc_ref[...])
    c_ref[...] = c
    cb = c.astype(BF16)
    kr = jnp.where(lo, _rope_pair(misc_ref[...], gkr_ref[...], cos4, sin4, lane), 0.0)
    kr_ref[...] = kr
    krb = kr.astype(BF16)

    gqn = gqn_ref[...]
    gkn = gkn_ref[...]
    for h in range(MLA_HEADS):
        qn = _rms_rows(qn_ref[:, h * QK_NOPE:(h + 1) * QK_NOPE], gqn)
        qcat_ref[:, h * QK_PAD:h * QK_PAD + QK_NOPE] = qn.astype(BF16)
        if with_qg:
            qg_ref[:, h * QK_NOPE:(h + 1) * QK_NOPE] = (qn * gkn).astype(BF16)
    gqr = gqr_ref[...]
    for j in range(MLA_HEADS // 2):
        rot = _rope_pair(qr_ref[:, j * LANES:(j + 1) * LANES], gqr, cos4, sin4, lane)
        even = jnp.where(lo, rot, 0.0)
        odd = jnp.where(lo, pltpu.roll(rot, QK_ROPE, axis=1), 0.0)
        qcat_ref[:, (2 * j) * QK_PAD + QK_NOPE:(2 * j + 1) * QK_PAD] = even.astype(BF16)
        qcat_ref[:, (2 * j + 1) * QK_PAD + QK_NOPE:(2 * j + 2) * QK_PAD] = odd.astype(BF16)

    kn = jnp.dot(cb, wuk_ref[...], preferred_element_type=F32)
    for h in range(MLA_HEADS):
        kh = _rms_rows(kn[:, h * QK_NOPE:(h + 1) * QK_NOPE], gkn)
        kcat_ref[:, h * QK_PAD:h * QK_PAD + QK_NOPE] = kh.astype(BF16)
        kcat_ref[:, h * QK_PAD + QK_NOPE:(h + 1) * QK_PAD] = krb
    if with_vt:
        vt_ref[...] = _nt_dot(wuvt_ref[...], cb).astype(BF16)


def _mla_prep(proj, cos4, sin4, gqn, gqr2, gkn, gkr2, gckv, w_uk_b, w_uvt_b, *, tm, with_vt, with_qg):
    n = proj.shape[0]
    tm = _tile(n, tm, 16)
    nt = n // tm
    hq = MLA_HEADS * QK_NOPE
    row = lambda w, off: pl.BlockSpec((tm, w), lambda i, off=off, w=w: (i, off // w))
    const = lambda a: pl.BlockSpec(a.shape, lambda i: (0,) * a.ndim)
    in_specs = [row(hq, SEG_QN), row(MLA_HEADS * QK_ROPE, SEG_QR), row(KV_LORA, SEG_CKV), row(LANES, SEG_MISC),
                pl.BlockSpec((tm, LANES), lambda i: (i, 0)), pl.BlockSpec((tm, LANES), lambda i: (i, 0)),
                const(gqn), const(gqr2), const(gkn), const(gkr2), const(gckv), const(w_uk_b)]
    args = [proj, proj, proj, proj, cos4, sin4, gqn, gqr2, gkn, gkr2, gckv, w_uk_b]
    if with_vt:
        in_specs.append(const(w_uvt_b))
        args.append(w_uvt_b)
    out_specs = [pl.BlockSpec((tm, MLA_HEADS * QK_PAD), lambda i: (i, 0)),
                 pl.BlockSpec((tm, MLA_HEADS * QK_PAD), lambda i: (i, 0)),
                 pl.BlockSpec((tm, KV_LORA), lambda i: (i, 0)),
                 pl.BlockSpec((tm, LANES), lambda i: (i, 0))]
    out_shape = [jax.ShapeDtypeStruct((n, MLA_HEADS * QK_PAD), BF16),
                 jax.ShapeDtypeStruct((n, MLA_HEADS * QK_PAD), BF16),
                 jax.ShapeDtypeStruct((n, KV_LORA), F32),
                 jax.ShapeDtypeStruct((n, LANES), F32)]
    if with_vt:
        out_specs.append(pl.BlockSpec((None, MLA_HEADS * V_HEAD, tm), lambda i: (i, 0, 0)))
        out_shape.append(jax.ShapeDtypeStruct((nt, MLA_HEADS * V_HEAD, tm), BF16))
    if with_qg:
        out_specs.append(pl.BlockSpec((tm, hq), lambda i: (i, 0)))
        out_shape.append(jax.ShapeDtypeStruct((n, hq), BF16))
    return pl.pallas_call(
        functools.partial(_mla_prep_kernel, with_vt=with_vt, with_qg=with_qg),
        grid=(nt,), in_specs=in_specs, out_specs=out_specs, out_shape=out_shape,
        compiler_params=_cparams(("arbitrary",)),
        name="mla_prep",
    )(*args)


def _flash_kernel(q_ref, k_ref, vt_ref, o_ref, m_ref, l_ref, acc_ref, *, tq, scale):
    qi = pl.program_id(2)
    q = q_ref[...]
    m_ref[...] = jnp.full(m_ref.shape, -jnp.inf, F32)
    l_ref[...] = jnp.zeros(l_ref.shape, F32)
    acc_ref[...] = jnp.zeros(acc_ref.shape, F32)

    c2 = scale * 1.4426950408889634

    def block(ki, nblk, masked):
        start = pl.multiple_of(ki * tq, tq)
        k = k_ref[pl.ds(start, nblk * tq), :]
        s = _nt_dot(k, q)
        if masked:
            key = lax.broadcasted_iota(jnp.int32, s.shape, 0)
            qry = lax.broadcasted_iota(jnp.int32, s.shape, 1)
            s = jnp.where(key <= qry, s, -jnp.inf)
        m_old = m_ref[...]
        m_new = jnp.maximum(m_old, jnp.max(s, axis=0, keepdims=True))
        alpha = jnp.exp2((m_old - m_new) * c2)
        p32 = jnp.exp2((s - m_new) * c2)
        l_ref[...] = l_ref[...] * alpha + jnp.sum(p32, axis=0, keepdims=True)
        p = p32.astype(BF16)
        pv = jnp.dot(vt_ref[ki], p[0:tq], preferred_element_type=F32)
        for j in range(1, nblk):
            pv = pv + jnp.dot(vt_ref[ki + j], p[j * tq:(j + 1) * tq], preferred_element_type=F32)
        acc_ref[...] = acc_ref[...] * alpha + pv
        m_ref[...] = m_new

    def body(kp, carry):
        block(2 * kp, 2, False)
        return carry

    lax.fori_loop(0, qi // 2, body, 0)

    @pl.when(qi % 2 == 1)
    def _():
        block(qi - 1, 1, False)

    block(qi, 1, True)
    o_ref[...] = (acc_ref[...] / l_ref[...]).T.astype(o_ref.dtype)


def _flash_attention(q_cat, k_cat, vt, *, batch, seq, tq):
    nq = seq // tq
    kern = functools.partial(_flash_kernel, tq=tq, scale=QK_HEAD ** -0.5)
    return pl.pallas_call(
        kern,
        grid=(batch, MLA_HEADS, nq),
        in_specs=[pl.BlockSpec((tq, QK_PAD), lambda b, h, i: (b * nq + i, h)),
                  pl.BlockSpec((seq, QK_PAD), lambda b, h, i: (b, h)),
                  pl.BlockSpec((nq, V_HEAD, tq), lambda b, h, i: (b, h, 0))],
        out_specs=pl.BlockSpec((tq, V_HEAD), lambda b, h, i: (b * nq + i, h)),
        out_shape=jax.ShapeDtypeStruct((batch * seq, MLA_HEADS * V_HEAD), BF16),
        scratch_shapes=[pltpu.VMEM((1, tq), F32), pltpu.VMEM((1, tq), F32), pltpu.VMEM((V_HEAD, tq), F32)],
        compiler_params=_cparams(("arbitrary", "arbitrary", "arbitrary")),
        name="mla_prompt_attention",
    )(q_cat, k_cat, vt)


PAGES_PER_STEP = 16
PAGE_RING = 3
PAGE_GROUPS = 2


def _paged_kernel(pt_ref, ckv_hbm, krt_hbm, wukt_ref, qa_ref, qcat_ref, qcat_prev_ref, kcat_prev_ref, cn_prev_ref,
                  o_ref, lhs_ref, cbuf, krbuf, csem, krsem, c_scr, krt_scr, s_scr, m_ref, l_ref, acc_ref,
                  *, scale, steps, n_chunks):
    npg = PAGES_PER_STEP
    t = pl.program_id(0)
    n_hd = MLA_HEADS * QK_NOPE
    span = npg * PAGE_SIZE // PAGE_GROUPS
    prev_chunk = (t + steps - 1) % steps

    def page_copies(chunk, slot, resolve_pages):
        base = jnp.minimum(chunk, n_chunks - 1) * npg
        copies = []
        for i in range(npg):
            page = pt_ref[base + i] if resolve_pages else 0
            copies.append(pltpu.make_async_copy(ckv_hbm.at[page], cbuf.at[slot, i], csem.at[slot]))
            copies.append(pltpu.make_async_copy(krt_hbm.at[page], krbuf.at[slot, i], krsem.at[slot]))
        return copies

    @pl.when(t == 0)
    def _():
        for chunk in range(PAGE_RING - 1):
            for cp in page_copies(chunk, chunk, True):
                cp.start()
        lhs_ref[0:n_hd, :] = wukt_ref[...]
        c_scr[1] = jnp.zeros(c_scr.shape[1:], BF16)
        s_scr[1] = jnp.zeros(s_scr.shape[1:], F32)
        m_ref[...] = jnp.zeros(m_ref.shape, F32)
        l_ref[...] = jnp.zeros(l_ref.shape, F32)
        acc_ref[...] = jnp.zeros(acc_ref.shape, F32)

    @pl.when(t % steps == 0)
    def _():
        lhs_ref[n_hd:n_hd + MLA_HEADS, :] = qa_ref[...]

    ring = t % PAGE_RING

    def step(cur, prv):
        for cp in page_copies(t, ring, False):
            cp.wait()
        first = prev_chunk == 0
        s_prev = s_scr[prv]
        m_old = jnp.where(first, -jnp.inf, m_ref[...])
        l_old = jnp.where(first, 0.0, l_ref[...])
        acc_old = jnp.where(first, 0.0, acc_ref[...])
        m_new = jnp.maximum(m_old, jnp.max(s_prev, axis=-1, keepdims=True))
        alpha = jnp.exp(m_old - m_new)
        pr = jnp.exp(s_prev - m_new)
        l_ref[...] = l_old * alpha + jnp.sum(pr, axis=-1, keepdims=True)
        acc_ref[...] = acc_old * alpha + jnp.dot(pr.astype(BF16), c_scr[prv], preferred_element_type=F32)
        m_ref[...] = m_new
        for i in range(npg):
            c_scr[cur, i * PAGE_SIZE:(i + 1) * PAGE_SIZE, :] = cbuf[ring, i].astype(BF16)
            krt_scr[:, i * PAGE_SIZE:(i + 1) * PAGE_SIZE] = krbuf[ring, i].astype(BF16)
        qr = qcat_ref[:, QK_NOPE:QK_NOPE + QK_ROPE]
        for g in range(PAGE_GROUPS):
            c = c_scr[cur, g * span:(g + 1) * span, :]
            res = _nt_dot(lhs_ref[...], c)
            kt = res[0:n_hd, :].reshape(MLA_HEADS, QK_NOPE, span)
            inv = lax.rsqrt(jnp.sum(kt * kt, axis=1) * (1.0 / QK_NOPE) + EPS)
            s_rope = jnp.dot(qr, krt_scr[:, g * span:(g + 1) * span], preferred_element_type=F32)
            s_scr[cur, :, g * span:(g + 1) * span] = (res[n_hd:n_hd + MLA_HEADS, :] * inv + s_rope) * scale
        for cp in page_copies(t + PAGE_RING - 1, (t + PAGE_RING - 1) % PAGE_RING, True):
            cp.start()

    @pl.when(t % 2 == 0)
    def _():
        step(0, 1)

    @pl.when(t % 2 == 1)
    def _():
        step(1, 0)

    @pl.when((prev_chunk == steps - 1) & (t > 0))
    def _():
        s_new = jnp.sum(qcat_prev_ref[...].astype(F32) * kcat_prev_ref[...].astype(F32), axis=-1,
                        keepdims=True) * scale
        m_old = m_ref[...]
        m_fin = jnp.maximum(m_old, s_new)
        a2 = jnp.exp(m_old - m_fin)
        p_new = jnp.exp(s_new - m_fin)
        l_fin = l_ref[...] * a2 + p_new
        cn = cn_prev_ref[...].astype(BF16).astype(F32)
        o_ref[...] = (acc_ref[...] * a2 + p_new * cn) / l_fin

    @pl.when(t == n_chunks)
    def _():
        for ahead in range(1, PAGE_RING):
            for cp in page_copies(t + ahead, (t + ahead) % PAGE_RING, False):
                cp.wait()


def _paged_attention(page_table, ckv_pool, krt_pool, wukt, qa, q_cat, k_cat, cn):
    n_s, n_pages = page_table.shape
    npg = PAGES_PER_STEP
    assert n_pages % npg == 0
    steps = n_pages // npg
    n_chunks = n_s * steps
    span = npg * PAGE_SIZE
    pt_flat = page_table.reshape(n_s * n_pages)

    def cur_sample(shape):
        return pl.BlockSpec((None,) + shape, lambda t, pt: (jnp.minimum(t // steps, n_s - 1), 0, 0))

    def prev_sample(shape):
        return pl.BlockSpec((None,) + shape, lambda t, pt: (jnp.maximum(t - 1, 0) // steps, 0, 0))

    in_specs = [pl.BlockSpec(memory_space=pl.ANY), pl.BlockSpec(memory_space=pl.ANY),
                pl.BlockSpec(wukt.shape, lambda t, pt: (0, 0)),
                cur_sample((MLA_HEADS, KV_LORA)), cur_sample((MLA_HEADS, QK_PAD)),
                prev_sample((MLA_HEADS, QK_PAD)), prev_sample((MLA_HEADS, QK_PAD)), prev_sample((1, KV_LORA))]
    grid_spec = pltpu.PrefetchScalarGridSpec(
        num_scalar_prefetch=1,
        grid=(n_chunks + 1,),
        in_specs=in_specs,
        out_specs=prev_sample((MLA_HEADS, KV_LORA)),
        scratch_shapes=[pltpu.VMEM((MLA_HEADS * QK_NOPE + MLA_HEADS, KV_LORA), BF16),
                        pltpu.VMEM((PAGE_RING, npg, PAGE_SIZE, KV_LORA), F32),
                        pltpu.VMEM((PAGE_RING, npg, QK_ROPE, PAGE_SIZE), F32),
                        pltpu.SemaphoreType.DMA((PAGE_RING,)), pltpu.SemaphoreType.DMA((PAGE_RING,)),
                        pltpu.VMEM((2, span, KV_LORA), BF16), pltpu.VMEM((QK_ROPE, span), BF16),
                        pltpu.VMEM((2, MLA_HEADS, span), F32),
                        pltpu.VMEM((MLA_HEADS, 1), F32), pltpu.VMEM((MLA_HEADS, 1), F32),
                        pltpu.VMEM((MLA_HEADS, KV_LORA), F32)])
    return pl.pallas_call(
        functools.partial(_paged_kernel, scale=QK_HEAD ** -0.5, steps=steps, n_chunks=n_chunks),
        grid_spec=grid_spec,
        out_shape=jax.ShapeDtypeStruct((n_s, MLA_HEADS, KV_LORA), F32),
        compiler_params=_cparams(("arbitrary",)),
        name="mla_sample_attention",
    )(pt_flat, ckv_pool, krt_pool, wukt, qa, q_cat, q_cat, k_cat, cn)


def _gdn_prep_kernel(x_ref, halo_ref, misc_ref, cw_ref, alog_ref, dtb_ref, qkv_ref, kt_ref, gates_ref, xs_ref,
                     *, tiles_per_seq):
    i = pl.program_id(0)
    s = pl.program_id(1)
    tm = x_ref.shape[0]
    first = (i % tiles_per_seq) == 0

    @pl.when(first)
    def _():
        xs_ref[0:SUBLANES, :] = jnp.zeros((SUBLANES, xs_ref.shape[1]), F32)

    @pl.when(jnp.logical_not(first))
    def _():
        xs_ref[0:SUBLANES, :] = halo_ref[...]

    xs_ref[SUBLANES:SUBLANES + tm, :] = x_ref[...]
    acc = xs_ref[pl.ds(SUBLANES - (CONV_W - 1), tm), :] * cw_ref[0:1, :]
    for w in range(1, CONV_W):
        acc = acc + xs_ref[pl.ds(SUBLANES - (CONV_W - 1) + w, tm), :] * cw_ref[w:w + 1, :]
    y = acc * _sigmoid(acc)

    @pl.when(s == 2)
    def _():
        qkv_ref[...] = y

    @pl.when(s < 2)
    def _():
        scale = jnp.where(s == 0, GDN_DK ** -0.5, 1.0).astype(F32)
        for h in range(GDN_HEADS):
            yh = y[:, h * GDN_DK:(h + 1) * GDN_DK]
            yn = yh * (lax.rsqrt(jnp.sum(yh * yh, axis=-1, keepdims=True) + EPS) * scale)
            qkv_ref[:, h * GDN_DK:(h + 1) * GDN_DK] = yn

    @pl.when(s == 1)
    def _():
        kt_ref[...] = qkv_ref[...].T

    @pl.when(s == 0)
    def _():
        x = misc_ref[...] + dtb_ref[...]
        softplus = jnp.maximum(x, 0.0) + jnp.log(1.0 + jnp.exp(-jnp.abs(x)))
        lane = lax.broadcasted_iota(jnp.int32, x.shape, 1)
        is_beta = (lane >= MISC_B) & (lane < MISC_B + GDN_HEADS)
        gates_ref[...] = jnp.where(is_beta, _sigmoid(misc_ref[...]), -jnp.exp(alog_ref[...]) * softplus)


def _gdn_prep(proj, conv_w, alog_l, dtb_l, *, batch, seq, tm):
    n = batch * seq
    tm = _tile(seq, tm, LANES)
    tps = seq // tm
    sec0 = SEG_QKV // GDN_HK
    halo_blocks = tm // SUBLANES
    return pl.pallas_call(
        functools.partial(_gdn_prep_kernel, tiles_per_seq=tps),
        grid=(n // tm, 3),
        in_specs=[pl.BlockSpec((tm, GDN_HK), lambda i, s: (i, sec0 + s)),
                  pl.BlockSpec((SUBLANES, GDN_HK), lambda i, s: (jnp.maximum(i * halo_blocks - 1, 0), sec0 + s)),
                  pl.BlockSpec((tm, LANES), lambda i, s: (i, SEG_MISC // LANES)),
                  pl.BlockSpec((CONV_W, GDN_HK), lambda i, s: (0, s)),
                  pl.BlockSpec((1, LANES), lambda i, s: (0, 0)),
                  pl.BlockSpec((1, LANES), lambda i, s: (0, 0))],
        out_specs=[pl.BlockSpec((tm, GDN_HK), lambda i, s: (i, s)),
                   pl.BlockSpec((None, GDN_HK, tm), lambda i, s: (i // tps, 0, i % tps)),
                   pl.BlockSpec((tm, LANES), lambda i, s: (i, 0))],
        out_shape=[jax.ShapeDtypeStruct((n, 3 * GDN_HK), F32),
                   jax.ShapeDtypeStruct((batch, GDN_HK, seq), F32),
                   jax.ShapeDtypeStruct((n, LANES), F32)],
        scratch_shapes=[pltpu.VMEM((tm + SUBLANES, GDN_HK), F32)],
        compiler_params=_cparams(("arbitrary", "arbitrary")),
        name="gdn_prep",
    )(proj, proj, proj, conv_w, alog_l, dtb_l)


def _split3(x):
    hi = x.astype(BF16)
    r1 = x - hi.astype(F32)
    mid = r1.astype(BF16)
    lo = (r1 - mid.astype(F32)).astype(BF16)
    return hi, mid, lo


def _bdot(a, b):
    return jnp.dot(a.astype(BF16), b.astype(BF16), preferred_element_type=F32)


def _delta_kernel(q_ref, k_ref, v_ref, kt_ref, gates_ref, o_ref, s_out_ref, s_ref):
    c = GDN_CHUNK
    n = pl.program_id(1)
    heads = range(GDN_HEADS)

    @pl.when(n == 0)
    def _():
        s_ref[...] = jnp.zeros(s_ref.shape, F32)

    row = lax.broadcasted_iota(jnp.int32, (c, c), 0)
    col = lax.broadcasted_iota(jnp.int32, (c, c), 1)
    incl = row >= col
    strict = row > col
    eye = (row == col).astype(F32)
    tril = incl.astype(BF16)
    triu = (row <= col).astype(BF16)
    gates = gates_ref[...]
    gates_t = gates.T
    gc_all = sum(jnp.dot(tril, part, preferred_element_type=F32) for part in _split3(gates))
    gct_all = sum(jnp.dot(part, triu, preferred_element_type=F32) for part in _split3(gates_t))

    sl = [slice(h * GDN_DK, (h + 1) * GDN_DK) for h in heads]
    gcol = [gc_all[:, MISC_A + h:MISC_A + h + 1] for h in heads]
    grow = [gct_all[MISC_A + h:MISC_A + h + 1, :] for h in heads]
    glast = [gct_all[MISC_A + h:MISC_A + h + 1, c - 1:c] for h in heads]
    bcol = [gates[:, MISC_B + h:MISC_B + h + 1] for h in heads]
    eg = [jnp.exp(g) for g in gcol]
    e = [jnp.exp(jnp.where(incl, gcol[h] - grow[h], 0.0)) for h in heads]
    kb = [k_ref[:, sl[h]] * bcol[h] for h in heads]
    ktb = [kt_ref[sl[h], :].astype(BF16) for h in heads]
    x = [-(jnp.dot(kb[h].astype(BF16), ktb[h], preferred_element_type=F32) * jnp.where(strict, e[h], 0.0))
         for h in heads]
    t = [eye + x[h] for h in heads]
    for _ in range(6):
        x = [_bdot(x[h], x[h]) for h in heads]
        t = [t[h] + _bdot(t[h], x[h]) for h in heads]
    rhs = [jnp.concatenate([v_ref[:, sl[h]] * bcol[h], kb[h] * eg[h]], axis=1) for h in heads]
    sol = [_bdot(t[h], rhs[h]) for h in heads]
    attn = [jnp.dot(q_ref[:, sl[h]].astype(BF16), ktb[h], preferred_element_type=F32) * jnp.where(incl, e[h], 0.0)
            for h in heads]
    s_old = [s_ref[h] for h in heads]
    lhs = [jnp.concatenate([sol[h][:, GDN_DV:], q_ref[:, sl[h]] * eg[h]], axis=0) for h in heads]
    prod = [_bdot(lhs[h], s_old[h]) for h in heads]
    v_new = [sol[h][:, :GDN_DV] - prod[h][:c] for h in heads]
    for h in heads:
        o_ref[:, sl[h]] = prod[h][c:] + _bdot(attn[h], v_new[h])
    for h in heads:
        kdt = kt_ref[sl[h], :] * jnp.exp(glast[h] - grow[h])
        s_ref[h] = s_old[h] * jnp.exp(glast[h]) + _bdot(kdt, v_new[h])

    @pl.when(n == pl.num_programs(1) - 1)
    def _():
        s_out_ref[...] = s_ref[...]


def _delta_rule(qkv, kt, gates, *, batch, seq):
    c = GDN_CHUNK
    assert seq % c == 0
    nc = seq // c
    sec = lambda s: pl.BlockSpec((c, GDN_HK), lambda b, n, s=s: (b * nc + n, s))
    return pl.pallas_call(
        _delta_kernel,
        grid=(batch, nc),
        in_specs=[sec(0), sec(1), sec(2),
                  pl.BlockSpec((None, GDN_HK, c), lambda b, n: (b, 0, n)),
                  pl.BlockSpec((c, LANES), lambda b, n: (b * nc + n, 0))],
        out_specs=[pl.BlockSpec((c, GDN_HK), lambda b, n: (b * nc + n, 0)),
                   pl.BlockSpec((None, GDN_HEADS, GDN_DK, GDN_DV), lambda b, n: (b, 0, 0, 0))],
        out_shape=[jax.ShapeDtypeStruct((batch * seq, GDN_HK), F32),
                   jax.ShapeDtypeStruct((batch, GDN_HEADS, GDN_DK, GDN_DV), F32)],
        scratch_shapes=[pltpu.VMEM((GDN_HEADS, GDN_DK, GDN_DV), F32)],
        compiler_params=_cparams(("arbitrary", "arbitrary")),
        name="gdn_delta_rule",
    )(qkv, qkv, qkv, kt, gates)


def _merge_kernel(og_ref, z_ref, oa_ref, ga_ref, gb_ref, gn_ref, wpa_ref, wpb_ref, m_ref, ob_ref):
    @pl.when(pl.program_id(1) == 0)
    def _():
        gn = gn_ref[...]
        for h in range(GDN_HEADS):
            sl = slice(h * GDN_DV, (h + 1) * GDN_DV)
            z = z_ref[:, sl]
            ob_ref[:, sl] = (_rms_rows(og_ref[:, sl], gn) * (z * _sigmoid(z))).astype(BF16)

    pa = jnp.dot(oa_ref[...], wpa_ref[...], preferred_element_type=F32)
    pb = jnp.dot(ob_ref[...], wpb_ref[...], preferred_element_type=F32)
    m_ref[...] = (_sigmoid(ga_ref[...]) * pa + _sigmoid(gb_ref[...]) * pb).astype(BF16)


def _merge(o_g, oa, proj, gdn_norm, w_pa_b, w_pb_b, *, tm):
    n = o_g.shape[0]
    tm = _tile(n, tm, 16)
    tn = 1024
    nj = D_MODEL // tn
    return pl.pallas_call(
        _merge_kernel,
        grid=(n // tm, nj),
        in_specs=[pl.BlockSpec((tm, GDN_HK), lambda i, j: (i, 0)),
                  pl.BlockSpec((tm, GDN_HK), lambda i, j: (i, SEG_Z // GDN_HK)),
                  pl.BlockSpec((tm, MLA_HEADS * V_HEAD), lambda i, j: (i, 0)),
                  pl.BlockSpec((tm, tn), lambda i, j: (i, SEG_GA // tn + j)),
                  pl.BlockSpec((tm, tn), lambda i, j: (i, SEG_GB // tn + j)),
                  pl.BlockSpec((1, GDN_DV), lambda i, j: (0, 0)),
                  pl.BlockSpec((MLA_HEADS * V_HEAD, tn), lambda i, j: (0, j)),
                  pl.BlockSpec((GDN_HK, tn), lambda i, j: (0, j))],
        out_specs=pl.BlockSpec((tm, tn), lambda i, j: (i, j)),
        out_shape=jax.ShapeDtypeStruct((n, D_MODEL), BF16),
        scratch_shapes=[pltpu.VMEM((tm, GDN_HK), BF16)],
        compiler_params=_cparams(("arbitrary", "arbitrary")),
        name="gated_merge",
    )(o_g, proj, oa, proj, proj, gdn_norm, w_pa_b, w_pb_b)


def _out_proj_kernel(mp_ref, xp_ref, ms_ref, xs_ref, wo_ref, gf_ref, wrt_ref, h_ref, hn_ref, lg_ref, *, np_tiles):
    i = pl.program_id(0)

    def rows(m_ref, x_ref, n):
        h = x_ref[...] + jnp.dot(m_ref[...], wo_ref[...], preferred_element_type=F32)
        h_ref[0:n, :] = h
        hn = _rms_rows(h, gf_ref[...])
        hn_ref[0:n, :] = hn
        lg_ref[0:n, :] = jnp.dot(hn.astype(BF16), wrt_ref[...], preferred_element_type=F32)

    @pl.when(i < np_tiles)
    def _():
        rows(mp_ref, xp_ref, mp_ref.shape[0])

    @pl.when(i == np_tiles)
    def _():
        rows(ms_ref, xs_ref, ms_ref.shape[0])


def _out_proj(m_p, x_p, m_s, x_s, w_o_b, norm_ffn, w_rt_b, *, tm):
    n_p, n_s = m_p.shape[0], m_s.shape[0]
    tm = _tile(n_p, tm, 16)
    np_tiles = n_p // tm
    assert n_s <= tm
    const = lambda a: pl.BlockSpec(a.shape, lambda i: (0,) * a.ndim)
    p_spec = pl.BlockSpec((tm, D_MODEL), lambda i: (jnp.minimum(i, np_tiles - 1), 0))
    out = lambda w: pl.BlockSpec((tm, w), lambda i: (i, 0))
    return pl.pallas_call(
        functools.partial(_out_proj_kernel, np_tiles=np_tiles),
        grid=(np_tiles + 1,),
        in_specs=[p_spec, p_spec, const(m_s), const(x_s), const(w_o_b), const(norm_ffn), const(w_rt_b)],
        out_specs=[out(D_MODEL), out(D_MODEL), out(LANES)],
        out_shape=[jax.ShapeDtypeStruct((n_p + n_s, D_MODEL), F32),
                   jax.ShapeDtypeStruct((n_p + n_s, D_MODEL), F32),
                   jax.ShapeDtypeStruct((n_p + n_s, LANES), F32)],
        compiler_params=_cparams(("arbitrary",)),
        name="out_proj_router",
    )(m_p, x_p, m_s, x_s, w_o_b, norm_ffn, w_rt_b)


def _expert_weights(sched_ref, w_hbm, wbuf, wb_ref, sem):
    i = pl.program_id(0)

    def fetch(expert, slot):
        return pltpu.make_async_copy(w_hbm.at[expert], wbuf.at[slot], sem.at[slot])

    @pl.when(i == 0)
    def _():
        fetch(sched_ref[0, 0], 0).start()

    @pl.when(sched_ref[2, i] != 0)
    def _():
        slot = sched_ref[3, i]
        fetch(sched_ref[0, i], slot).wait()

        @pl.when(sched_ref[4, i] >= 0)
        def _():
            fetch(sched_ref[4, i], 1 - slot).start()

        wb_ref[...] = wbuf[slot].astype(BF16)


def _gmm_kernel(sched_ref, x_ref, w_hbm, o_ref, wbuf, wb_ref, sem):
    i = pl.program_id(0)
    _expert_weights(sched_ref, w_hbm, wbuf, wb_ref, sem)

    @pl.when(sched_ref[1, i] != 0)
    def _():
        o_ref[...] = jnp.dot(x_ref[...].astype(BF16), wb_ref[...], preferred_element_type=F32)

    @pl.when(sched_ref[1, i] == 0)
    def _():
        o_ref[...] = jnp.zeros(o_ref.shape, o_ref.dtype)


def _gmm_down_kernel(sched_ref, g_ref, u_ref, w_hbm, acc_hbm, o_ref, wbuf, wb_ref, sem):
    del acc_hbm
    i = pl.program_id(0)
    _expert_weights(sched_ref, w_hbm, wbuf, wb_ref, sem)

    @pl.when(sched_ref[1, i] != 0)
    def _():
        gate = g_ref[...]
        act = gate * _sigmoid(gate) * u_ref[...]
        o_ref[...] = jnp.dot(act.astype(BF16), wb_ref[...], preferred_element_type=F32)

    @pl.when(sched_ref[1, i] == 0)
    def _():
        o_ref[...] = jnp.zeros(o_ref.shape, o_ref.dtype)


def _expert_schedule(bexp, used):
    nb = bexp.shape[0]
    first = jnp.concatenate([jnp.ones((1,), jnp.int32), (bexp[1:] != bexp[:-1]).astype(jnp.int32)])
    run = jnp.cumsum(first) - 1
    run_expert = jnp.zeros((nb,), jnp.int32).at[run].set(bexp)
    nxt = jnp.where(run + 1 <= run[-1], run_expert[jnp.minimum(run + 1, nb - 1)], -1)
    return jnp.stack([bexp, used, first, run % 2, nxt]).astype(jnp.int32)


def _grouped_call(kern, sched, row_inputs, w, *, name, out_into=None, block_offset=0):
    rows = row_inputs[0].shape[0]
    k, n = w.shape[1], w.shape[2]
    nb = rows // MOE_ROWS
    in_specs = ([pl.BlockSpec((MOE_ROWS, a.shape[1]), lambda i, sc: (i, 0)) for a in row_inputs]
                + [pl.BlockSpec(memory_space=pl.ANY)])
    args = [sched, *row_inputs, w]
    aliases = {}
    if out_into is not None:
        in_specs.append(pl.BlockSpec(memory_space=pl.ANY))
        aliases = {len(args): 0}
        args.append(out_into)
        rows = out_into.shape[0]
    grid_spec = pltpu.PrefetchScalarGridSpec(
        num_scalar_prefetch=1,
        grid=(nb,),
        in_specs=in_specs,
        out_specs=pl.BlockSpec((MOE_ROWS, n), lambda i, sc: (i + block_offset, 0)),
        scratch_shapes=[pltpu.VMEM((2, k, n), F32), pltpu.VMEM((k, n), BF16), pltpu.SemaphoreType.DMA((2,))])
    return pl.pallas_call(
        kern, grid_spec=grid_spec,
        out_shape=jax.ShapeDtypeStruct((rows, n), F32),
        input_output_aliases=aliases,
        compiler_params=_cparams(("arbitrary",)),
        name=name,
    )(*args)


def _l2n(x):
    return x * lax.rsqrt(jnp.sum(x * x, axis=-1, keepdims=True) + EPS)


def _rms(x, g):
    return x * lax.rsqrt(jnp.mean(x * x, axis=-1, keepdims=True) + EPS) * g


def _rope_tables(pos):
    inv = ROPE_THETA ** (-jnp.arange(0, QK_ROPE, 2, dtype=F32) / QK_ROPE)
    ang = pos[:, None] * inv[None, :]
    cos, sin = jnp.cos(ang), jnp.sin(ang)
    return jnp.concatenate([cos, cos, cos, cos], axis=1), jnp.concatenate([-sin, sin, -sin, sin], axis=1)


def _w_in_segments():
    segs = []
    for h in range(MLA_HEADS):
        segs.append((h * QK_HEAD, QK_NOPE, SEG_QN + h * QK_NOPE))
        segs.append((h * QK_HEAD + QK_NOPE, QK_ROPE, SEG_QR + h * QK_ROPE))
    o = MLA_HEADS * QK_HEAD
    segs.append((o, KV_LORA, SEG_CKV)); o += KV_LORA
    segs.append((o, QK_ROPE, SEG_MISC)); o += QK_ROPE
    segs.append((o, GDN_QKV_W, SEG_QKV)); o += GDN_QKV_W
    segs.append((o, GDN_HEADS, SEG_MISC + MISC_A)); o += GDN_HEADS
    segs.append((o, GDN_HEADS, SEG_MISC + MISC_B)); o += GDN_HEADS
    segs.append((o, GDN_HEADS * GDN_DV, SEG_Z)); o += GDN_HEADS * GDN_DV
    segs.append((o, D_MODEL, SEG_GA)); o += D_MODEL
    segs.append((o, D_MODEL, SEG_GB))
    return segs


def _reorder_kernel(w_ref, o_ref):
    for src, width, dst in _w_in_segments():
        o_ref[:, dst:dst + width] = w_ref[:, src:src + width].astype(BF16)
    tail = SEG_MISC + MISC_B + GDN_HEADS
    o_ref[:, tail:D_PROJ] = jnp.zeros((o_ref.shape[0], D_PROJ - tail), BF16)


def _reorder_w_in_pallas(w_in):
    d, d_in = w_in.shape
    tr = 128
    return pl.pallas_call(
        _reorder_kernel,
        grid=(d // tr,),
        in_specs=[pl.BlockSpec((tr, d_in), lambda i: (i, 0))],
        out_specs=pl.BlockSpec((tr, D_PROJ), lambda i: (i, 0)),
        out_shape=jax.ShapeDtypeStruct((d, D_PROJ), BF16),
        compiler_params=_cparams(("arbitrary",)),
        name="w_in_reorder",
    )(w_in)


def _reorder_w_in(w_in):
    o = 0
    wq = w_in[:, o:o + MLA_HEADS * QK_HEAD].reshape(D_MODEL, MLA_HEADS, QK_HEAD); o += MLA_HEADS * QK_HEAD
    wckv = w_in[:, o:o + KV_LORA]; o += KV_LORA
    wkr = w_in[:, o:o + QK_ROPE]; o += QK_ROPE
    wqkv = w_in[:, o:o + GDN_QKV_W]; o += GDN_QKV_W
    wa = w_in[:, o:o + GDN_HEADS]; o += GDN_HEADS
    wb = w_in[:, o:o + GDN_HEADS]; o += GDN_HEADS
    wz = w_in[:, o:o + GDN_HEADS * GDN_DV]; o += GDN_HEADS * GDN_DV
    wga = w_in[:, o:o + D_MODEL]; o += D_MODEL
    wgb = w_in[:, o:o + D_MODEL]
    pad = jnp.zeros((D_MODEL, D_PROJ - SEG_MISC - QK_ROPE - 2 * GDN_HEADS), w_in.dtype)
    return jnp.concatenate([
        wz, wga, wgb, wqkv, wq[:, :, :QK_NOPE].reshape(D_MODEL, -1), wq[:, :, QK_NOPE:].reshape(D_MODEL, -1),
        wckv, wkr, wa, wb, pad], axis=1).astype(BF16)


def _lane_vec(v, offset):
    return jnp.zeros((1, LANES), F32).at[0, offset:offset + v.shape[0]].set(v.astype(F32))


def _hier_moe(h, hn, logits, b_rg, b_re, w_gate, w_up, w_down):
    t = h.shape[0]
    n_rt = N_GROUPS + N_EXPERTS
    g_logits = logits[:, :N_GROUPS] + b_rg.astype(F32)
    p_group = jax.nn.softmax(g_logits, axis=-1)
    grp = jnp.argmax(g_logits, axis=-1).astype(jnp.int32)
    p_grp_sel = jnp.take_along_axis(p_group, grp[:, None], axis=-1)[:, 0]
    e_logits = (logits[:, N_GROUPS:n_rt] + b_re.astype(F32)).reshape(t, N_GROUPS, EXP_PER_GROUP)
    e_sel = jnp.take_along_axis(e_logits, grp[:, None, None], axis=1)[:, 0]
    top_w, top_i = lax.top_k(jax.nn.softmax(e_sel, axis=-1), TOP_K)
    top_w = top_w / jnp.sum(top_w, axis=-1, keepdims=True) * p_grp_sel[:, None]
    expert = grp[:, None] * EXP_PER_GROUP + top_i.astype(jnp.int32)
    n_assign = t * TOP_K
    e_flat = expert.reshape(n_assign)
    w_flat = top_w.reshape(n_assign)
    onehot = (e_flat[:, None] == jnp.arange(N_EXPERTS, dtype=jnp.int32)[None, :]).astype(jnp.int32)
    csum = jnp.cumsum(onehot, axis=0)
    rank = jnp.take_along_axis(csum, e_flat[:, None], axis=1)[:, 0] - 1
    counts = csum[-1]
    pcounts = (counts + MOE_ROWS - 1) // MOE_ROWS * MOE_ROWS
    pends = jnp.cumsum(pcounts)
    pstarts = pends - pcounts
    dest = pstarts[e_flat] + rank
    n_blocks = -(-(-(-n_assign // MOE_ROWS) + N_EXPERTS) // MOE_SPLITS) * MOE_SPLITS
    n_rows = n_blocks * MOE_ROWS
    tok = jnp.arange(n_assign, dtype=jnp.int32) // TOP_K
    src = jnp.zeros((n_rows,), jnp.int32).at[dest].set(tok)
    bstart = jnp.arange(n_blocks, dtype=jnp.int32) * MOE_ROWS
    bexp = jnp.minimum(jnp.searchsorted(pends, bstart, side='right'), N_EXPERTS - 1).astype(jnp.int32)
    used = (bstart < pends[-1]).astype(jnp.int32)
    bq = n_blocks // MOE_SPLITS
    yb = jnp.zeros((n_rows, D_MODEL), F32)
    for q in range(MOE_SPLITS):
        xb = hn[src[q * bq * MOE_ROWS:(q + 1) * bq * MOE_ROWS]]
        sched = _expert_schedule(bexp[q * bq:(q + 1) * bq], used[q * bq:(q + 1) * bq])
        gate = _grouped_call(_gmm_kernel, sched, [xb], w_gate, name="moe_gate")
        up = _grouped_call(_gmm_kernel, sched, [xb], w_up, name="moe_up")
        yb = _grouped_call(_gmm_down_kernel, sched, [gate, up], w_down, name="moe_down",
                           out_into=yb, block_offset=q * bq)
    dest2 = dest.reshape(t, TOP_K)
    y = h
    for k in range(TOP_K):
        y = y + yb[dest2[:, k]] * top_w[:, k:k + 1]
    return y


def kernel(x_prompt, x_sample, cache_ckv, cache_krope, state_ssm, state_conv, page_table, norm_attn, w_in, norm_ckv, w_uk, w_uv, q_gain_nope, q_gain_rope, k_gain_nope, k_gain_rope, conv_w, gdn_a_log, gdn_dt_bias, gdn_norm, w_pa, w_pb, w_o, norm_ffn, w_router_group, b_router_group, w_router_expert, b_router_expert, w_gate, w_up, w_down):
    depth = w_in.shape[0]
    bsz, seq, _ = x_prompt.shape
    n_s, s_new, _ = x_sample.shape
    assert s_new == 1
    n_p = bsz * seq
    n_all = n_p + n_s
    n_past = page_table.shape[1] * PAGE_SIZE
    cos_p, sin_p = _rope_tables(jnp.tile(jnp.arange(seq, dtype=F32), bsz))
    cos_s, sin_s = _rope_tables(jnp.full((n_s,), n_past, F32))
    tq = _tile(seq, 512, LANES)
    xp = x_prompt.reshape(n_p, D_MODEL)
    xs = x_sample.reshape(n_s, D_MODEL)
    outs = [[] for _ in range(8)]
    row = lambda v: v.astype(F32)[None, :]
    for l in range(depth):
        w_in_r = _reorder_w_in_pallas(w_in[l])
        w_uk_b = w_uk[l].astype(BF16)
        w_uvt_b = w_uv[l].T.astype(BF16)
        wukt = w_uk_b.T
        w_pa_b, w_pb_b, w_o_b = w_pa[l].astype(BF16), w_pb[l].astype(BF16), w_o[l].astype(BF16)
        w_rt_b = jnp.concatenate([w_router_group[l], w_router_expert[l],
                                  jnp.zeros((D_MODEL, LANES - N_GROUPS - N_EXPERTS), F32)], axis=1).astype(BF16)
        gains = (row(q_gain_nope[l]), jnp.tile(row(q_gain_rope[l]), (1, 2)), row(k_gain_nope[l]),
                 jnp.tile(row(k_gain_rope[l]), (1, 2)), row(norm_ckv[l]))
        alog_l = _lane_vec(gdn_a_log[l], MISC_A)
        dtb_l = _lane_vec(gdn_dt_bias[l], MISC_A)
        proj_p = _in_proj(xp, row(norm_attn[l]), w_in_r, tm=1024)
        proj_s = _in_proj(xs, row(norm_attn[l]), w_in_r, tm=n_s)

        qcat_p, kcat_p, c_p, kr_p, vt_p = _mla_prep(proj_p, cos_p, sin_p, *gains, w_uk_b, w_uvt_b,
                                                    tm=tq, with_vt=True, with_qg=False)
        oa_p = _flash_attention(qcat_p, kcat_p, vt_p, batch=bsz, seq=seq, tq=tq)

        qcat_s, kcat_s, c_s, kr_s, qg_s = _mla_prep(proj_s, cos_s, sin_s, *gains, w_uk_b, w_uvt_b,
                                                    tm=n_s, with_vt=False, with_qg=True)
        qa = _headwise_matmul(qg_s, wukt, heads=MLA_HEADS, b_rows_by_head=True, name="q_absorb")
        o_lat = _paged_attention(page_table, cache_ckv[l], jnp.swapaxes(cache_krope[l], 1, 2), wukt,
                                 qa.astype(BF16).reshape(n_s, MLA_HEADS, KV_LORA),
                                 qcat_s.reshape(n_s, MLA_HEADS, QK_PAD), kcat_s.reshape(n_s, MLA_HEADS, QK_PAD),
                                 c_s[:, None, :])
        oa_s = _headwise_matmul(o_lat.astype(BF16).reshape(n_s, MLA_HEADS * KV_LORA), w_uv[l].astype(BF16),
                                heads=MLA_HEADS, b_rows_by_head=False, name="v_absorb").astype(BF16)

        qkv_p, kt_p, gates_p = _gdn_prep(proj_p, conv_w[l], alog_l, dtb_l, batch=bsz, seq=seq, tm=512)
        o_g, ssm_p = _delta_rule(qkv_p, kt_p, gates_p, batch=bsz, seq=seq)
        conv_p = proj_p.reshape(bsz, seq, D_PROJ)[:, seq - (CONV_W - 1):, SEG_QKV:SEG_QKV + GDN_QKV_W]

        qkv_raw_s = proj_s[:, SEG_QKV:SEG_QKV + GDN_QKV_W]
        xpad_s = jnp.concatenate([state_conv[l].astype(F32), qkv_raw_s[:, None, :]], axis=1)
        acc_s = xpad_s[:, 0] * conv_w[l, 0]
        for w in range(1, CONV_W):
            acc_s = acc_s + xpad_s[:, w] * conv_w[l, w]
        qkv_s = jax.nn.silu(acc_s)
        conv_s = xpad_s[:, 1:]
        q_s = _l2n(qkv_s[:, :GDN_HK].reshape(n_s, GDN_HEADS, GDN_DK)) * (GDN_DK ** -0.5)
        k_s = _l2n(qkv_s[:, GDN_HK:2 * GDN_HK].reshape(n_s, GDN_HEADS, GDN_DK))
        v_s = qkv_s[:, 2 * GDN_HK:].reshape(n_s, GDN_HEADS, GDN_DV)
        a_s = proj_s[:, SEG_MISC + MISC_A:SEG_MISC + MISC_A + GDN_HEADS]
        b_s = proj_s[:, SEG_MISC + MISC_B:SEG_MISC + MISC_B + GDN_HEADS]
        g_s = -jnp.exp(gdn_a_log[l].astype(F32)) * jax.nn.softplus(a_s + gdn_dt_bias[l].astype(F32))
        beta_s = jax.nn.sigmoid(b_s)
        eg = jnp.exp(g_s)[..., None]
        s0 = state_ssm[l].astype(F32)
        kcd = k_s * beta_s[..., None] * eg
        v_new = v_s * beta_s[..., None] - jnp.einsum('bhk,bhkv->bhv', kcd, s0)
        qk = jnp.einsum('bhd,bhd->bh', q_s, k_s)[..., None]
        o_s = jnp.einsum('bhk,bhkv->bhv', q_s * eg, s0) + qk * v_new
        ssm_s = s0 * eg[..., None] + jnp.einsum('bhk,bhv->bhkv', k_s, v_new)

        m_p = _merge(o_g, oa_p, proj_p, row(gdn_norm[l]), w_pa_b, w_pb_b, tm=512)
        m_s = _merge(o_s.reshape(n_s, GDN_HK), oa_s, proj_s, row(gdn_norm[l]), w_pa_b, w_pb_b, tm=n_s)
        h_all, hn_all, lg_all = _out_proj(m_p, xp, m_s, xs, w_o_b, row(norm_ffn[l]), w_rt_b, tm=512)
        y_all = _hier_moe(h_all, hn_all, lg_all, b_router_group[l], b_router_expert[l],
                          w_gate[l], w_up[l], w_down[l])
        xp, xs = y_all[:n_p], y_all[n_p:]

        for lst, val in zip(outs, (c_p.reshape(bsz, seq, KV_LORA), kr_p[:, :QK_ROPE].reshape(bsz, seq, QK_ROPE),
                                   c_s.reshape(n_s, 1, KV_LORA), kr_s[:, :QK_ROPE].reshape(n_s, 1, QK_ROPE),
                                   ssm_p, conv_p, ssm_s, conv_s)):
            lst.append(val)
    return (xp.reshape(bsz, seq, D_MODEL), xs.reshape(n_s, 1, D_MODEL)) + tuple(jnp.stack(o) for o in outs)
```

```python
import functools

import jax
import jax.numpy as jnp
from jax import lax
from jax.experimental import pallas as pl
from jax.experimental.pallas import tpu as pltpu

F32 = jnp.float32
BF16 = jnp.bfloat16

D_MODEL = 2048
PAGE_SIZE = 128
MLA_HEADS = 16
QK_NOPE = 128
QK_ROPE = 64
QK_HEAD = QK_NOPE + QK_ROPE
QK_PAD = 256
V_HEAD = 128
KV_LORA = 512
ROPE_THETA = 10000.0
GDN_HEADS = 16
GDN_DK = 128
GDN_DV = 128
GDN_HK = GDN_HEADS * GDN_DK
GDN_QKV_W = GDN_HEADS * (2 * GDN_DK + GDN_DV)
CONV_W = 4
GDN_CHUNK = 128
N_GROUPS = 8
EXP_PER_GROUP = 8
N_EXPERTS = N_GROUPS * EXP_PER_GROUP
TOP_K = 2
D_EXPERT = 1408
MOE_ROWS = 128
MOE_SPLITS = 4
EPS = 1e-6
LANES = 128
SUBLANES = 8

V7X_VMEM_LIMIT = 56 * 1024 * 1024

SEG_Z = 0
SEG_GA = SEG_Z + GDN_HEADS * GDN_DV
SEG_GB = SEG_GA + D_MODEL
SEG_QKV = SEG_GB + D_MODEL
SEG_QN = SEG_QKV + GDN_QKV_W
SEG_QR = SEG_QN + MLA_HEADS * QK_NOPE
SEG_CKV = SEG_QR + MLA_HEADS * QK_ROPE
SEG_MISC = SEG_CKV + KV_LORA
PROJ_TN = 1024
D_PROJ = -(-(SEG_MISC + LANES) // PROJ_TN) * PROJ_TN
MISC_A = QK_ROPE
MISC_B = QK_ROPE + GDN_HEADS


def _cparams(sem, vmem=V7X_VMEM_LIMIT):
    return pltpu.CompilerParams(dimension_semantics=sem, vmem_limit_bytes=vmem)


def _tile(dim, cap, align):
    if dim <= cap:
        return dim
    for t in range(cap - cap % align, 0, -align):
        if dim % t == 0:
            return t
    raise ValueError(f"no {align}-aligned tile <= {cap} divides {dim}")


def _nt_dot(a, b):
    return lax.dot_general(a, b, (((1,), (1,)), ((), ())), preferred_element_type=F32)


def _rms_rows(x, gain):
    return x * lax.rsqrt(jnp.mean(x * x, axis=-1, keepdims=True) + EPS) * gain


def _sigmoid(x):
    return 1.0 / (1.0 + jnp.exp(-x))


def _in_proj_kernel(x_ref, g_ref, w_ref, o_ref, u_ref):
    @pl.when(pl.program_id(1) == 0)
    def _():
        u_ref[...] = _rms_rows(x_ref[...], g_ref[...]).astype(BF16)

    o_ref[...] = jnp.dot(u_ref[...], w_ref[...], preferred_element_type=F32)


def _in_proj(x, gain, w, *, tm):
    m, d = x.shape
    n = w.shape[1]
    tm = _tile(m, tm, 16)
    tn = PROJ_TN
    assert n % tn == 0
    return pl.pallas_call(
        _in_proj_kernel,
        grid=(m // tm, n // tn),
        in_specs=[pl.BlockSpec((tm, d), lambda i, j: (i, 0)),
                  pl.BlockSpec((1, d), lambda i, j: (0, 0)),
                  pl.BlockSpec((d, tn), lambda i, j: (0, j))],
        out_specs=pl.BlockSpec((tm, tn), lambda i, j: (i, j)),
        out_shape=jax.ShapeDtypeStruct((m, n), F32),
        scratch_shapes=[pltpu.VMEM((tm, d), BF16)],
        compiler_params=_cparams(("arbitrary", "arbitrary")),
        name="in_proj",
    )(x, gain, w)


def _mm_kernel(a_ref, b_ref, o_ref):
    o_ref[...] = jnp.dot(a_ref[...], b_ref[...], preferred_element_type=F32).astype(o_ref.dtype)


def _headwise_matmul(a, b, *, heads, b_rows_by_head, name):
    m = a.shape[0]
    k = a.shape[1] // heads
    if b_rows_by_head:
        n = b.shape[1]
        b_spec = pl.BlockSpec((k, n), lambda h: (h, 0))
    else:
        n = b.shape[1] // heads
        b_spec = pl.BlockSpec((k, n), lambda h: (0, h))
    return pl.pallas_call(
        _mm_kernel,
        grid=(heads,),
        in_specs=[pl.BlockSpec((m, k), lambda h: (0, h)), b_spec],
        out_specs=pl.BlockSpec((m, n), lambda h: (0, h)),
        out_shape=jax.ShapeDtypeStruct((m, heads * n), F32),
        compiler_params=_cparams(("arbitrary",)),
        name=name,
    )(a, b)


def _swap_halves(y, lane):
    half = QK_ROPE // 2
    return jnp.where(lane % QK_ROPE < half, pltpu.roll(y, LANES - half, axis=1), pltpu.roll(y, half, axis=1))


def _rope_pair(x, gain2, cos4, sin4, lane):
    lo = lane < QK_ROPE
    ss = x * x
    s_lo = jnp.sum(jnp.where(lo, ss, 0.0), axis=-1, keepdims=True)
    s_hi = jnp.sum(jnp.where(lo, 0.0, ss), axis=-1, keepdims=True)
    r = jnp.where(lo, lax.rsqrt(s_lo * (1.0 / QK_ROPE) + EPS), lax.rsqrt(s_hi * (1.0 / QK_ROPE) + EPS))
    y = x * r * gain2
    return y * cos4 + _swap_halves(y, lane) * sin4


def _mla_prep_kernel(qn_ref, qr_ref, ckv_ref, misc_ref, cos_ref, sin_ref, gqn_ref, gqr_ref, gkn_ref, gkr_ref,
                     gc_ref, wuk_ref, *rest, with_vt, with_qg):
    rest = list(rest)
    wuvt_ref = rest.pop(0) if with_vt else None
    qcat_ref, kcat_ref, c_ref, kr_ref = rest[:4]
    rest = rest[4:]
    vt_ref = rest.pop(0) if with_vt else None
    qg_ref = rest.pop(0) if with_qg else None
    rows = qn_ref.shape[0]
    lane = lax.broadcasted_iota(jnp.int32, (rows, LANES), 1)
    lo = lane < QK_ROPE
    cos4 = cos_ref[...]
    sin4 = sin_ref[...]

    c = _rms_rows(ckv_ref[...], gc_ref[...])
    c_ref[...] = c
    cb = c.astype(BF16)
    kr = jnp.where(lo, _rope_pair(misc_ref[...], gkr_ref[...], cos4, sin4, lane), 0.0)
    kr_ref[...] = kr
    krb = kr.astype(BF16)

    gqn = gqn_ref[...]
    gkn = gkn_ref[...]
    for h in range(MLA_HEADS):
        qn = _rms_rows(qn_ref[:, h * QK_NOPE:(h + 1) * QK_NOPE], gqn)
        qcat_ref[:, h * QK_PAD:h * QK_PAD + QK_NOPE] = qn.astype(BF16)
        if with_qg:
            qg_ref[:, h * QK_NOPE:(h + 1) * QK_NOPE] = (qn * gkn).astype(BF16)
    gqr = gqr_ref[...]
    for j in range(MLA_HEADS // 2):
        rot = _rope_pair(qr_ref[:, j * LANES:(j + 1) * LANES], gqr, cos4, sin4, lane)
        even = jnp.where(lo, rot, 0.0)
        odd = jnp.where(lo, pltpu.roll(rot, QK_ROPE, axis=1), 0.0)
        qcat_ref[:, (2 * j) * QK_PAD + QK_NOPE:(2 * j + 1) * QK_PAD] = even.astype(BF16)
        qcat_ref[:, (2 * j + 1) * QK_PAD + QK_NOPE:(2 * j + 2) * QK_PAD] = odd.astype(BF16)

    kn = jnp.dot(cb, wuk_ref[...], preferred_element_type=F32)
    for h in range(MLA_HEADS):
        kh = _rms_rows(kn[:, h * QK_NOPE:(h + 1) * QK_NOPE], gkn)
        kcat_ref[:, h * QK_PAD:h * QK_PAD + QK_NOPE] = kh.astype(BF16)
        kcat_ref[:, h * QK_PAD + QK_NOPE:(h + 1) * QK_PAD] = krb
    if with_vt:
        vt_ref[...] = _nt_dot(wuvt_ref[...], cb).astype(BF16)


def _mla_prep(proj, cos4, sin4, gqn, gqr2, gkn, gkr2, gckv, w_uk_b, w_uvt_b, *, tm, with_vt, with_qg):
    n = proj.shape[0]
    tm = _tile(n, tm, 16)
    nt = n // tm
    hq = MLA_HEADS * QK_NOPE
    row = lambda w, off: pl.BlockSpec((tm, w), lambda i, off=off, w=w: (i, off // w))
    const = lambda a: pl.BlockSpec(a.shape, lambda i: (0,) * a.ndim)
    in_specs = [row(hq, SEG_QN), row(MLA_HEADS * QK_ROPE, SEG_QR), row(KV_LORA, SEG_CKV), row(LANES, SEG_MISC),
                pl.BlockSpec((tm, LANES), lambda i: (i, 0)), pl.BlockSpec((tm, LANES), lambda i: (i, 0)),
                const(gqn), const(gqr2), const(gkn), const(gkr2), const(gckv), const(w_uk_b)]
    args = [proj, proj, proj, proj, cos4, sin4, gqn, gqr2, gkn, gkr2, gckv, w_uk_b]
    if with_vt:
        in_specs.append(const(w_uvt_b))
        args.append(w_uvt_b)
    out_specs = [pl.BlockSpec((tm, MLA_HEADS * QK_PAD), lambda i: (i, 0)),
                 pl.BlockSpec((tm, MLA_HEADS * QK_PAD), lambda i: (i, 0)),
                 pl.BlockSpec((tm, KV_LORA), lambda i: (i, 0)),
                 pl.BlockSpec((tm, LANES), lambda i: (i, 0))]
    out_shape = [jax.ShapeDtypeStruct((n, MLA_HEADS * QK_PAD), BF16),
                 jax.ShapeDtypeStruct((n, MLA_HEADS * QK_PAD), BF16),
                 jax.ShapeDtypeStruct((n, KV_LORA), F32),
                 jax.ShapeDtypeStruct((n, LANES), F32)]
    if with_vt:
        out_specs.append(pl.BlockSpec((None, MLA_HEADS * V_HEAD, tm), lambda i: (i, 0, 0)))
        out_shape.append(jax.ShapeDtypeStruct((nt, MLA_HEADS * V_HEAD, tm), BF16))
    if with_qg:
        out_specs.append(pl.BlockSpec((tm, hq), lambda i: (i, 0)))
        out_shape.append(jax.ShapeDtypeStruct((n, hq), BF16))
    return pl.pallas_call(
        functools.partial(_mla_prep_kernel, with_vt=with_vt, with_qg=with_qg),
        grid=(nt,), in_specs=in_specs, out_specs=out_specs, out_shape=out_shape,
        compiler_params=_cparams(("arbitrary",)),
        name="mla_prep",
    )(*args)


def _flash_kernel(q_ref, k_ref, vt_ref, o_ref, m_ref, l_ref, acc_ref, *, tq, scale):
    qi = pl.program_id(2)
    q = q_ref[...]
    m_ref[...] = jnp.full(m_ref.shape, -jnp.inf, F32)
    l_ref[...] = jnp.zeros(l_ref.shape, F32)
    acc_ref[...] = jnp.zeros(acc_ref.shape, F32)

    c2 = scale * 1.4426950408889634

    def block(ki, nblk, masked):
        start = pl.multiple_of(ki * tq, tq)
        k = k_ref[pl.ds(start, nblk * tq), :]
        s = _nt_dot(k, q)
        if masked:
            key = lax.broadcasted_iota(jnp.int32, s.shape, 0)
            qry = lax.broadcasted_iota(jnp.int32, s.shape, 1)
            s = jnp.where(key <= qry, s, -jnp.inf)
        m_old = m_ref[...]
        m_new = jnp.maximum(m_old, jnp.max(s, axis=0, keepdims=True))
        alpha = jnp.exp2((m_old - m_new) * c2)
        p32 = jnp.exp2((s - m_new) * c2)
        l_ref[...] = l_ref[...] * alpha + jnp.sum(p32, axis=0, keepdims=True)
        p = p32.astype(BF16)
        pv = jnp.dot(vt_ref[ki], p[0:tq], preferred_element_type=F32)
        for j in range(1, nblk):
            pv = pv + jnp.dot(vt_ref[ki + j], p[j * tq:(j + 1) * tq], preferred_element_type=F32)
        acc_ref[...] = acc_ref[...] * alpha + pv
        m_ref[...] = m_new

    def body(kp, carry):
        block(2 * kp, 2, False)
        return carry

    lax.fori_loop(0, qi // 2, body, 0)

    @pl.when(qi % 2 == 1)
    def _():
        block(qi - 1, 1, False)

    block(qi, 1, True)
    o_ref[...] = (acc_ref[...] / l_ref[...]).T.astype(o_ref.dtype)


def _flash_attention(q_cat, k_cat, vt, *, batch, seq, tq):
    nq = seq // tq
    kern = functools.partial(_flash_kernel, tq=tq, scale=QK_HEAD ** -0.5)
    return pl.pallas_call(
        kern,
        grid=(batch, MLA_HEADS, nq),
        in_specs=[pl.BlockSpec((tq, QK_PAD), lambda b, h, i: (b * nq + i, h)),
                  pl.BlockSpec((seq, QK_PAD), lambda b, h, i: (b, h)),
                  pl.BlockSpec((nq, V_HEAD, tq), lambda b, h, i: (b, h, 0))],
        out_specs=pl.BlockSpec((tq, V_HEAD), lambda b, h, i: (b * nq + i, h)),
        out_shape=jax.ShapeDtypeStruct((batch * seq, MLA_HEADS * V_HEAD), BF16),
        scratch_shapes=[pltpu.VMEM((1, tq), F32), pltpu.VMEM((1, tq), F32), pltpu.VMEM((V_HEAD, tq), F32)],
        compiler_params=_cparams(("arbitrary", "arbitrary", "arbitrary")),
        name="mla_prompt_attention",
    )(q_cat, k_cat, vt)


PAGES_PER_STEP = 16
PAGE_RING = 3
PAGE_GROUPS = 2


def _paged_kernel(pt_ref, ckv_hbm, krt_hbm, wukt_ref, qa_ref, qcat_ref, qcat_prev_ref, kcat_prev_ref, cn_prev_ref,
                  o_ref, lhs_ref, cbuf, krbuf, csem, krsem, c_scr, krt_scr, s_scr, m_ref, l_ref, acc_ref,
                  *, scale, steps, n_chunks):
    npg = PAGES_PER_STEP
    t = pl.program_id(0)
    n_hd = MLA_HEADS * QK_NOPE
    span = npg * PAGE_SIZE // PAGE_GROUPS
    prev_chunk = (t + steps - 1) % steps

    def page_copies(chunk, slot, resolve_pages):
        base = jnp.minimum(chunk, n_chunks - 1) * npg
        copies = []
        for i in range(npg):
            page = pt_ref[base + i] if resolve_pages else 0
            copies.append(pltpu.make_async_copy(ckv_hbm.at[page], cbuf.at[slot, i], csem.at[slot]))
            copies.append(pltpu.make_async_copy(krt_hbm.at[page], krbuf.at[slot, i], krsem.at[slot]))
        return copies

    @pl.when(t == 0)
    def _():
        for chunk in range(PAGE_RING - 1):
            for cp in page_copies(chunk, chunk, True):
                cp.start()
        lhs_ref[0:n_hd, :] = wukt_ref[...]
        c_scr[1] = jnp.zeros(c_scr.shape[1:], BF16)
        s_scr[1] = jnp.zeros(s_scr.shape[1:], F32)
        m_ref[...] = jnp.zeros(m_ref.shape, F32)
        l_ref[...] = jnp.zeros(l_ref.shape, F32)
        acc_ref[...] = jnp.zeros(acc_ref.shape, F32)

    @pl.when(t % steps == 0)
    def _():
        lhs_ref[n_hd:n_hd + MLA_HEADS, :] = qa_ref[...]

    ring = t % PAGE_RING

    def step(cur, prv):
        for cp in page_copies(t, ring, False):
            cp.wait()
        first = prev_chunk == 0
        s_prev = s_scr[prv]
        m_old = jnp.where(first, -jnp.inf, m_ref[...])
        l_old = jnp.where(first, 0.0, l_ref[...])
        acc_old = jnp.where(first, 0.0, acc_ref[...])
        m_new = jnp.maximum(m_old, jnp.max(s_prev, axis=-1, keepdims=True))
        alpha = jnp.exp(m_old - m_new)
        pr = jnp.exp(s_prev - m_new)
        l_ref[...] = l_old * alpha + jnp.sum(pr, axis=-1, keepdims=True)
        acc_ref[...] = acc_old * alpha + jnp.dot(pr.astype(BF16), c_scr[prv], preferred_element_type=F32)
        m_ref[...] = m_new
        for i in range(npg):
            c_scr[cur, i * PAGE_SIZE:(i + 1) * PAGE_SIZE, :] = cbuf[ring, i].astype(BF16)
            krt_scr[:, i * PAGE_SIZE:(i + 1) * PAGE_SIZE] = krbuf[ring, i].astype(BF16)
        qr = qcat_ref[:, QK_NOPE:QK_NOPE + QK_ROPE]
        for g in range(PAGE_GROUPS):
            c = c_scr[cur, g * span:(g + 1) * span, :]
            res = _nt_dot(lhs_ref[...], c)
            kt = res[0:n_hd, :].reshape(MLA_HEADS, QK_NOPE, span)
            inv = lax.rsqrt(jnp.sum(kt * kt, axis=1) * (1.0 / QK_NOPE) + EPS)
            s_rope = jnp.dot(qr, krt_scr[:, g * span:(g + 1) * span], preferred_element_type=F32)
            s_scr[cur, :, g * span:(g + 1) * span] = (res[n_hd:n_hd + MLA_HEADS, :] * inv + s_rope) * scale
        for cp in page_copies(t + PAGE_RING - 1, (t + PAGE_RING - 1) % PAGE_RING, True):
            cp.start()

    @pl.when(t % 2 == 0)
    def _():
        step(0, 1)

    @pl.when(t % 2 == 1)
    def _():
        step(1, 0)

    @pl.when((prev_chunk == steps - 1) & (t > 0))
    def _():
        s_new = jnp.sum(qcat_prev_ref[...].astype(F32) * kcat_prev_ref[...].astype(F32), axis=-1,
                        keepdims=True) * scale
        m_old = m_ref[...]
        m_fin = jnp.maximum(m_old, s_new)
        a2 = jnp.exp(m_old - m_fin)
        p_new = jnp.exp(s_new - m_fin)
        l_fin = l_ref[...] * a2 + p_new
        cn = cn_prev_ref[...].astype(BF16).astype(F32)
        o_ref[...] = (acc_ref[...] * a2 + p_new * cn) / l_fin

    @pl.when(t == n_chunks)
    def _():
        for ahead in range(1, PAGE_RING):
            for cp in page_copies(t + ahead, (t + ahead) % PAGE_RING, False):
                cp.wait()


def _paged_attention(page_table, ckv_pool, krt_pool, wukt, qa, q_cat, k_cat, cn):
    n_s, n_pages = page_table.shape
    npg = PAGES_PER_STEP
    assert n_pages % npg == 0
    steps = n_pages // npg
    n_chunks = n_s * steps
    span = npg * PAGE_SIZE
    pt_flat = page_table.reshape(n_s * n_pages)

    def cur_sample(shape):
        return pl.BlockSpec((None,) + shape, lambda t, pt: (jnp.minimum(t // steps, n_s - 1), 0, 0))

    def prev_sample(shape):
        return pl.BlockSpec((None,) + shape, lambda t, pt: (jnp.maximum(t - 1, 0) // steps, 0, 0))

    in_specs = [pl.BlockSpec(memory_space=pl.ANY), pl.BlockSpec(memory_space=pl.ANY),
                pl.BlockSpec(wukt.shape, lambda t, pt: (0, 0)),
                cur_sample((MLA_HEADS, KV_LORA)), cur_sample((MLA_HEADS, QK_PAD)),
                prev_sample((MLA_HEADS, QK_PAD)), prev_sample((MLA_HEADS, QK_PAD)), prev_sample((1, KV_LORA))]
    grid_spec = pltpu.PrefetchScalarGridSpec(
        num_scalar_prefetch=1,
        grid=(n_chunks + 1,),
        in_specs=in_specs,
        out_specs=prev_sample((MLA_HEADS, KV_LORA)),
        scratch_shapes=[pltpu.VMEM((MLA_HEADS * QK_NOPE + MLA_HEADS, KV_LORA), BF16),
                        pltpu.VMEM((PAGE_RING, npg, PAGE_SIZE, KV_LORA), F32),
                        pltpu.VMEM((PAGE_RING, npg, QK_ROPE, PAGE_SIZE), F32),
                        pltpu.SemaphoreType.DMA((PAGE_RING,)), pltpu.SemaphoreType.DMA((PAGE_RING,)),
                        pltpu.VMEM((2, span, KV_LORA), BF16), pltpu.VMEM((QK_ROPE, span), BF16),
                        pltpu.VMEM((2, MLA_HEADS, span), F32),
                        pltpu.VMEM((MLA_HEADS, 1), F32), pltpu.VMEM((MLA_HEADS, 1), F32),
                        pltpu.VMEM((MLA_HEADS, KV_LORA), F32)])
    return pl.pallas_call(
        functools.partial(_paged_kernel, scale=QK_HEAD ** -0.5, steps=steps, n_chunks=n_chunks),
        grid_spec=grid_spec,
        out_shape=jax.ShapeDtypeStruct((n_s, MLA_HEADS, KV_LORA), F32),
        compiler_params=_cparams(("arbitrary",)),
        name="mla_sample_attention",
    )(pt_flat, ckv_pool, krt_pool, wukt, qa, q_cat, q_cat, k_cat, cn)


def _gdn_prep_kernel(x_ref, halo_ref, misc_ref, cw_ref, alog_ref, dtb_ref, qkv_ref, kt_ref, gates_ref, xs_ref,
                     *, tiles_per_seq):
    i = pl.program_id(0)
    s = pl.program_id(1)
    tm = x_ref.shape[0]
    first = (i % tiles_per_seq) == 0

    @pl.when(first)
    def _():
        xs_ref[0:SUBLANES, :] = jnp.zeros((SUBLANES, xs_ref.shape[1]), F32)

    @pl.when(jnp.logical_not(first))
    def _():
        xs_ref[0:SUBLANES, :] = halo_ref[...]

    xs_ref[SUBLANES:SUBLANES + tm, :] = x_ref[...]
    acc = xs_ref[pl.ds(SUBLANES - (CONV_W - 1), tm), :] * cw_ref[0:1, :]
    for w in range(1, CONV_W):
        acc = acc + xs_ref[pl.ds(SUBLANES - (CONV_W - 1) + w, tm), :] * cw_ref[w:w + 1, :]
    y = acc * _sigmoid(acc)

    @pl.when(s == 2)
    def _():
        qkv_ref[...] = y

    @pl.when(s < 2)
    def _():
        scale = jnp.where(s == 0, GDN_DK ** -0.5, 1.0).astype(F32)
        for h in range(GDN_HEADS):
            yh = y[:, h * GDN_DK:(h + 1) * GDN_DK]
            yn = yh * (lax.rsqrt(jnp.sum(yh * yh, axis=-1, keepdims=True) + EPS) * scale)
            qkv_ref[:, h * GDN_DK:(h + 1) * GDN_DK] = yn

    @pl.when(s == 1)
    def _():
        kt_ref[...] = qkv_ref[...].T

    @pl.when(s == 0)
    def _():
        x = misc_ref[...] + dtb_ref[...]
        softplus = jnp.maximum(x, 0.0) + jnp.log(1.0 + jnp.exp(-jnp.abs(x)))
        lane = lax.broadcasted_iota(jnp.int32, x.shape, 1)
        is_beta = (lane >= MISC_B) & (lane < MISC_B + GDN_HEADS)
        gates_ref[...] = jnp.where(is_beta, _sigmoid(misc_ref[...]), -jnp.exp(alog_ref[...]) * softplus)


def _gdn_prep(proj, conv_w, alog_l, dtb_l, *, batch, seq, tm):
    n = batch * seq
    tm = _tile(seq, tm, LANES)
    tps = seq // tm
    sec0 = SEG_QKV // GDN_HK
    halo_blocks = tm // SUBLANES
    return pl.pallas_call(
        functools.partial(_gdn_prep_kernel, tiles_per_seq=tps),
        grid=(n // tm, 3),
        in_specs=[pl.BlockSpec((tm, GDN_HK), lambda i, s: (i, sec0 + s)),
                  pl.BlockSpec((SUBLANES, GDN_HK), lambda i, s: (jnp.maximum(i * halo_blocks - 1, 0), sec0 + s)),
                  pl.BlockSpec((tm, LANES), lambda i, s: (i, SEG_MISC // LANES)),
                  pl.BlockSpec((CONV_W, GDN_HK), lambda i, s: (0, s)),
                  pl.BlockSpec((1, LANES), lambda i, s: (0, 0)),
                  pl.BlockSpec((1, LANES), lambda i, s: (0, 0))],
        out_specs=[pl.BlockSpec((tm, GDN_HK), lambda i, s: (i, s)),
                   pl.BlockSpec((None, GDN_HK, tm), lambda i, s: (i // tps, 0, i % tps)),
                   pl.BlockSpec((tm, LANES), lambda i, s: (i, 0))],
        out_shape=[jax.ShapeDtypeStruct((n, 3 * GDN_HK), F32),
                   jax.ShapeDtypeStruct((batch, GDN_HK, seq), F32),
                   jax.ShapeDtypeStruct((n, LANES), F32)],
        scratch_shapes=[pltpu.VMEM((tm + SUBLANES, GDN_HK), F32)],
        compiler_params=_cparams(("arbitrary", "arbitrary")),
        name="gdn_prep",
    )(proj, proj, proj, conv_w, alog_l, dtb_l)


def _split3(x):
    hi = x.astype(BF16)
    r1 = x - hi.astype(F32)
    mid = r1.astype(BF16)
    lo = (r1 - mid.astype(F32)).astype(BF16)
    return hi, mid, lo


def _bdot(a, b):
    return jnp.dot(a.astype(BF16), b.astype(BF16), preferred_element_type=F32)


def _delta_kernel(q_ref, k_ref, v_ref, kt_ref, gates_ref, o_ref, s_out_ref, s_ref):
    c = GDN_CHUNK
    n = pl.program_id(1)
    heads = range(GDN_HEADS)

    @pl.when(n == 0)
    def _():
        s_ref[...] = jnp.zeros(s_ref.shape, F32)

    row = lax.broadcasted_iota(jnp.int32, (c, c), 0)
    col = lax.broadcasted_iota(jnp.int32, (c, c), 1)
    incl = row >= col
    strict = row > col
    eye = (row == col).astype(F32)
    tril = incl.astype(BF16)
    triu = (row <= col).astype(BF16)
    gates = gates_ref[...]
    gates_t = gates.T
    gc_all = sum(jnp.dot(tril, part, preferred_element_type=F32) for part in _split3(gates))
    gct_all = sum(jnp.dot(part, triu, preferred_element_type=F32) for part in _split3(gates_t))

    sl = [slice(h * GDN_DK, (h + 1) * GDN_DK) for h in heads]
    gcol = [gc_all[:, MISC_A + h:MISC_A + h + 1] for h in heads]
    grow = [gct_all[MISC_A + h:MISC_A + h + 1, :] for h in heads]
    glast = [gct_all[MISC_A + h:MISC_A + h + 1, c - 1:c] for h in heads]
    bcol = [gates[:, MISC_B + h:MISC_B + h + 1] for h in heads]
    eg = [jnp.exp(g) for g in gcol]
    e = [jnp.exp(jnp.where(incl, gcol[h] - grow[h], 0.0)) for h in heads]
    kb = [k_ref[:, sl[h]] * bcol[h] for h in heads]
    ktb = [kt_ref[sl[h], :].astype(BF16) for h in heads]
    x = [-(jnp.dot(kb[h].astype(BF16), ktb[h], preferred_element_type=F32) * jnp.where(strict, e[h], 0.0))
         for h in heads]
    t = [eye + x[h] for h in heads]
    for _ in range(6):
        x = [_bdot(x[h], x[h]) for h in heads]
        t = [t[h] + _bdot(t[h], x[h]) for h in heads]
    rhs = [jnp.concatenate([v_ref[:, sl[h]] * bcol[h], kb[h] * eg[h]], axis=1) for h in heads]
    sol = [_bdot(t[h], rhs[h]) for h in heads]
    attn = [jnp.dot(q_ref[:, sl[h]].astype(BF16), ktb[h], preferred_element_type=F32) * jnp.where(incl, e[h], 0.0)
            for h in heads]
    s_old = [s_ref[h] for h in heads]
    lhs = [jnp.concatenate([sol[h][:, GDN_DV:], q_ref[:, sl[h]] * eg[h]], axis=0) for h in heads]
    prod = [_bdot(lhs[h], s_old[h]) for h in heads]
    v_new = [sol[h][:, :GDN_DV] - prod[h][:c] for h in heads]
    for h in heads:
        o_ref[:, sl[h]] = prod[h][c:] + _bdot(attn[h], v_new[h])
    for h in heads:
        kdt = kt_ref[sl[h], :] * jnp.exp(glast[h] - grow[h])
        s_ref[h] = s_old[h] * jnp.exp(glast[h]) + _bdot(kdt, v_new[h])

    @pl.when(n == pl.num_programs(1) - 1)
    def _():
        s_out_ref[...] = s_ref[...]


def _delta_rule(qkv, kt, gates, *, batch, seq):
    c = GDN_CHUNK
    assert seq % c == 0
    nc = seq // c
    sec = lambda s: pl.BlockSpec((c, GDN_HK), lambda b, n, s=s: (b * nc + n, s))
    return pl.pallas_call(
        _delta_kernel,
        grid=(batch, nc),
        in_specs=[sec(0), sec(1), sec(2),
                  pl.BlockSpec((None, GDN_HK, c), lambda b, n: (b, 0, n)),
                  pl.BlockSpec((c, LANES), lambda b, n: (b * nc + n, 0))],
        out_specs=[pl.BlockSpec((c, GDN_HK), lambda b, n: (b * nc + n, 0)),
                   pl.BlockSpec((None, GDN_HEADS, GDN_DK, GDN_DV), lambda b, n: (b, 0, 0, 0))],
        out_shape=[jax.ShapeDtypeStruct((batch * seq, GDN_HK), F32),
                   jax.ShapeDtypeStruct((batch, GDN_HEADS, GDN_DK, GDN_DV), F32)],
        scratch_shapes=[pltpu.VMEM((GDN_HEADS, GDN_DK, GDN_DV), F32)],
        compiler_params=_cparams(("arbitrary", "arbitrary")),
        name="gdn_delta_rule",
    )(qkv, qkv, qkv, kt, gates)


GDN_STEP_ROWS = SUBLANES


def _gdn_step_kernel(x_ref, buf_ref, misc_ref, cw_ref, alog_ref, dtb_ref, s_ref, o_ref, s_out_ref):
    nb = x_ref.shape[0]
    acc = x_ref[...] * cw_ref[CONV_W - 1:CONV_W, :]
    for w in range(CONV_W - 1):
        acc = acc + buf_ref[:, w, :] * cw_ref[w:w + 1, :]
    y = acc * _sigmoid(acc)
    x = misc_ref[...] + dtb_ref[...]
    softplus = jnp.maximum(x, 0.0) + jnp.log(1.0 + jnp.exp(-jnp.abs(x)))
    eg_all = jnp.exp(-jnp.exp(alog_ref[...]) * softplus)
    beta_all = _sigmoid(misc_ref[...])
    for h in range(GDN_HEADS):
        qs = slice(h * GDN_DK, (h + 1) * GDN_DK)
        ks = slice(GDN_HK + h * GDN_DK, GDN_HK + (h + 1) * GDN_DK)
        vs = slice(2 * GDN_HK + h * GDN_DV, 2 * GDN_HK + (h + 1) * GDN_DV)
        q = y[:, qs]
        k = y[:, ks]
        q = q * (lax.rsqrt(jnp.sum(q * q, axis=-1, keepdims=True) + EPS) * (GDN_DK ** -0.5))
        k = k * lax.rsqrt(jnp.sum(k * k, axis=-1, keepdims=True) + EPS)
        eg = eg_all[:, MISC_A + h:MISC_A + h + 1]
        beta = beta_all[:, MISC_B + h:MISC_B + h + 1]
        vb = y[:, vs] * beta
        kcd = k * (beta * eg)
        qd = q * eg
        qk = jnp.sum(q * k, axis=-1, keepdims=True)
        kt = k.T
        for b in range(nb):
            s_old = s_ref[b, h]
            lhs = jnp.concatenate([kcd[b:b + 1], qd[b:b + 1], jnp.zeros((SUBLANES - 2, GDN_DK), F32)], axis=0)
            prod = jnp.dot(lhs.astype(BF16), s_old.astype(BF16), preferred_element_type=F32)
            v_new = vb[b:b + 1] - prod[0:1]
            o_ref[b:b + 1, qs] = prod[1:2] + qk[b:b + 1] * v_new
            s_out_ref[b, h] = s_old * eg[b:b + 1] + kt[:, b:b + 1] * v_new


def _gdn_step(proj, state_conv, state_ssm, conv_w, alog_l, dtb_l):
    n_s = proj.shape[0]
    nb = min(GDN_STEP_ROWS, n_s)
    assert n_s % nb == 0
    return pl.pallas_call(
        _gdn_step_kernel,
        grid=(n_s // nb,),
        in_specs=[pl.BlockSpec((nb, GDN_QKV_W), lambda i: (i, SEG_QKV // GDN_QKV_W)),
                  pl.BlockSpec((nb, CONV_W - 1, GDN_QKV_W), lambda i: (i, 0, 0)),
                  pl.BlockSpec((nb, LANES), lambda i: (i, SEG_MISC // LANES)),
                  pl.BlockSpec((CONV_W, GDN_QKV_W), lambda i: (0, 0)),
                  pl.BlockSpec((1, LANES), lambda i: (0, 0)),
                  pl.BlockSpec((1, LANES), lambda i: (0, 0)),
                  pl.BlockSpec((nb, GDN_HEADS, GDN_DK, GDN_DV), lambda i: (i, 0, 0, 0))],
        out_specs=[pl.BlockSpec((nb, GDN_HK), lambda i: (i, 0)),
                   pl.BlockSpec((nb, GDN_HEADS, GDN_DK, GDN_DV), lambda i: (i, 0, 0, 0))],
        out_shape=[jax.ShapeDtypeStruct((n_s, GDN_HK), F32),
                   jax.ShapeDtypeStruct(state_ssm.shape, F32)],
        compiler_params=_cparams(("arbitrary",)),
        name="gdn_step",
    )(proj, state_conv, proj, conv_w, alog_l, dtb_l, state_ssm)


def _merge_kernel(og_ref, z_ref, oa_ref, ga_ref, gb_ref, gn_ref, wpa_ref, wpb_ref, m_ref, ob_ref):
    @pl.when(pl.program_id(1) == 0)
    def _():
        gn = gn_ref[...]
        for h in range(GDN_HEADS):
            sl = slice(h * GDN_DV, (h + 1) * GDN_DV)
            z = z_ref[:, sl]
            ob_ref[:, sl] = (_rms_rows(og_ref[:, sl], gn) * (z * _sigmoid(z))).astype(BF16)

    pa = jnp.dot(oa_ref[...], wpa_ref[...], preferred_element_type=F32)
    pb = jnp.dot(ob_ref[...], wpb_ref[...], preferred_element_type=F32)
    m_ref[...] = (_sigmoid(ga_ref[...]) * pa + _sigmoid(gb_ref[...]) * pb).astype(BF16)


def _merge(o_g, oa, proj, gdn_norm, w_pa_b, w_pb_b, *, tm):
    n = o_g.shape[0]
    tm = _tile(n, tm, 16)
    tn = 1024
    nj = D_MODEL // tn
    return pl.pallas_call(
        _merge_kernel,
        grid=(n // tm, nj),
        in_specs=[pl.BlockSpec((tm, GDN_HK), lambda i, j: (i, 0)),
                  pl.BlockSpec((tm, GDN_HK), lambda i, j: (i, SEG_Z // GDN_HK)),
                  pl.BlockSpec((tm, MLA_HEADS * V_HEAD), lambda i, j: (i, 0)),
                  pl.BlockSpec((tm, tn), lambda i, j: (i, SEG_GA // tn + j)),
                  pl.BlockSpec((tm, tn), lambda i, j: (i, SEG_GB // tn + j)),
                  pl.BlockSpec((1, GDN_DV), lambda i, j: (0, 0)),
                  pl.BlockSpec((MLA_HEADS * V_HEAD, tn), lambda i, j: (0, j)),
                  pl.BlockSpec((GDN_HK, tn), lambda i, j: (0, j))],
        out_specs=pl.BlockSpec((tm, tn), lambda i, j: (i, j)),
        out_shape=jax.ShapeDtypeStruct((n, D_MODEL), BF16),
        scratch_shapes=[pltpu.VMEM((tm, GDN_HK), BF16)],
        compiler_params=_cparams(("arbitrary", "arbitrary")),
        name="gated_merge",
    )(o_g, proj, oa, proj, proj, gdn_norm, w_pa_b, w_pb_b)


ROUTE_E1, ROUTE_E2, ROUTE_W1, ROUTE_W2 = 0, 1, 2, 3


def _first_lane_of_max(x, lane):
    m = jnp.max(x, axis=-1, keepdims=True)
    return m, jnp.min(jnp.where(x == m, lane, LANES), axis=-1, keepdims=True)


def _route(logits, lane):
    neg = -jnp.inf
    is_g = lane < N_GROUPS
    gmax, grp = _first_lane_of_max(jnp.where(is_g, logits, neg), lane)
    p_grp = 1.0 / jnp.sum(jnp.where(is_g, jnp.exp(logits - gmax), 0.0), axis=-1, keepdims=True)
    lo = N_GROUPS + grp * EXP_PER_GROUP
    in_grp = (lane >= lo) & (lane < lo + EXP_PER_GROUP)
    emax = jnp.max(jnp.where(in_grp, logits, neg), axis=-1, keepdims=True)
    ee = jnp.where(in_grp, jnp.exp(logits - emax), 0.0)
    soft = jnp.where(in_grp, ee / jnp.sum(ee, axis=-1, keepdims=True), -1.0)
    w1, i1 = _first_lane_of_max(soft, lane)
    w2, i2 = _first_lane_of_max(jnp.where(lane == i1, -1.0, soft), lane)
    wsum = w1 + w2
    tile = jnp.where(lane == ROUTE_E1, (i1 - N_GROUPS).astype(F32), 0.0)
    tile = jnp.where(lane == ROUTE_E2, (i2 - N_GROUPS).astype(F32), tile)
    tile = jnp.where(lane == ROUTE_W1, w1 / wsum * p_grp, tile)
    return jnp.where(lane == ROUTE_W2, w2 / wsum * p_grp, tile)


def _out_proj_kernel(mp_ref, xp_ref, ms_ref, xs_ref, wo_ref, gf_ref, wrt_ref, brt_ref, h_ref, hn_ref, rt_ref,
                     *, np_tiles):
    i = pl.program_id(0)

    def rows(m_ref, x_ref, n):
        h = x_ref[...] + jnp.dot(m_ref[...], wo_ref[...], preferred_element_type=F32)
        h_ref[0:n, :] = h
        hn = _rms_rows(h, gf_ref[...])
        hn_ref[0:n, :] = hn
        logits = jnp.dot(hn.astype(BF16), wrt_ref[...], preferred_element_type=F32) + brt_ref[...]
        rt_ref[0:n, :] = _route(logits, lax.broadcasted_iota(jnp.int32, logits.shape, 1))

    @pl.when(i < np_tiles)
    def _():
        rows(mp_ref, xp_ref, mp_ref.shape[0])

    @pl.when(i == np_tiles)
    def _():
        rows(ms_ref, xs_ref, ms_ref.shape[0])


def _out_proj(m_p, x_p, m_s, x_s, w_o_b, norm_ffn, w_rt_b, b_rt, *, tm):
    n_p, n_s = m_p.shape[0], m_s.shape[0]
    tm = _tile(n_p, tm, 16)
    np_tiles = n_p // tm
    assert n_s <= tm
    const = lambda a: pl.BlockSpec(a.shape, lambda i: (0,) * a.ndim)
    p_spec = pl.BlockSpec((tm, D_MODEL), lambda i: (jnp.minimum(i, np_tiles - 1), 0))
    out = lambda w: pl.BlockSpec((tm, w), lambda i: (i, 0))
    return pl.pallas_call(
        functools.partial(_out_proj_kernel, np_tiles=np_tiles),
        grid=(np_tiles + 1,),
        in_specs=[p_spec, p_spec, const(m_s), const(x_s), const(w_o_b), const(norm_ffn), const(w_rt_b),
                  const(b_rt)],
        out_specs=[out(D_MODEL), out(D_MODEL), out(LANES)],
        out_shape=[jax.ShapeDtypeStruct((n_p + n_s, D_MODEL), F32),
                   jax.ShapeDtypeStruct((n_p + n_s, D_MODEL), F32),
                   jax.ShapeDtypeStruct((n_p + n_s, LANES), F32)],
        compiler_params=_cparams(("arbitrary",)),
        name="out_proj_router",
    )(m_p, x_p, m_s, x_s, w_o_b, norm_ffn, w_rt_b, b_rt)


def _expert_weights(sched_ref, w_hbm, wbuf, wb_ref, sem):
    i = pl.program_id(0)

    def fetch(expert, slot):
        return pltpu.make_async_copy(w_hbm.at[expert], wbuf.at[slot], sem.at[slot])

    @pl.when(i == 0)
    def _():
        fetch(sched_ref[0, 0], 0).start()

    @pl.when(sched_ref[2, i] != 0)
    def _():
        slot = sched_ref[3, i]
        fetch(sched_ref[0, i], slot).wait()

        @pl.when(sched_ref[4, i] >= 0)
        def _():
            fetch(sched_ref[4, i], 1 - slot).start()

        wb_ref[...] = wbuf[slot].astype(BF16)


def _gmm_kernel(sched_ref, x_ref, w_hbm, o_ref, wbuf, wb_ref, sem):
    i = pl.program_id(0)
    _expert_weights(sched_ref, w_hbm, wbuf, wb_ref, sem)

    @pl.when(sched_ref[1, i] != 0)
    def _():
        o_ref[...] = jnp.dot(x_ref[...].astype(BF16), wb_ref[...], preferred_element_type=F32)

    @pl.when(sched_ref[1, i] == 0)
    def _():
        o_ref[...] = jnp.zeros(o_ref.shape, o_ref.dtype)


def _gmm_down_kernel(sched_ref, g_ref, u_ref, w_hbm, acc_hbm, o_ref, wbuf, wb_ref, sem):
    del acc_hbm
    i = pl.program_id(0)
    _expert_weights(sched_ref, w_hbm, wbuf, wb_ref, sem)

    @pl.when(sched_ref[1, i] != 0)
    def _():
        gate = g_ref[...]
        act = gate * _sigmoid(gate) * u_ref[...]
        o_ref[...] = jnp.dot(act.astype(BF16), wb_ref[...], preferred_element_type=F32)

    @pl.when(sched_ref[1, i] == 0)
    def _():
        o_ref[...] = jnp.zeros(o_ref.shape, o_ref.dtype)


def _expert_schedule(bexp, used):
    nb = bexp.shape[0]
    first = jnp.concatenate([jnp.ones((1,), jnp.int32), (bexp[1:] != bexp[:-1]).astype(jnp.int32)])
    run = jnp.cumsum(first) - 1
    run_expert = jnp.zeros((nb,), jnp.int32).at[run].set(bexp)
    nxt = jnp.where(run + 1 <= run[-1], run_expert[jnp.minimum(run + 1, nb - 1)], -1)
    return jnp.stack([bexp, used, first, run % 2, nxt]).astype(jnp.int32)


def _grouped_call(kern, sched, row_inputs, w, *, name, out_into=None, block_offset=0):
    rows = row_inputs[0].shape[0]
    k, n = w.shape[1], w.shape[2]
    nb = rows // MOE_ROWS
    in_specs = ([pl.BlockSpec((MOE_ROWS, a.shape[1]), lambda i, sc: (i, 0)) for a in row_inputs]
                + [pl.BlockSpec(memory_space=pl.ANY)])
    args = [sched, *row_inputs, w]
    aliases = {}
    if out_into is not None:
        in_specs.append(pl.BlockSpec(memory_space=pl.ANY))
        aliases = {len(args): 0}
        args.append(out_into)
        rows = out_into.shape[0]
    grid_spec = pltpu.PrefetchScalarGridSpec(
        num_scalar_prefetch=1,
        grid=(nb,),
        in_specs=in_specs,
        out_specs=pl.BlockSpec((MOE_ROWS, n), lambda i, sc: (i + block_offset, 0)),
        scratch_shapes=[pltpu.VMEM((2, k, n), F32), pltpu.VMEM((k, n), BF16), pltpu.SemaphoreType.DMA((2,))])
    return pl.pallas_call(
        kern, grid_spec=grid_spec,
        out_shape=jax.ShapeDtypeStruct((rows, n), F32),
        input_output_aliases=aliases,
        compiler_params=_cparams(("arbitrary",)),
        name=name,
    )(*args)


def _l2n(x):
    return x * lax.rsqrt(jnp.sum(x * x, axis=-1, keepdims=True) + EPS)


def _rms(x, g):
    return x * lax.rsqrt(jnp.mean(x * x, axis=-1, keepdims=True) + EPS) * g


def _rope_tables(pos):
    inv = ROPE_THETA ** (-jnp.arange(0, QK_ROPE, 2, dtype=F32) / QK_ROPE)
    ang = pos[:, None] * inv[None, :]
    cos, sin = jnp.cos(ang), jnp.sin(ang)
    return jnp.concatenate([cos, cos, cos, cos], axis=1), jnp.concatenate([-sin, sin, -sin, sin], axis=1)


def _w_in_segments():
    segs = []
    for h in range(MLA_HEADS):
        segs.append((h * QK_HEAD, QK_NOPE, SEG_QN + h * QK_NOPE))
        segs.append((h * QK_HEAD + QK_NOPE, QK_ROPE, SEG_QR + h * QK_ROPE))
    o = MLA_HEADS * QK_HEAD
    segs.append((o, KV_LORA, SEG_CKV)); o += KV_LORA
    segs.append((o, QK_ROPE, SEG_MISC)); o += QK_ROPE
    segs.append((o, GDN_QKV_W, SEG_QKV)); o += GDN_QKV_W
    segs.append((o, GDN_HEADS, SEG_MISC + MISC_A)); o += GDN_HEADS
    segs.append((o, GDN_HEADS, SEG_MISC + MISC_B)); o += GDN_HEADS
    segs.append((o, GDN_HEADS * GDN_DV, SEG_Z)); o += GDN_HEADS * GDN_DV
    segs.append((o, D_MODEL, SEG_GA)); o += D_MODEL
    segs.append((o, D_MODEL, SEG_GB))
    return segs


def _reorder_kernel(w_ref, o_ref):
    for src, width, dst in _w_in_segments():
        o_ref[:, dst:dst + width] = w_ref[:, src:src + width].astype(BF16)
    tail = SEG_MISC + MISC_B + GDN_HEADS
    o_ref[:, tail:D_PROJ] = jnp.zeros((o_ref.shape[0], D_PROJ - tail), BF16)


def _reorder_w_in_pallas(w_in):
    d, d_in = w_in.shape
    tr = 128
    return pl.pallas_call(
        _reorder_kernel,
        grid=(d // tr,),
        in_specs=[pl.BlockSpec((tr, d_in), lambda i: (i, 0))],
        out_specs=pl.BlockSpec((tr, D_PROJ), lambda i: (i, 0)),
        out_shape=jax.ShapeDtypeStruct((d, D_PROJ), BF16),
        compiler_params=_cparams(("arbitrary",)),
        name="w_in_reorder",
    )(w_in)


def _reorder_w_in(w_in):
    o = 0
    wq = w_in[:, o:o + MLA_HEADS * QK_HEAD].reshape(D_MODEL, MLA_HEADS, QK_HEAD); o += MLA_HEADS * QK_HEAD
    wckv = w_in[:, o:o + KV_LORA]; o += KV_LORA
    wkr = w_in[:, o:o + QK_ROPE]; o += QK_ROPE
    wqkv = w_in[:, o:o + GDN_QKV_W]; o += GDN_QKV_W
    wa = w_in[:, o:o + GDN_HEADS]; o += GDN_HEADS
    wb = w_in[:, o:o + GDN_HEADS]; o += GDN_HEADS
    wz = w_in[:, o:o + GDN_HEADS * GDN_DV]; o += GDN_HEADS * GDN_DV
    wga = w_in[:, o:o + D_MODEL]; o += D_MODEL
    wgb = w_in[:, o:o + D_MODEL]
    pad = jnp.zeros((D_MODEL, D_PROJ - SEG_MISC - QK_ROPE - 2 * GDN_HEADS), w_in.dtype)
    return jnp.concatenate([
        wz, wga, wgb, wqkv, wq[:, :, :QK_NOPE].reshape(D_MODEL, -1), wq[:, :, QK_NOPE:].reshape(D_MODEL, -1),
        wckv, wkr, wa, wb, pad], axis=1).astype(BF16)


def _lane_vec(v, offset):
    return jnp.zeros((1, LANES), F32).at[0, offset:offset + v.shape[0]].set(v.astype(F32))


def _hier_moe(h, hn, route, w_gate, w_up, w_down):
    t = h.shape[0]
    expert = route[:, ROUTE_E1:ROUTE_E2 + 1].astype(jnp.int32)
    top_w = route[:, ROUTE_W1:ROUTE_W2 + 1]
    n_assign = t * TOP_K
    e_flat = expert.reshape(n_assign)
    onehot = (e_flat[:, None] == jnp.arange(N_EXPERTS, dtype=jnp.int32)[None, :]).astype(jnp.int32)
    csum = jnp.cumsum(onehot, axis=0)
    rank = jnp.take_along_axis(csum, e_flat[:, None], axis=1)[:, 0] - 1
    counts = csum[-1]
    pcounts = (counts + MOE_ROWS - 1) // MOE_ROWS * MOE_ROWS
    pends = jnp.cumsum(pcounts)
    pstarts = pends - pcounts
    dest = pstarts[e_flat] + rank
    n_blocks = -(-(-(-n_assign // MOE_ROWS) + N_EXPERTS) // MOE_SPLITS) * MOE_SPLITS
    n_rows = n_blocks * MOE_ROWS
    tok = jnp.arange(n_assign, dtype=jnp.int32) // TOP_K
    src = jnp.zeros((n_rows,), jnp.int32).at[dest].set(tok)
    bstart = jnp.arange(n_blocks, dtype=jnp.int32) * MOE_ROWS
    bexp = jnp.minimum(jnp.searchsorted(pends, bstart, side='right'), N_EXPERTS - 1).astype(jnp.int32)
    used = (bstart < pends[-1]).astype(jnp.int32)
    bq = n_blocks // MOE_SPLITS
    yb = jnp.zeros((n_rows, D_MODEL), F32)
    for q in range(MOE_SPLITS):
        xb = hn[src[q * bq * MOE_ROWS:(q + 1) * bq * MOE_ROWS]]
        sched = _expert_schedule(bexp[q * bq:(q + 1) * bq], used[q * bq:(q + 1) * bq])
        gate = _grouped_call(_gmm_kernel, sched, [xb], w_gate, name="moe_gate")
        up = _grouped_call(_gmm_kernel, sched, [xb], w_up, name="moe_up")
        yb = _grouped_call(_gmm_down_kernel, sched, [gate, up], w_down, name="moe_down",
                           out_into=yb, block_offset=q * bq)
    dest2 = dest.reshape(t, TOP_K)
    y = h
    for k in range(TOP_K):
        y = y + yb[dest2[:, k]] * top_w[:, k:k + 1]
    return y


def kernel(x_prompt, x_sample, cache_ckv, cache_krope, state_ssm, state_conv, page_table, norm_attn, w_in, norm_ckv, w_uk, w_uv, q_gain_nope, q_gain_rope, k_gain_nope, k_gain_rope, conv_w, gdn_a_log, gdn_dt_bias, gdn_norm, w_pa, w_pb, w_o, norm_ffn, w_router_group, b_router_group, w_router_expert, b_router_expert, w_gate, w_up, w_down):
    depth = w_in.shape[0]
    bsz, seq, _ = x_prompt.shape
    n_s, s_new, _ = x_sample.shape
    assert s_new == 1
    n_p = bsz * seq
    n_all = n_p + n_s
    n_past = page_table.shape[1] * PAGE_SIZE
    cos_p, sin_p = _rope_tables(jnp.tile(jnp.arange(seq, dtype=F32), bsz))
    cos_s, sin_s = _rope_tables(jnp.full((n_s,), n_past, F32))
    tq = _tile(seq, 512, LANES)
    xp = x_prompt.reshape(n_p, D_MODEL)
    xs = x_sample.reshape(n_s, D_MODEL)
    outs = [[] for _ in range(8)]
    row = lambda v: v.astype(F32)[None, :]
    for l in range(depth):
        w_in_r = _reorder_w_in_pallas(w_in[l])
        w_uk_b = w_uk[l].astype(BF16)
        w_uvt_b = w_uv[l].T.astype(BF16)
        wukt = w_uk_b.T
        w_pa_b, w_pb_b, w_o_b = w_pa[l].astype(BF16), w_pb[l].astype(BF16), w_o[l].astype(BF16)
        w_rt_b = jnp.concatenate([w_router_group[l], w_router_expert[l],
                                  jnp.zeros((D_MODEL, LANES - N_GROUPS - N_EXPERTS), F32)], axis=1).astype(BF16)
        gains = (row(q_gain_nope[l]), jnp.tile(row(q_gain_rope[l]), (1, 2)), row(k_gain_nope[l]),
                 jnp.tile(row(k_gain_rope[l]), (1, 2)), row(norm_ckv[l]))
        alog_l = _lane_vec(gdn_a_log[l], MISC_A)
        dtb_l = _lane_vec(gdn_dt_bias[l], MISC_A)
        proj_p = _in_proj(xp, row(norm_attn[l]), w_in_r, tm=1024)
        proj_s = _in_proj(xs, row(norm_attn[l]), w_in_r, tm=n_s)

        qcat_p, kcat_p, c_p, kr_p, vt_p = _mla_prep(proj_p, cos_p, sin_p, *gains, w_uk_b, w_uvt_b,
                                                    tm=tq, with_vt=True, with_qg=False)
        oa_p = _flash_attention(qcat_p, kcat_p, vt_p, batch=bsz, seq=seq, tq=tq)

        qcat_s, kcat_s, c_s, kr_s, qg_s = _mla_prep(proj_s, cos_s, sin_s, *gains, w_uk_b, w_uvt_b,
                                                    tm=n_s, with_vt=False, with_qg=True)
        qa = _headwise_matmul(qg_s, wukt, heads=MLA_HEADS, b_rows_by_head=True, name="q_absorb")
        o_lat = _paged_attention(page_table, cache_ckv[l], jnp.swapaxes(cache_krope[l], 1, 2), wukt,
                                 qa.astype(BF16).reshape(n_s, MLA_HEADS, KV_LORA),
                                 qcat_s.reshape(n_s, MLA_HEADS, QK_PAD), kcat_s.reshape(n_s, MLA_HEADS, QK_PAD),
                                 c_s[:, None, :])
        oa_s = _headwise_matmul(o_lat.astype(BF16).reshape(n_s, MLA_HEADS * KV_LORA), w_uv[l].astype(BF16),
                                heads=MLA_HEADS, b_rows_by_head=False, name="v_absorb").astype(BF16)

        qkv_p, kt_p, gates_p = _gdn_prep(proj_p, conv_w[l], alog_l, dtb_l, batch=bsz, seq=seq, tm=512)
        o_g, ssm_p = _delta_rule(qkv_p, kt_p, gates_p, batch=bsz, seq=seq)
        conv_p = proj_p.reshape(bsz, seq, D_PROJ)[:, seq - (CONV_W - 1):, SEG_QKV:SEG_QKV + GDN_QKV_W]

        o_s, ssm_s = _gdn_step(proj_s, state_conv[l], state_ssm[l], conv_w[l], alog_l, dtb_l)
        conv_s = jnp.concatenate([state_conv[l][:, 1:], proj_s[:, None, SEG_QKV:SEG_QKV + GDN_QKV_W]], axis=1)

        m_p = _merge(o_g, oa_p, proj_p, row(gdn_norm[l]), w_pa_b, w_pb_b, tm=512)
        m_s = _merge(o_s, oa_s, proj_s, row(gdn_norm[l]), w_pa_b, w_pb_b, tm=n_s)
        b_rt = _lane_vec(jnp.concatenate([b_router_group[l], b_router_expert[l]]), 0)
        h_all, hn_all, route = _out_proj(m_p, xp, m_s, xs, w_o_b, row(norm_ffn[l]), w_rt_b, b_rt, tm=512)
        y_all = _hier_moe(h_all, hn_all, route, w_gate[l], w_up[l], w_down[l])
        xp, xs = y_all[:n_p], y_all[n_p:]

        for lst, val in zip(outs, (c_p.reshape(bsz, seq, KV_LORA), kr_p[:, :QK_ROPE].reshape(bsz, seq, QK_ROPE),
                                   c_s.reshape(n_s, 1, KV_LORA), kr_s[:, :QK_ROPE].reshape(n_s, 1, QK_ROPE),
                                   ssm_p, conv_p, ssm_s, conv_s)):
            lst.append(val)
    return (xp.reshape(bsz, seq, D_MODEL), xs.reshape(n_s, 1, D_MODEL)) + tuple(jnp.stack(o) for o in outs)
```

```python
import functools

import jax
import jax.numpy as jnp
from jax import lax
from jax.experimental import pallas as pl
from jax.experimental.pallas import tpu as pltpu

F32 = jnp.float32
BF16 = jnp.bfloat16

D_MODEL = 2048
PAGE_SIZE = 128
MLA_HEADS = 16
QK_NOPE = 128
QK_ROPE = 64
QK_HEAD = QK_NOPE + QK_ROPE
QK_PAD = 256
V_HEAD = 128
KV_LORA = 512
ROPE_THETA = 10000.0
GDN_HEADS = 16
GDN_DK = 128
GDN_DV = 128
GDN_HK = GDN_HEADS * GDN_DK
GDN_QKV_W = GDN_HEADS * (2 * GDN_DK + GDN_DV)
CONV_W = 4
GDN_CHUNK = 128
N_GROUPS = 8
EXP_PER_GROUP = 8
N_EXPERTS = N_GROUPS * EXP_PER_GROUP
TOP_K = 2
D_EXPERT = 1408
MOE_ROWS = 128
MOE_SPLITS = 4
EPS = 1e-6
LANES = 128
SUBLANES = 8

V7X_VMEM_LIMIT = 56 * 1024 * 1024

SEG_Z = 0
SEG_GA = SEG_Z + GDN_HEADS * GDN_DV
SEG_GB = SEG_GA + D_MODEL
SEG_QKV = SEG_GB + D_MODEL
SEG_QN = SEG_QKV + GDN_QKV_W
SEG_QR = SEG_QN + MLA_HEADS * QK_NOPE
SEG_CKV = SEG_QR + MLA_HEADS * QK_ROPE
SEG_MISC = SEG_CKV + KV_LORA
PROJ_TN = 1024
D_PROJ = -(-(SEG_MISC + LANES) // PROJ_TN) * PROJ_TN
MISC_A = QK_ROPE
MISC_B = QK_ROPE + GDN_HEADS


def _cparams(sem, vmem=V7X_VMEM_LIMIT):
    return pltpu.CompilerParams(dimension_semantics=sem, vmem_limit_bytes=vmem)


def _tile(dim, cap, align):
    if dim <= cap:
        return dim
    for t in range(cap - cap % align, 0, -align):
        if dim % t == 0:
            return t
    raise ValueError(f"no {align}-aligned tile <= {cap} divides {dim}")


def _nt_dot(a, b):
    return lax.dot_general(a, b, (((1,), (1,)), ((), ())), preferred_element_type=F32)


def _rms_rows(x, gain):
    return x * lax.rsqrt(jnp.mean(x * x, axis=-1, keepdims=True) + EPS) * gain


def _sigmoid(x):
    return 1.0 / (1.0 + jnp.exp(-x))


def _in_proj_kernel(x_ref, g_ref, w_ref, o_ref, u_ref):
    @pl.when(pl.program_id(1) == 0)
    def _():
        u_ref[...] = _rms_rows(x_ref[...], g_ref[...]).astype(BF16)

    o_ref[...] = jnp.dot(u_ref[...], w_ref[...], preferred_element_type=F32)


def _in_proj(x, gain, w, *, tm):
    m, d = x.shape
    n = w.shape[1]
    tm = _tile(m, tm, 16)
    tn = PROJ_TN
    assert n % tn == 0
    return pl.pallas_call(
        _in_proj_kernel,
        grid=(m // tm, n // tn),
        in_specs=[pl.BlockSpec((tm, d), lambda i, j: (i, 0)),
                  pl.BlockSpec((1, d), lambda i, j: (0, 0)),
                  pl.BlockSpec((d, tn), lambda i, j: (0, j))],
        out_specs=pl.BlockSpec((tm, tn), lambda i, j: (i, j)),
        out_shape=jax.ShapeDtypeStruct((m, n), F32),
        scratch_shapes=[pltpu.VMEM((tm, d), BF16)],
        compiler_params=_cparams(("arbitrary", "arbitrary")),
        name="in_proj",
    )(x, gain, w)


def _mm_kernel(a_ref, b_ref, o_ref):
    o_ref[...] = jnp.dot(a_ref[...], b_ref[...], preferred_element_type=F32).astype(o_ref.dtype)


def _headwise_matmul(a, b, *, heads, b_rows_by_head, name):
    m = a.shape[0]
    k = a.shape[1] // heads
    if b_rows_by_head:
        n = b.shape[1]
        b_spec = pl.BlockSpec((k, n), lambda h: (h, 0))
    else:
        n = b.shape[1] // heads
        b_spec = pl.BlockSpec((k, n), lambda h: (0, h))
    return pl.pallas_call(
        _mm_kernel,
        grid=(heads,),
        in_specs=[pl.BlockSpec((m, k), lambda h: (0, h)), b_spec],
        out_specs=pl.BlockSpec((m, n), lambda h: (0, h)),
        out_shape=jax.ShapeDtypeStruct((m, heads * n), F32),
        compiler_params=_cparams(("arbitrary",)),
        name=name,
    )(a, b)


def _swap_halves(y, lane):
    half = QK_ROPE // 2
    return jnp.where(lane % QK_ROPE < half, pltpu.roll(y, LANES - half, axis=1), pltpu.roll(y, half, axis=1))


def _rope_pair(x, gain2, cos4, sin4, lane):
    lo = lane < QK_ROPE
    ss = x * x
    s_lo = jnp.sum(jnp.where(lo, ss, 0.0), axis=-1, keepdims=True)
    s_hi = jnp.sum(jnp.where(lo, 0.0, ss), axis=-1, keepdims=True)
    r = jnp.where(lo, lax.rsqrt(s_lo * (1.0 / QK_ROPE) + EPS), lax.rsqrt(s_hi * (1.0 / QK_ROPE) + EPS))
    y = x * r * gain2
    return y * cos4 + _swap_halves(y, lane) * sin4


def _mla_prep_kernel(qn_ref, qr_ref, ckv_ref, misc_ref, cos_ref, sin_ref, gqn_ref, gqr_ref, gkn_ref, gkr_ref,
                     gc_ref, wuk_ref, *rest, with_vt, with_qg):
    rest = list(rest)
    wuvt_ref = rest.pop(0) if with_vt else None
    qcat_ref, kcat_ref, c_ref, kr_ref = rest[:4]
    rest = rest[4:]
    vt_ref = rest.pop(0) if with_vt else None
    qg_ref = rest.pop(0) if with_qg else None
    rows = qn_ref.shape[0]
    lane = lax.broadcasted_iota(jnp.int32, (rows, LANES), 1)
    lo = lane < QK_ROPE
    cos4 = cos_ref[...]
    sin4 = sin_ref[...]

    c = _rms_rows(ckv_ref[...], gc_ref[...])
    c_ref[...] = c
    cb = c.astype(BF16)
    kr = jnp.where(lo, _rope_pair(misc_ref[...], gkr_ref[...], cos4, sin4, lane), 0.0)
    kr_ref[...] = kr
    krb = kr.astype(BF16)

    gqn = gqn_ref[...]
    gkn = gkn_ref[...]
    for h in range(MLA_HEADS):
        qn = _rms_rows(qn_ref[:, h * QK_NOPE:(h + 1) * QK_NOPE], gqn)
        qcat_ref[:, h * QK_PAD:h * QK_PAD + QK_NOPE] = qn.astype(BF16)
        if with_qg:
            qg_ref[:, h * QK_NOPE:(h + 1) * QK_NOPE] = (qn * gkn).astype(BF16)
    gqr = gqr_ref[...]
    for j in range(MLA_HEADS // 2):
        rot = _rope_pair(qr_ref[:, j * LANES:(j + 1) * LANES], gqr, cos4, sin4, lane)
        even = jnp.where(lo, rot, 0.0)
        odd = jnp.where(lo, pltpu.roll(rot, QK_ROPE, axis=1), 0.0)
        qcat_ref[:, (2 * j) * QK_PAD + QK_NOPE:(2 * j + 1) * QK_PAD] = even.astype(BF16)
        qcat_ref[:, (2 * j + 1) * QK_PAD + QK_NOPE:(2 * j + 2) * QK_PAD] = odd.astype(BF16)

    kn = jnp.dot(cb, wuk_ref[...], preferred_element_type=F32)
    for h in range(MLA_HEADS):
        kh = _rms_rows(kn[:, h * QK_NOPE:(h + 1) * QK_NOPE], gkn)
        kcat_ref[:, h * QK_PAD:h * QK_PAD + QK_NOPE] = kh.astype(BF16)
        kcat_ref[:, h * QK_PAD + QK_NOPE:(h + 1) * QK_PAD] = krb
    if with_vt:
        vt_ref[...] = _nt_dot(wuvt_ref[...], cb).astype(BF16)


def _mla_prep(proj, cos4, sin4, gqn, gqr2, gkn, gkr2, gckv, w_uk_b, w_uvt_b, *, tm, with_vt, with_qg):
    n = proj.shape[0]
    tm = _tile(n, tm, 16)
    nt = n // tm
    hq = MLA_HEADS * QK_NOPE
    row = lambda w, off: pl.BlockSpec((tm, w), lambda i, off=off, w=w: (i, off // w))
    const = lambda a: pl.BlockSpec(a.shape, lambda i: (0,) * a.ndim)
    in_specs = [row(hq, SEG_QN), row(MLA_HEADS * QK_ROPE, SEG_QR), row(KV_LORA, SEG_CKV), row(LANES, SEG_MISC),
                pl.BlockSpec((tm, LANES), lambda i: (i, 0)), pl.BlockSpec((tm, LANES), lambda i: (i, 0)),
                const(gqn), const(gqr2), const(gkn), const(gkr2), const(gckv), const(w_uk_b)]
    args = [proj, proj, proj, proj, cos4, sin4, gqn, gqr2, gkn, gkr2, gckv, w_uk_b]
    if with_vt:
        in_specs.append(const(w_uvt_b))
        args.append(w_uvt_b)
    out_specs = [pl.BlockSpec((tm, MLA_HEADS * QK_PAD), lambda i: (i, 0)),
                 pl.BlockSpec((tm, MLA_HEADS * QK_PAD), lambda i: (i, 0)),
                 pl.BlockSpec((tm, KV_LORA), lambda i: (i, 0)),
                 pl.BlockSpec((tm, LANES), lambda i: (i, 0))]
    out_shape = [jax.ShapeDtypeStruct((n, MLA_HEADS * QK_PAD), BF16),
                 jax.ShapeDtypeStruct((n, MLA_HEADS * QK_PAD), BF16),
                 jax.ShapeDtypeStruct((n, KV_LORA), F32),
                 jax.ShapeDtypeStruct((n, LANES), F32)]
    if with_vt:
        out_specs.append(pl.BlockSpec((None, MLA_HEADS * V_HEAD, tm), lambda i: (i, 0, 0)))
        out_shape.append(jax.ShapeDtypeStruct((nt, MLA_HEADS * V_HEAD, tm), BF16))
    if with_qg:
        out_specs.append(pl.BlockSpec((tm, hq), lambda i: (i, 0)))
        out_shape.append(jax.ShapeDtypeStruct((n, hq), BF16))
    return pl.pallas_call(
        functools.partial(_mla_prep_kernel, with_vt=with_vt, with_qg=with_qg),
        grid=(nt,), in_specs=in_specs, out_specs=out_specs, out_shape=out_shape,
        compiler_params=_cparams(("arbitrary",)),
        name="mla_prep",
    )(*args)


def _flash_kernel(q_ref, k_ref, vt_ref, o_ref, m_ref, l_ref, acc_ref, *, tq, scale):
    qi = pl.program_id(2)
    q = q_ref[...]
    m_ref[...] = jnp.full(m_ref.shape, -jnp.inf, F32)
    l_ref[...] = jnp.zeros(l_ref.shape, F32)
    acc_ref[...] = jnp.zeros(acc_ref.shape, F32)

    c2 = scale * 1.4426950408889634

    def block(ki, nblk, masked):
        start = pl.multiple_of(ki * tq, tq)
        k = k_ref[pl.ds(start, nblk * tq), :]
        s = _nt_dot(k, q)
        if masked:
            key = lax.broadcasted_iota(jnp.int32, s.shape, 0)
            qry = lax.broadcasted_iota(jnp.int32, s.shape, 1)
            s = jnp.where(key <= qry, s, -jnp.inf)
        m_old = m_ref[...]
        m_new = jnp.maximum(m_old, jnp.max(s, axis=0, keepdims=True))
        alpha = jnp.exp2((m_old - m_new) * c2)
        p32 = jnp.exp2((s - m_new) * c2)
        l_ref[...] = l_ref[...] * alpha + jnp.sum(p32, axis=0, keepdims=True)
        p = p32.astype(BF16)
        pv = jnp.dot(vt_ref[ki], p[0:tq], preferred_element_type=F32)
        for j in range(1, nblk):
            pv = pv + jnp.dot(vt_ref[ki + j], p[j * tq:(j + 1) * tq], preferred_element_type=F32)
        acc_ref[...] = acc_ref[...] * alpha + pv
        m_ref[...] = m_new

    def body(kp, carry):
        block(2 * kp, 2, False)
        return carry

    lax.fori_loop(0, qi // 2, body, 0)

    @pl.when(qi % 2 == 1)
    def _():
        block(qi - 1, 1, False)

    block(qi, 1, True)
    o_ref[...] = (acc_ref[...] / l_ref[...]).T.astype(o_ref.dtype)


def _flash_attention(q_cat, k_cat, vt, *, batch, seq, tq):
    nq = seq // tq
    kern = functools.partial(_flash_kernel, tq=tq, scale=QK_HEAD ** -0.5)
    return pl.pallas_call(
        kern,
        grid=(batch, MLA_HEADS, nq),
        in_specs=[pl.BlockSpec((tq, QK_PAD), lambda b, h, i: (b * nq + i, h)),
                  pl.BlockSpec((seq, QK_PAD), lambda b, h, i: (b, h)),
                  pl.BlockSpec((nq, V_HEAD, tq), lambda b, h, i: (b, h, 0))],
        out_specs=pl.BlockSpec((tq, V_HEAD), lambda b, h, i: (b * nq + i, h)),
        out_shape=jax.ShapeDtypeStruct((batch * seq, MLA_HEADS * V_HEAD), BF16),
        scratch_shapes=[pltpu.VMEM((1, tq), F32), pltpu.VMEM((1, tq), F32), pltpu.VMEM((V_HEAD, tq), F32)],
        compiler_params=_cparams(("arbitrary", "arbitrary", "arbitrary")),
        name="mla_prompt_attention",
    )(q_cat, k_cat, vt)


PAGES_PER_STEP = 16
PAGE_RING = 3
PAGE_GROUPS = 2


def _paged_kernel(pt_ref, ckv_hbm, krt_hbm, wukt_ref, qa_ref, qcat_ref, qcat_prev_ref, kcat_prev_ref, cn_prev_ref,
                  o_ref, lhs_ref, cbuf, krbuf, csem, krsem, c_scr, krt_scr, s_scr, m_ref, l_ref, acc_ref,
                  *, scale, steps, n_chunks):
    npg = PAGES_PER_STEP
    t = pl.program_id(0)
    n_hd = MLA_HEADS * QK_NOPE
    span = npg * PAGE_SIZE // PAGE_GROUPS
    prev_chunk = (t + steps - 1) % steps

    def page_copies(chunk, slot, resolve_pages):
        base = jnp.minimum(chunk, n_chunks - 1) * npg
        copies = []
        for i in range(npg):
            page = pt_ref[base + i] if resolve_pages else 0
            copies.append(pltpu.make_async_copy(ckv_hbm.at[page], cbuf.at[slot, i], csem.at[slot]))
            copies.append(pltpu.make_async_copy(krt_hbm.at[page], krbuf.at[slot, i], krsem.at[slot]))
        return copies

    @pl.when(t == 0)
    def _():
        for chunk in range(PAGE_RING - 1):
            for cp in page_copies(chunk, chunk, True):
                cp.start()
        lhs_ref[0:n_hd, :] = wukt_ref[...]
        c_scr[1] = jnp.zeros(c_scr.shape[1:], BF16)
        s_scr[1] = jnp.zeros(s_scr.shape[1:], F32)
        m_ref[...] = jnp.zeros(m_ref.shape, F32)
        l_ref[...] = jnp.zeros(l_ref.shape, F32)
        acc_ref[...] = jnp.zeros(acc_ref.shape, F32)

    @pl.when(t % steps == 0)
    def _():
        lhs_ref[n_hd:n_hd + MLA_HEADS, :] = qa_ref[...]

    ring = t % PAGE_RING

    def step(cur, prv):
        for cp in page_copies(t, ring, False):
            cp.wait()
        first = prev_chunk == 0
        s_prev = s_scr[prv]
        m_old = jnp.where(first, -jnp.inf, m_ref[...])
        l_old = jnp.where(first, 0.0, l_ref[...])
        acc_old = jnp.where(first, 0.0, acc_ref[...])
        m_new = jnp.maximum(m_old, jnp.max(s_prev, axis=-1, keepdims=True))
        alpha = jnp.exp(m_old - m_new)
        pr = jnp.exp(s_prev - m_new)
        l_ref[...] = l_old * alpha + jnp.sum(pr, axis=-1, keepdims=True)
        acc_ref[...] = acc_old * alpha + jnp.dot(pr.astype(BF16), c_scr[prv], preferred_element_type=F32)
        m_ref[...] = m_new
        for i in range(npg):
            c_scr[cur, i * PAGE_SIZE:(i + 1) * PAGE_SIZE, :] = cbuf[ring, i].astype(BF16)
            krt_scr[:, i * PAGE_SIZE:(i + 1) * PAGE_SIZE] = krbuf[ring, i].astype(BF16)
        qr = qcat_ref[:, QK_NOPE:QK_NOPE + QK_ROPE]
        for g in range(PAGE_GROUPS):
            c = c_scr[cur, g * span:(g + 1) * span, :]
            res = _nt_dot(lhs_ref[...], c)
            kt = res[0:n_hd, :].reshape(MLA_HEADS, QK_NOPE, span)
            inv = lax.rsqrt(jnp.sum(kt * kt, axis=1) * (1.0 / QK_NOPE) + EPS)
            s_rope = jnp.dot(qr, krt_scr[:, g * span:(g + 1) * span], preferred_element_type=F32)
            s_scr[cur, :, g * span:(g + 1) * span] = (res[n_hd:n_hd + MLA_HEADS, :] * inv + s_rope) * scale
        for cp in page_copies(t + PAGE_RING - 1, (t + PAGE_RING - 1) % PAGE_RING, True):
            cp.start()

    @pl.when(t % 2 == 0)
    def _():
        step(0, 1)

    @pl.when(t % 2 == 1)
    def _():
        step(1, 0)

    @pl.when((prev_chunk == steps - 1) & (t > 0))
    def _():
        s_new = jnp.sum(qcat_prev_ref[...].astype(F32) * kcat_prev_ref[...].astype(F32), axis=-1,
                        keepdims=True) * scale
        m_old = m_ref[...]
        m_fin = jnp.maximum(m_old, s_new)
        a2 = jnp.exp(m_old - m_fin)
        p_new = jnp.exp(s_new - m_fin)
        l_fin = l_ref[...] * a2 + p_new
        cn = cn_prev_ref[...].astype(BF16).astype(F32)
        o_ref[...] = (acc_ref[...] * a2 + p_new * cn) / l_fin

    @pl.when(t == n_chunks)
    def _():
        for ahead in range(1, PAGE_RING):
            for cp in page_copies(t + ahead, (t + ahead) % PAGE_RING, False):
                cp.wait()


def _paged_attention(page_table, ckv_pool, krt_pool, wukt, qa, q_cat, k_cat, cn):
    n_s, n_pages = page_table.shape
    npg = PAGES_PER_STEP
    assert n_pages % npg == 0
    steps = n_pages // npg
    n_chunks = n_s * steps
    span = npg * PAGE_SIZE
    pt_flat = page_table.reshape(n_s * n_pages)

    def cur_sample(shape):
        return pl.BlockSpec((None,) + shape, lambda t, pt: (jnp.minimum(t // steps, n_s - 1), 0, 0))

    def prev_sample(shape):
        return pl.BlockSpec((None,) + shape, lambda t, pt: (jnp.maximum(t - 1, 0) // steps, 0, 0))

    in_specs = [pl.BlockSpec(memory_space=pl.ANY), pl.BlockSpec(memory_space=pl.ANY),
                pl.BlockSpec(wukt.shape, lambda t, pt: (0, 0)),
                cur_sample((MLA_HEADS, KV_LORA)), cur_sample((MLA_HEADS, QK_PAD)),
                prev_sample((MLA_HEADS, QK_PAD)), prev_sample((MLA_HEADS, QK_PAD)), prev_sample((1, KV_LORA))]
    grid_spec = pltpu.PrefetchScalarGridSpec(
        num_scalar_prefetch=1,
        grid=(n_chunks + 1,),
        in_specs=in_specs,
        out_specs=prev_sample((MLA_HEADS, KV_LORA)),
        scratch_shapes=[pltpu.VMEM((MLA_HEADS * QK_NOPE + MLA_HEADS, KV_LORA), BF16),
                        pltpu.VMEM((PAGE_RING, npg, PAGE_SIZE, KV_LORA), F32),
                        pltpu.VMEM((PAGE_RING, npg, QK_ROPE, PAGE_SIZE), F32),
                        pltpu.SemaphoreType.DMA((PAGE_RING,)), pltpu.SemaphoreType.DMA((PAGE_RING,)),
                        pltpu.VMEM((2, span, KV_LORA), BF16), pltpu.VMEM((QK_ROPE, span), BF16),
                        pltpu.VMEM((2, MLA_HEADS, span), F32),
                        pltpu.VMEM((MLA_HEADS, 1), F32), pltpu.VMEM((MLA_HEADS, 1), F32),
                        pltpu.VMEM((MLA_HEADS, KV_LORA), F32)])
    return pl.pallas_call(
        functools.partial(_paged_kernel, scale=QK_HEAD ** -0.5, steps=steps, n_chunks=n_chunks),
        grid_spec=grid_spec,
        out_shape=jax.ShapeDtypeStruct((n_s, MLA_HEADS, KV_LORA), F32),
        compiler_params=_cparams(("arbitrary",)),
        name="mla_sample_attention",
    )(pt_flat, ckv_pool, krt_pool, wukt, qa, q_cat, q_cat, k_cat, cn)


def _gdn_prep_kernel(x_ref, halo_ref, misc_ref, cw_ref, alog_ref, dtb_ref, qkv_ref, kt_ref, gates_ref, xs_ref,
                     *, tiles_per_seq):
    i = pl.program_id(0)
    s = pl.program_id(1)
    tm = x_ref.shape[0]
    first = (i % tiles_per_seq) == 0

    @pl.when(first)
    def _():
        xs_ref[0:SUBLANES, :] = jnp.zeros((SUBLANES, xs_ref.shape[1]), F32)

    @pl.when(jnp.logical_not(first))
    def _():
        xs_ref[0:SUBLANES, :] = halo_ref[...]

    xs_ref[SUBLANES:SUBLANES + tm, :] = x_ref[...]
    acc = xs_ref[pl.ds(SUBLANES - (CONV_W - 1), tm), :] * cw_ref[0:1, :]
    for w in range(1, CONV_W):
        acc = acc + xs_ref[pl.ds(SUBLANES - (CONV_W - 1) + w, tm), :] * cw_ref[w:w + 1, :]
    y = acc * _sigmoid(acc)

    @pl.when(s == 2)
    def _():
        qkv_ref[...] = y

    @pl.when(s < 2)
    def _():
        scale = jnp.where(s == 0, GDN_DK ** -0.5, 1.0).astype(F32)
        for h in range(GDN_HEADS):
            yh = y[:, h * GDN_DK:(h + 1) * GDN_DK]
            yn = yh * (lax.rsqrt(jnp.sum(yh * yh, axis=-1, keepdims=True) + EPS) * scale)
            qkv_ref[:, h * GDN_DK:(h + 1) * GDN_DK] = yn

    @pl.when(s == 1)
    def _():
        kt_ref[...] = qkv_ref[...].T

    @pl.when(s == 0)
    def _():
        x = misc_ref[...] + dtb_ref[...]
        softplus = jnp.maximum(x, 0.0) + jnp.log(1.0 + jnp.exp(-jnp.abs(x)))
        lane = lax.broadcasted_iota(jnp.int32, x.shape, 1)
        is_beta = (lane >= MISC_B) & (lane < MISC_B + GDN_HEADS)
        gates_ref[...] = jnp.where(is_beta, _sigmoid(misc_ref[...]), -jnp.exp(alog_ref[...]) * softplus)


def _gdn_prep(proj, conv_w, alog_l, dtb_l, *, batch, seq, tm):
    n = batch * seq
    tm = _tile(seq, tm, LANES)
    tps = seq // tm
    sec0 = SEG_QKV // GDN_HK
    halo_blocks = tm // SUBLANES
    return pl.pallas_call(
        functools.partial(_gdn_prep_kernel, tiles_per_seq=tps),
        grid=(n // tm, 3),
        in_specs=[pl.BlockSpec((tm, GDN_HK), lambda i, s: (i, sec0 + s)),
                  pl.BlockSpec((SUBLANES, GDN_HK), lambda i, s: (jnp.maximum(i * halo_blocks - 1, 0), sec0 + s)),
                  pl.BlockSpec((tm, LANES), lambda i, s: (i, SEG_MISC // LANES)),
                  pl.BlockSpec((CONV_W, GDN_HK), lambda i, s: (0, s)),
                  pl.BlockSpec((1, LANES), lambda i, s: (0, 0)),
                  pl.BlockSpec((1, LANES), lambda i, s: (0, 0))],
        out_specs=[pl.BlockSpec((tm, GDN_HK), lambda i, s: (i, s)),
                   pl.BlockSpec((None, GDN_HK, tm), lambda i, s: (i // tps, 0, i % tps)),
                   pl.BlockSpec((tm, LANES), lambda i, s: (i, 0))],
        out_shape=[jax.ShapeDtypeStruct((n, 3 * GDN_HK), F32),
                   jax.ShapeDtypeStruct((batch, GDN_HK, seq), F32),
                   jax.ShapeDtypeStruct((n, LANES), F32)],
        scratch_shapes=[pltpu.VMEM((tm + SUBLANES, GDN_HK), F32)],
        compiler_params=_cparams(("arbitrary", "arbitrary")),
        name="gdn_prep",
    )(proj, proj, proj, conv_w, alog_l, dtb_l)


def _split3(x):
    hi = x.astype(BF16)
    r1 = x - hi.astype(F32)
    mid = r1.astype(BF16)
    lo = (r1 - mid.astype(F32)).astype(BF16)
    return hi, mid, lo


def _bdot(a, b):
    return jnp.dot(a.astype(BF16), b.astype(BF16), preferred_element_type=F32)


def _delta_kernel(q_ref, k_ref, v_ref, kt_ref, gates_ref, o_ref, s_out_ref, s_ref):
    c = GDN_CHUNK
    n = pl.program_id(1)
    heads = range(GDN_HEADS)

    @pl.when(n == 0)
    def _():
        s_ref[...] = jnp.zeros(s_ref.shape, F32)

    row = lax.broadcasted_iota(jnp.int32, (c, c), 0)
    col = lax.broadcasted_iota(jnp.int32, (c, c), 1)
    incl = row >= col
    strict = row > col
    eye = (row == col).astype(F32)
    tril = incl.astype(BF16)
    triu = (row <= col).astype(BF16)
    gates = gates_ref[...]
    gates_t = gates.T
    gc_all = sum(jnp.dot(tril, part, preferred_element_type=F32) for part in _split3(gates))
    gct_all = sum(jnp.dot(part, triu, preferred_element_type=F32) for part in _split3(gates_t))

    sl = [slice(h * GDN_DK, (h + 1) * GDN_DK) for h in heads]
    gcol = [gc_all[:, MISC_A + h:MISC_A + h + 1] for h in heads]
    grow = [gct_all[MISC_A + h:MISC_A + h + 1, :] for h in heads]
    glast = [gct_all[MISC_A + h:MISC_A + h + 1, c - 1:c] for h in heads]
    bcol = [gates[:, MISC_B + h:MISC_B + h + 1] for h in heads]
    eg = [jnp.exp(g) for g in gcol]
    e = [jnp.exp(jnp.where(incl, gcol[h] - grow[h], 0.0)) for h in heads]
    kb = [k_ref[:, sl[h]] * bcol[h] for h in heads]
    ktb = [kt_ref[sl[h], :].astype(BF16) for h in heads]
    x = [-(jnp.dot(kb[h].astype(BF16), ktb[h], preferred_element_type=F32) * jnp.where(strict, e[h], 0.0))
         for h in heads]
    t = [eye + x[h] for h in heads]
    for _ in range(6):
        x = [_bdot(x[h], x[h]) for h in heads]
        t = [t[h] + _bdot(t[h], x[h]) for h in heads]
    rhs = [jnp.concatenate([v_ref[:, sl[h]] * bcol[h], kb[h] * eg[h]], axis=1) for h in heads]
    sol = [_bdot(t[h], rhs[h]) for h in heads]
    attn = [jnp.dot(q_ref[:, sl[h]].astype(BF16), ktb[h], preferred_element_type=F32) * jnp.where(incl, e[h], 0.0)
            for h in heads]
    s_old = [s_ref[h] for h in heads]
    lhs = [jnp.concatenate([sol[h][:, GDN_DV:], q_ref[:, sl[h]] * eg[h]], axis=0) for h in heads]
    prod = [_bdot(lhs[h], s_old[h]) for h in heads]
    v_new = [sol[h][:, :GDN_DV] - prod[h][:c] for h in heads]
    for h in heads:
        o_ref[:, sl[h]] = prod[h][c:] + _bdot(attn[h], v_new[h])
    for h in heads:
        kdt = kt_ref[sl[h], :] * jnp.exp(glast[h] - grow[h])
        s_ref[h] = s_old[h] * jnp.exp(glast[h]) + _bdot(kdt, v_new[h])

    @pl.when(n == pl.num_programs(1) - 1)
    def _():
        s_out_ref[...] = s_ref[...]


def _delta_rule(qkv, kt, gates, *, batch, seq):
    c = GDN_CHUNK
    assert seq % c == 0
    nc = seq // c
    sec = lambda s: pl.BlockSpec((c, GDN_HK), lambda b, n, s=s: (b * nc + n, s))
    return pl.pallas_call(
        _delta_kernel,
        grid=(batch, nc),
        in_specs=[sec(0), sec(1), sec(2),
                  pl.BlockSpec((None, GDN_HK, c), lambda b, n: (b, 0, n)),
                  pl.BlockSpec((c, LANES), lambda b, n: (b * nc + n, 0))],
        out_specs=[pl.BlockSpec((c, GDN_HK), lambda b, n: (b * nc + n, 0)),
                   pl.BlockSpec((None, GDN_HEADS, GDN_DK, GDN_DV), lambda b, n: (b, 0, 0, 0))],
        out_shape=[jax.ShapeDtypeStruct((batch * seq, GDN_HK), F32),
                   jax.ShapeDtypeStruct((batch, GDN_HEADS, GDN_DK, GDN_DV), F32)],
        scratch_shapes=[pltpu.VMEM((GDN_HEADS, GDN_DK, GDN_DV), F32)],
        compiler_params=_cparams(("arbitrary", "arbitrary")),
        name="gdn_delta_rule",
    )(qkv, qkv, qkv, kt, gates)


GDN_STEP_ROWS = SUBLANES


def _gdn_step_kernel(x_ref, buf_ref, misc_ref, cw_ref, alog_ref, dtb_ref, s_ref, o_ref, s_out_ref):
    nb = x_ref.shape[0]
    acc = x_ref[...] * cw_ref[CONV_W - 1:CONV_W, :]
    for w in range(CONV_W - 1):
        acc = acc + buf_ref[:, w, :] * cw_ref[w:w + 1, :]
    y = acc * _sigmoid(acc)
    x = misc_ref[...] + dtb_ref[...]
    softplus = jnp.maximum(x, 0.0) + jnp.log(1.0 + jnp.exp(-jnp.abs(x)))
    eg_all = jnp.exp(-jnp.exp(alog_ref[...]) * softplus)
    beta_all = _sigmoid(misc_ref[...])
    for h in range(GDN_HEADS):
        qs = slice(h * GDN_DK, (h + 1) * GDN_DK)
        ks = slice(GDN_HK + h * GDN_DK, GDN_HK + (h + 1) * GDN_DK)
        vs = slice(2 * GDN_HK + h * GDN_DV, 2 * GDN_HK + (h + 1) * GDN_DV)
        q = y[:, qs]
        k = y[:, ks]
        q = q * (lax.rsqrt(jnp.sum(q * q, axis=-1, keepdims=True) + EPS) * (GDN_DK ** -0.5))
        k = k * lax.rsqrt(jnp.sum(k * k, axis=-1, keepdims=True) + EPS)
        eg = eg_all[:, MISC_A + h:MISC_A + h + 1]
        beta = beta_all[:, MISC_B + h:MISC_B + h + 1]
        vb = y[:, vs] * beta
        kcd = k * (beta * eg)
        qd = q * eg
        qk = jnp.sum(q * k, axis=-1, keepdims=True)
        kt = k.T
        for b in range(nb):
            s_old = s_ref[b, h]
            lhs = jnp.concatenate([kcd[b:b + 1], qd[b:b + 1], jnp.zeros((SUBLANES - 2, GDN_DK), F32)], axis=0)
            prod = jnp.dot(lhs.astype(BF16), s_old.astype(BF16), preferred_element_type=F32)
            v_new = vb[b:b + 1] - prod[0:1]
            o_ref[b:b + 1, qs] = prod[1:2] + qk[b:b + 1] * v_new
            s_out_ref[b, h] = s_old * eg[b:b + 1] + kt[:, b:b + 1] * v_new


def _gdn_step(proj, state_conv, state_ssm, conv_w, alog_l, dtb_l):
    n_s = proj.shape[0]
    nb = min(GDN_STEP_ROWS, n_s)
    assert n_s % nb == 0
    return pl.pallas_call(
        _gdn_step_kernel,
        grid=(n_s // nb,),
        in_specs=[pl.BlockSpec((nb, GDN_QKV_W), lambda i: (i, SEG_QKV // GDN_QKV_W)),
                  pl.BlockSpec((nb, CONV_W - 1, GDN_QKV_W), lambda i: (i, 0, 0)),
                  pl.BlockSpec((nb, LANES), lambda i: (i, SEG_MISC // LANES)),
                  pl.BlockSpec((CONV_W, GDN_QKV_W), lambda i: (0, 0)),
                  pl.BlockSpec((1, LANES), lambda i: (0, 0)),
                  pl.BlockSpec((1, LANES), lambda i: (0, 0)),
                  pl.BlockSpec((nb, GDN_HEADS, GDN_DK, GDN_DV), lambda i: (i, 0, 0, 0))],
        out_specs=[pl.BlockSpec((nb, GDN_HK), lambda i: (i, 0)),
                   pl.BlockSpec((nb, GDN_HEADS, GDN_DK, GDN_DV), lambda i: (i, 0, 0, 0))],
        out_shape=[jax.ShapeDtypeStruct((n_s, GDN_HK), F32),
                   jax.ShapeDtypeStruct(state_ssm.shape, F32)],
        compiler_params=_cparams(("arbitrary",)),
        name="gdn_step",
    )(proj, state_conv, proj, conv_w, alog_l, dtb_l, state_ssm)


def _merge_kernel(og_ref, z_ref, oa_ref, ga_ref, gb_ref, gn_ref, wpa_ref, wpb_ref, m_ref, ob_ref):
    @pl.when(pl.program_id(1) == 0)
    def _():
        gn = gn_ref[...]
        for h in range(GDN_HEADS):
            sl = slice(h * GDN_DV, (h + 1) * GDN_DV)
            z = z_ref[:, sl]
            ob_ref[:, sl] = (_rms_rows(og_ref[:, sl], gn) * (z * _sigmoid(z))).astype(BF16)

    pa = jnp.dot(oa_ref[...], wpa_ref[...], preferred_element_type=F32)
    pb = jnp.dot(ob_ref[...], wpb_ref[...], preferred_element_type=F32)
    m_ref[...] = (_sigmoid(ga_ref[...]) * pa + _sigmoid(gb_ref[...]) * pb).astype(BF16)


def _merge(o_g, oa, proj, gdn_norm, w_pa_b, w_pb_b, *, tm):
    n = o_g.shape[0]
    tm = _tile(n, tm, 16)
    tn = 1024
    nj = D_MODEL // tn
    return pl.pallas_call(
        _merge_kernel,
        grid=(n // tm, nj),
        in_specs=[pl.BlockSpec((tm, GDN_HK), lambda i, j: (i, 0)),
                  pl.BlockSpec((tm, GDN_HK), lambda i, j: (i, SEG_Z // GDN_HK)),
                  pl.BlockSpec((tm, MLA_HEADS * V_HEAD), lambda i, j: (i, 0)),
                  pl.BlockSpec((tm, tn), lambda i, j: (i, SEG_GA // tn + j)),
                  pl.BlockSpec((tm, tn), lambda i, j: (i, SEG_GB // tn + j)),
                  pl.BlockSpec((1, GDN_DV), lambda i, j: (0, 0)),
                  pl.BlockSpec((MLA_HEADS * V_HEAD, tn), lambda i, j: (0, j)),
                  pl.BlockSpec((GDN_HK, tn), lambda i, j: (0, j))],
        out_specs=pl.BlockSpec((tm, tn), lambda i, j: (i, j)),
        out_shape=jax.ShapeDtypeStruct((n, D_MODEL), BF16),
        scratch_shapes=[pltpu.VMEM((tm, GDN_HK), BF16)],
        compiler_params=_cparams(("arbitrary", "arbitrary")),
        name="gated_merge",
    )(o_g, proj, oa, proj, proj, gdn_norm, w_pa_b, w_pb_b)


def _pack_bf16_pairs(x):
    w = x.shape[1] // 2
    bits = lax.bitcast_convert_type(x.astype(F32), jnp.uint32)
    return (bits[:, :w] >> 16) | bits[:, w:]


def _unpack_bf16_pairs(p):
    lo = lax.bitcast_convert_type(p << 16, F32)
    hi = lax.bitcast_convert_type(p & jnp.uint32(0xFFFF0000), F32)
    return jnp.concatenate([lo, hi], axis=1).astype(BF16)


ROUTE_E1, ROUTE_E2, ROUTE_W1, ROUTE_W2 = 0, 1, 2, 3


def _first_lane_of_max(x, lane):
    m = jnp.max(x, axis=-1, keepdims=True)
    return m, jnp.min(jnp.where(x == m, lane, LANES), axis=-1, keepdims=True)


def _route(logits, lane):
    neg = -jnp.inf
    is_g = lane < N_GROUPS
    gmax, grp = _first_lane_of_max(jnp.where(is_g, logits, neg), lane)
    p_grp = 1.0 / jnp.sum(jnp.where(is_g, jnp.exp(logits - gmax), 0.0), axis=-1, keepdims=True)
    lo = N_GROUPS + grp * EXP_PER_GROUP
    in_grp = (lane >= lo) & (lane < lo + EXP_PER_GROUP)
    emax = jnp.max(jnp.where(in_grp, logits, neg), axis=-1, keepdims=True)
    ee = jnp.where(in_grp, jnp.exp(logits - emax), 0.0)
    soft = jnp.where(in_grp, ee / jnp.sum(ee, axis=-1, keepdims=True), -1.0)
    w1, i1 = _first_lane_of_max(soft, lane)
    w2, i2 = _first_lane_of_max(jnp.where(lane == i1, -1.0, soft), lane)
    wsum = w1 + w2
    tile = jnp.where(lane == ROUTE_E1, (i1 - N_GROUPS).astype(F32), 0.0)
    tile = jnp.where(lane == ROUTE_E2, (i2 - N_GROUPS).astype(F32), tile)
    tile = jnp.where(lane == ROUTE_W1, w1 / wsum * p_grp, tile)
    return jnp.where(lane == ROUTE_W2, w2 / wsum * p_grp, tile)


def _out_proj_kernel(mp_ref, xp_ref, ms_ref, xs_ref, wo_ref, gf_ref, wrt_ref, brt_ref, h_ref, hn_ref, rt_ref,
                     *, np_tiles):
    i = pl.program_id(0)

    def rows(m_ref, x_ref, n):
        h = x_ref[...] + jnp.dot(m_ref[...], wo_ref[...], preferred_element_type=F32)
        h_ref[0:n, :] = h
        hnb = _rms_rows(h, gf_ref[...]).astype(BF16)
        hn_ref[0:n, :] = _pack_bf16_pairs(hnb)
        logits = jnp.dot(hnb, wrt_ref[...], preferred_element_type=F32) + brt_ref[...]
        rt_ref[0:n, :] = _route(logits, lax.broadcasted_iota(jnp.int32, logits.shape, 1))

    @pl.when(i < np_tiles)
    def _():
        rows(mp_ref, xp_ref, mp_ref.shape[0])

    @pl.when(i == np_tiles)
    def _():
        rows(ms_ref, xs_ref, ms_ref.shape[0])


def _out_proj(m_p, x_p, m_s, x_s, w_o_b, norm_ffn, w_rt_b, b_rt, *, tm):
    n_p, n_s = m_p.shape[0], m_s.shape[0]
    tm = _tile(n_p, tm, 16)
    np_tiles = n_p // tm
    assert n_s <= tm
    const = lambda a: pl.BlockSpec(a.shape, lambda i: (0,) * a.ndim)
    p_spec = pl.BlockSpec((tm, D_MODEL), lambda i: (jnp.minimum(i, np_tiles - 1), 0))
    out = lambda w: pl.BlockSpec((tm, w), lambda i: (i, 0))
    return pl.pallas_call(
        functools.partial(_out_proj_kernel, np_tiles=np_tiles),
        grid=(np_tiles + 1,),
        in_specs=[p_spec, p_spec, const(m_s), const(x_s), const(w_o_b), const(norm_ffn), const(w_rt_b),
                  const(b_rt)],
        out_specs=[out(D_MODEL), out(D_MODEL // 2), out(LANES)],
        out_shape=[jax.ShapeDtypeStruct((n_p + n_s, D_MODEL), F32),
                   jax.ShapeDtypeStruct((n_p + n_s, D_MODEL // 2), jnp.uint32),
                   jax.ShapeDtypeStruct((n_p + n_s, LANES), F32)],
        compiler_params=_cparams(("arbitrary",)),
        name="out_proj_router",
    )(m_p, x_p, m_s, x_s, w_o_b, norm_ffn, w_rt_b, b_rt)


def _expert_weights(sched_ref, w_hbm, wbuf, wb_ref, sem):
    i = pl.program_id(0)

    def fetch(expert, slot):
        return pltpu.make_async_copy(w_hbm.at[expert], wbuf.at[slot], sem.at[slot])

    @pl.when(i == 0)
    def _():
        fetch(sched_ref[0, 0], 0).start()

    @pl.when(sched_ref[2, i] != 0)
    def _():
        slot = sched_ref[3, i]
        fetch(sched_ref[0, i], slot).wait()

        @pl.when(sched_ref[4, i] >= 0)
        def _():
            fetch(sched_ref[4, i], 1 - slot).start()

        wb_ref[...] = wbuf[slot].astype(BF16)


def _gmm_kernel(sched_ref, x_ref, w_hbm, o_ref, wbuf, wb_ref, sem):
    i = pl.program_id(0)
    _expert_weights(sched_ref, w_hbm, wbuf, wb_ref, sem)

    @pl.when(sched_ref[1, i] != 0)
    def _():
        o_ref[...] = jnp.dot(_unpack_bf16_pairs(x_ref[...]), wb_ref[...], preferred_element_type=F32)

    @pl.when(sched_ref[1, i] == 0)
    def _():
        o_ref[...] = jnp.zeros(o_ref.shape, o_ref.dtype)


def _gmm_down_kernel(sched_ref, g_ref, u_ref, w_hbm, acc_hbm, o_ref, wbuf, wb_ref, sem):
    del acc_hbm
    i = pl.program_id(0)
    _expert_weights(sched_ref, w_hbm, wbuf, wb_ref, sem)

    @pl.when(sched_ref[1, i] != 0)
    def _():
        gate = g_ref[...]
        act = gate * _sigmoid(gate) * u_ref[...]
        o_ref[...] = jnp.dot(act.astype(BF16), wb_ref[...], preferred_element_type=F32)

    @pl.when(sched_ref[1, i] == 0)
    def _():
        o_ref[...] = jnp.zeros(o_ref.shape, o_ref.dtype)


def _expert_schedule(bexp, used):
    nb = bexp.shape[0]
    first = jnp.concatenate([jnp.ones((1,), jnp.int32), (bexp[1:] != bexp[:-1]).astype(jnp.int32)])
    run = jnp.cumsum(first) - 1
    run_expert = jnp.zeros((nb,), jnp.int32).at[run].set(bexp)
    nxt = jnp.where(run + 1 <= run[-1], run_expert[jnp.minimum(run + 1, nb - 1)], -1)
    return jnp.stack([bexp, used, first, run % 2, nxt]).astype(jnp.int32)


def _grouped_call(kern, sched, row_inputs, w, *, name, out_into=None, block_offset=0):
    rows = row_inputs[0].shape[0]
    k, n = w.shape[1], w.shape[2]
    nb = rows // MOE_ROWS
    in_specs = ([pl.BlockSpec((MOE_ROWS, a.shape[1]), lambda i, sc: (i, 0)) for a in row_inputs]
                + [pl.BlockSpec(memory_space=pl.ANY)])
    args = [sched, *row_inputs, w]
    aliases = {}
    if out_into is not None:
        in_specs.append(pl.BlockSpec(memory_space=pl.ANY))
        aliases = {len(args): 0}
        args.append(out_into)
        rows = out_into.shape[0]
    grid_spec = pltpu.PrefetchScalarGridSpec(
        num_scalar_prefetch=1,
        grid=(nb,),
        in_specs=in_specs,
        out_specs=pl.BlockSpec((MOE_ROWS, n), lambda i, sc: (i + block_offset, 0)),
        scratch_shapes=[pltpu.VMEM((2, k, n), F32), pltpu.VMEM((k, n), BF16), pltpu.SemaphoreType.DMA((2,))])
    return pl.pallas_call(
        kern, grid_spec=grid_spec,
        out_shape=jax.ShapeDtypeStruct((rows, n), F32),
        input_output_aliases=aliases,
        compiler_params=_cparams(("arbitrary",)),
        name=name,
    )(*args)


def _l2n(x):
    return x * lax.rsqrt(jnp.sum(x * x, axis=-1, keepdims=True) + EPS)


def _rms(x, g):
    return x * lax.rsqrt(jnp.mean(x * x, axis=-1, keepdims=True) + EPS) * g


def _rope_tables(pos):
    inv = ROPE_THETA ** (-jnp.arange(0, QK_ROPE, 2, dtype=F32) / QK_ROPE)
    ang = pos[:, None] * inv[None, :]
    cos, sin = jnp.cos(ang), jnp.sin(ang)
    return jnp.concatenate([cos, cos, cos, cos], axis=1), jnp.concatenate([-sin, sin, -sin, sin], axis=1)


def _w_in_segments():
    segs = []
    for h in range(MLA_HEADS):
        segs.append((h * QK_HEAD, QK_NOPE, SEG_QN + h * QK_NOPE))
        segs.append((h * QK_HEAD + QK_NOPE, QK_ROPE, SEG_QR + h * QK_ROPE))
    o = MLA_HEADS * QK_HEAD
    segs.append((o, KV_LORA, SEG_CKV)); o += KV_LORA
    segs.append((o, QK_ROPE, SEG_MISC)); o += QK_ROPE
    segs.append((o, GDN_QKV_W, SEG_QKV)); o += GDN_QKV_W
    segs.append((o, GDN_HEADS, SEG_MISC + MISC_A)); o += GDN_HEADS
    segs.append((o, GDN_HEADS, SEG_MISC + MISC_B)); o += GDN_HEADS
    segs.append((o, GDN_HEADS * GDN_DV, SEG_Z)); o += GDN_HEADS * GDN_DV
    segs.append((o, D_MODEL, SEG_GA)); o += D_MODEL
    segs.append((o, D_MODEL, SEG_GB))
    return segs


def _reorder_kernel(w_ref, o_ref):
    for src, width, dst in _w_in_segments():
        o_ref[:, dst:dst + width] = w_ref[:, src:src + width].astype(BF16)
    tail = SEG_MISC + MISC_B + GDN_HEADS
    o_ref[:, tail:D_PROJ] = jnp.zeros((o_ref.shape[0], D_PROJ - tail), BF16)


def _reorder_w_in_pallas(w_in):
    d, d_in = w_in.shape
    tr = 128
    return pl.pallas_call(
        _reorder_kernel,
        grid=(d // tr,),
        in_specs=[pl.BlockSpec((tr, d_in), lambda i: (i, 0))],
        out_specs=pl.BlockSpec((tr, D_PROJ), lambda i: (i, 0)),
        out_shape=jax.ShapeDtypeStruct((d, D_PROJ), BF16),
        compiler_params=_cparams(("arbitrary",)),
        name="w_in_reorder",
    )(w_in)


def _reorder_w_in(w_in):
    o = 0
    wq = w_in[:, o:o + MLA_HEADS * QK_HEAD].reshape(D_MODEL, MLA_HEADS, QK_HEAD); o += MLA_HEADS * QK_HEAD
    wckv = w_in[:, o:o + KV_LORA]; o += KV_LORA
    wkr = w_in[:, o:o + QK_ROPE]; o += QK_ROPE
    wqkv = w_in[:, o:o + GDN_QKV_W]; o += GDN_QKV_W
    wa = w_in[:, o:o + GDN_HEADS]; o += GDN_HEADS
    wb = w_in[:, o:o + GDN_HEADS]; o += GDN_HEADS
    wz = w_in[:, o:o + GDN_HEADS * GDN_DV]; o += GDN_HEADS * GDN_DV
    wga = w_in[:, o:o + D_MODEL]; o += D_MODEL
    wgb = w_in[:, o:o + D_MODEL]
    pad = jnp.zeros((D_MODEL, D_PROJ - SEG_MISC - QK_ROPE - 2 * GDN_HEADS), w_in.dtype)
    return jnp.concatenate([
        wz, wga, wgb, wqkv, wq[:, :, :QK_NOPE].reshape(D_MODEL, -1), wq[:, :, QK_NOPE:].reshape(D_MODEL, -1),
        wckv, wkr, wa, wb, pad], axis=1).astype(BF16)


def _lane_vec(v, offset):
    return jnp.zeros((1, LANES), F32).at[0, offset:offset + v.shape[0]].set(v.astype(F32))


def _hier_moe(h, hn, route, w_gate, w_up, w_down):
    t = h.shape[0]
    expert = route[:, ROUTE_E1:ROUTE_E2 + 1].astype(jnp.int32)
    top_w = route[:, ROUTE_W1:ROUTE_W2 + 1]
    n_assign = t * TOP_K
    e_flat = expert.reshape(n_assign)
    onehot = (e_flat[:, None] == jnp.arange(N_EXPERTS, dtype=jnp.int32)[None, :]).astype(jnp.int32)
    csum = jnp.cumsum(onehot, axis=0)
    rank = jnp.take_along_axis(csum, e_flat[:, None], axis=1)[:, 0] - 1
    counts = csum[-1]
    pcounts = (counts + MOE_ROWS - 1) // MOE_ROWS * MOE_ROWS
    pends = jnp.cumsum(pcounts)
    pstarts = pends - pcounts
    dest = pstarts[e_flat] + rank
    n_blocks = -(-(-(-n_assign // MOE_ROWS) + N_EXPERTS) // MOE_SPLITS) * MOE_SPLITS
    n_rows = n_blocks * MOE_ROWS
    tok = jnp.arange(n_assign, dtype=jnp.int32) // TOP_K
    src = jnp.zeros((n_rows,), jnp.int32).at[dest].set(tok)
    bstart = jnp.arange(n_blocks, dtype=jnp.int32) * MOE_ROWS
    bexp = jnp.minimum(jnp.searchsorted(pends, bstart, side='right'), N_EXPERTS - 1).astype(jnp.int32)
    used = (bstart < pends[-1]).astype(jnp.int32)
    bq = n_blocks // MOE_SPLITS
    yb = jnp.zeros((n_rows, D_MODEL), F32)
    for q in range(MOE_SPLITS):
        xb = hn[src[q * bq * MOE_ROWS:(q + 1) * bq * MOE_ROWS]]
        sched = _expert_schedule(bexp[q * bq:(q + 1) * bq], used[q * bq:(q + 1) * bq])
        gate = _grouped_call(_gmm_kernel, sched, [xb], w_gate, name="moe_gate")
        up = _grouped_call(_gmm_kernel, sched, [xb], w_up, name="moe_up")
        yb = _grouped_call(_gmm_down_kernel, sched, [gate, up], w_down, name="moe_down",
                           out_into=yb, block_offset=q * bq)
    dest2 = dest.reshape(t, TOP_K)
    y = h
    for k in range(TOP_K):
        y = y + yb[dest2[:, k]] * top_w[:, k:k + 1]
    return y


def kernel(x_prompt, x_sample, cache_ckv, cache_krope, state_ssm, state_conv, page_table, norm_attn, w_in, norm_ckv, w_uk, w_uv, q_gain_nope, q_gain_rope, k_gain_nope, k_gain_rope, conv_w, gdn_a_log, gdn_dt_bias, gdn_norm, w_pa, w_pb, w_o, norm_ffn, w_router_group, b_router_group, w_router_expert, b_router_expert, w_gate, w_up, w_down):
    depth = w_in.shape[0]
    bsz, seq, _ = x_prompt.shape
    n_s, s_new, _ = x_sample.shape
    assert s_new == 1
    n_p = bsz * seq
    n_all = n_p + n_s
    n_past = page_table.shape[1] * PAGE_SIZE
    cos_p, sin_p = _rope_tables(jnp.tile(jnp.arange(seq, dtype=F32), bsz))
    cos_s, sin_s = _rope_tables(jnp.full((n_s,), n_past, F32))
    tq = _tile(seq, 512, LANES)
    xp = x_prompt.reshape(n_p, D_MODEL)
    xs = x_sample.reshape(n_s, D_MODEL)
    outs = [[] for _ in range(8)]
    row = lambda v: v.astype(F32)[None, :]
    for l in range(depth):
        w_in_r = _reorder_w_in_pallas(w_in[l])
        w_uk_b = w_uk[l].astype(BF16)
        w_uvt_b = w_uv[l].T.astype(BF16)
        wukt = w_uk_b.T
        w_pa_b, w_pb_b, w_o_b = w_pa[l].astype(BF16), w_pb[l].astype(BF16), w_o[l].astype(BF16)
        w_rt_b = jnp.concatenate([w_router_group[l], w_router_expert[l],
                                  jnp.zeros((D_MODEL, LANES - N_GROUPS - N_EXPERTS), F32)], axis=1).astype(BF16)
        gains = (row(q_gain_nope[l]), jnp.tile(row(q_gain_rope[l]), (1, 2)), row(k_gain_nope[l]),
                 jnp.tile(row(k_gain_rope[l]), (1, 2)), row(norm_ckv[l]))
        alog_l = _lane_vec(gdn_a_log[l], MISC_A)
        dtb_l = _lane_vec(gdn_dt_bias[l], MISC_A)
        proj_p = _in_proj(xp, row(norm_attn[l]), w_in_r, tm=1024)
        proj_s = _in_proj(xs, row(norm_attn[l]), w_in_r, tm=n_s)

        qcat_p, kcat_p, c_p, kr_p, vt_p = _mla_prep(proj_p, cos_p, sin_p, *gains, w_uk_b, w_uvt_b,
                                                    tm=tq, with_vt=True, with_qg=False)
        oa_p = _flash_attention(qcat_p, kcat_p, vt_p, batch=bsz, seq=seq, tq=tq)

        qcat_s, kcat_s, c_s, kr_s, qg_s = _mla_prep(proj_s, cos_s, sin_s, *gains, w_uk_b, w_uvt_b,
                                                    tm=n_s, with_vt=False, with_qg=True)
        qa = _headwise_matmul(qg_s, wukt, heads=MLA_HEADS, b_rows_by_head=True, name="q_absorb")
        o_lat = _paged_attention(page_table, cache_ckv[l], jnp.swapaxes(cache_krope[l], 1, 2), wukt,
                                 qa.astype(BF16).reshape(n_s, MLA_HEADS, KV_LORA),
                                 qcat_s.reshape(n_s, MLA_HEADS, QK_PAD), kcat_s.reshape(n_s, MLA_HEADS, QK_PAD),
                                 c_s[:, None, :])
        oa_s = _headwise_matmul(o_lat.astype(BF16).reshape(n_s, MLA_HEADS * KV_LORA), w_uv[l].astype(BF16),
                                heads=MLA_HEADS, b_rows_by_head=False, name="v_absorb").astype(BF16)

        qkv_p, kt_p, gates_p = _gdn_prep(proj_p, conv_w[l], alog_l, dtb_l, batch=bsz, seq=seq, tm=512)
        o_g, ssm_p = _delta_rule(qkv_p, kt_p, gates_p, batch=bsz, seq=seq)
        conv_p = proj_p.reshape(bsz, seq, D_PROJ)[:, seq - (CONV_W - 1):, SEG_QKV:SEG_QKV + GDN_QKV_W]

        o_s, ssm_s = _gdn_step(proj_s, state_conv[l], state_ssm[l], conv_w[l], alog_l, dtb_l)
        conv_s = jnp.concatenate([state_conv[l][:, 1:], proj_s[:, None, SEG_QKV:SEG_QKV + GDN_QKV_W]], axis=1)

        m_p = _merge(o_g, oa_p, proj_p, row(gdn_norm[l]), w_pa_b, w_pb_b, tm=512)
        m_s = _merge(o_s, oa_s, proj_s, row(gdn_norm[l]), w_pa_b, w_pb_b, tm=n_s)
        b_rt = _lane_vec(jnp.concatenate([b_router_group[l], b_router_expert[l]]), 0)
        h_all, hn_all, route = _out_proj(m_p, xp, m_s, xs, w_o_b, row(norm_ffn[l]), w_rt_b, b_rt, tm=512)
        y_all = _hier_moe(h_all, hn_all, route, w_gate[l], w_up[l], w_down[l])
        xp, xs = y_all[:n_p], y_all[n_p:]

        for lst, val in zip(outs, (c_p.reshape(bsz, seq, KV_LORA), kr_p[:, :QK_ROPE].reshape(bsz, seq, QK_ROPE),
                                   c_s.reshape(n_s, 1, KV_LORA), kr_s[:, :QK_ROPE].reshape(n_s, 1, QK_ROPE),
                                   ssm_p, conv_p, ssm_s, conv_s)):
            lst.append(val)
    return (xp.reshape(bsz, seq, D_MODEL), xs.reshape(n_s, 1, D_MODEL)) + tuple(jnp.stack(o) for o in outs)
```

```python
import functools

import jax
import jax.numpy as jnp
from jax import lax
from jax.experimental import pallas as pl
from jax.experimental.pallas import tpu as pltpu

F32 = jnp.float32
BF16 = jnp.bfloat16

D_MODEL = 2048
PAGE_SIZE = 128
MLA_HEADS = 16
QK_NOPE = 128
QK_ROPE = 64
QK_HEAD = QK_NOPE + QK_ROPE
QK_PAD = 256
V_HEAD = 128
KV_LORA = 512
ROPE_THETA = 10000.0
GDN_HEADS = 16
GDN_DK = 128
GDN_DV = 128
GDN_HK = GDN_HEADS * GDN_DK
GDN_QKV_W = GDN_HEADS * (2 * GDN_DK + GDN_DV)
CONV_W = 4
GDN_CHUNK = 128
N_GROUPS = 8
EXP_PER_GROUP = 8
N_EXPERTS = N_GROUPS * EXP_PER_GROUP
TOP_K = 2
D_EXPERT = 1408
MOE_ROWS = 128
MOE_SPLITS = 4
WEIGHT_DMA_PARTS = 2
EPS = 1e-6
LANES = 128
SUBLANES = 8

V7X_VMEM_LIMIT = 56 * 1024 * 1024

SEG_Z = 0
SEG_GA = SEG_Z + GDN_HEADS * GDN_DV
SEG_GB = SEG_GA + D_MODEL
SEG_QKV = SEG_GB + D_MODEL
SEG_QN = SEG_QKV + GDN_QKV_W
SEG_QR = SEG_QN + MLA_HEADS * QK_NOPE
SEG_CKV = SEG_QR + MLA_HEADS * QK_ROPE
SEG_MISC = SEG_CKV + KV_LORA
PROJ_TN = 1024
D_PROJ = -(-(SEG_MISC + LANES) // PROJ_TN) * PROJ_TN
MISC_A = QK_ROPE
MISC_B = QK_ROPE + GDN_HEADS


def _cparams(sem, vmem=V7X_VMEM_LIMIT):
    return pltpu.CompilerParams(dimension_semantics=sem, vmem_limit_bytes=vmem)


def _tile(dim, cap, align):
    if dim <= cap:
        return dim
    for t in range(cap - cap % align, 0, -align):
        if dim % t == 0:
            return t
    raise ValueError(f"no {align}-aligned tile <= {cap} divides {dim}")


def _nt_dot(a, b):
    return lax.dot_general(a, b, (((1,), (1,)), ((), ())), preferred_element_type=F32)


def _rms_rows(x, gain):
    return x * lax.rsqrt(jnp.mean(x * x, axis=-1, keepdims=True) + EPS) * gain


def _sigmoid(x):
    return 1.0 / (1.0 + jnp.exp(-x))


def _in_proj_kernel(x_ref, g_ref, w_ref, o_ref, u_ref):
    @pl.when(pl.program_id(1) == 0)
    def _():
        u_ref[...] = _rms_rows(x_ref[...], g_ref[...]).astype(BF16)

    o_ref[...] = jnp.dot(u_ref[...], w_ref[...], preferred_element_type=F32)


def _in_proj(x, gain, w, *, tm):
    m, d = x.shape
    n = w.shape[1]
    tm = _tile(m, tm, 16)
    tn = PROJ_TN
    assert n % tn == 0
    return pl.pallas_call(
        _in_proj_kernel,
        grid=(m // tm, n // tn),
        in_specs=[pl.BlockSpec((tm, d), lambda i, j: (i, 0)),
                  pl.BlockSpec((1, d), lambda i, j: (0, 0)),
                  pl.BlockSpec((d, tn), lambda i, j: (0, j))],
        out_specs=pl.BlockSpec((tm, tn), lambda i, j: (i, j)),
        out_shape=jax.ShapeDtypeStruct((m, n), F32),
        scratch_shapes=[pltpu.VMEM((tm, d), BF16)],
        compiler_params=_cparams(("arbitrary", "arbitrary")),
        name="in_proj",
    )(x, gain, w)


def _mm_kernel(a_ref, b_ref, o_ref):
    o_ref[...] = jnp.dot(a_ref[...], b_ref[...], preferred_element_type=F32).astype(o_ref.dtype)


def _headwise_matmul(a, b, *, heads, b_rows_by_head, name):
    m = a.shape[0]
    k = a.shape[1] // heads
    if b_rows_by_head:
        n = b.shape[1]
        b_spec = pl.BlockSpec((k, n), lambda h: (h, 0))
    else:
        n = b.shape[1] // heads
        b_spec = pl.BlockSpec((k, n), lambda h: (0, h))
    return pl.pallas_call(
        _mm_kernel,
        grid=(heads,),
        in_specs=[pl.BlockSpec((m, k), lambda h: (0, h)), b_spec],
        out_specs=pl.BlockSpec((m, n), lambda h: (0, h)),
        out_shape=jax.ShapeDtypeStruct((m, heads * n), F32),
        compiler_params=_cparams(("arbitrary",)),
        name=name,
    )(a, b)


def _swap_halves(y, lane):
    half = QK_ROPE // 2
    return jnp.where(lane % QK_ROPE < half, pltpu.roll(y, LANES - half, axis=1), pltpu.roll(y, half, axis=1))


def _rope_pair(x, gain2, cos4, sin4, lane):
    lo = lane < QK_ROPE
    ss = x * x
    s_lo = jnp.sum(jnp.where(lo, ss, 0.0), axis=-1, keepdims=True)
    s_hi = jnp.sum(jnp.where(lo, 0.0, ss), axis=-1, keepdims=True)
    r = jnp.where(lo, lax.rsqrt(s_lo * (1.0 / QK_ROPE) + EPS), lax.rsqrt(s_hi * (1.0 / QK_ROPE) + EPS))
    y = x * r * gain2
    return y * cos4 + _swap_halves(y, lane) * sin4


def _mla_prep_kernel(qn_ref, qr_ref, ckv_ref, misc_ref, cos_ref, sin_ref, gqn_ref, gqr_ref, gkn_ref, gkr_ref,
                     gc_ref, wuk_ref, *rest, with_vt, with_qg):
    rest = list(rest)
    wuvt_ref = rest.pop(0) if with_vt else None
    qcat_ref, kcat_ref, c_ref, kr_ref = rest[:4]
    rest = rest[4:]
    vt_ref = rest.pop(0) if with_vt else None
    qg_ref = rest.pop(0) if with_qg else None
    rows = qn_ref.shape[0]
    lane = lax.broadcasted_iota(jnp.int32, (rows, LANES), 1)
    lo = lane < QK_ROPE
    cos4 = cos_ref[...]
    sin4 = sin_ref[...]

    c = _rms_rows(ckv_ref[...], gc_ref[...])
    c_ref[...] = c
    cb = c.astype(BF16)
    kr = jnp.where(lo, _rope_pair(misc_ref[...], gkr_ref[...], cos4, sin4, lane), 0.0)
    kr_ref[...] = kr
    krb = kr.astype(BF16)

    gqn = gqn_ref[...]
    gkn = gkn_ref[...]
    for h in range(MLA_HEADS):
        qn = _rms_rows(qn_ref[:, h * QK_NOPE:(h + 1) * QK_NOPE], gqn)
        qcat_ref[:, h * QK_PAD:h * QK_PAD + QK_NOPE] = qn.astype(BF16)
        if with_qg:
            qg_ref[:, h * QK_NOPE:(h + 1) * QK_NOPE] = (qn * gkn).astype(BF16)
    gqr = gqr_ref[...]
    for j in range(MLA_HEADS // 2):
        rot = _rope_pair(qr_ref[:, j * LANES:(j + 1) * LANES], gqr, cos4, sin4, lane)
        even = jnp.where(lo, rot, 0.0)
        odd = jnp.where(lo, pltpu.roll(rot, QK_ROPE, axis=1), 0.0)
        qcat_ref[:, (2 * j) * QK_PAD + QK_NOPE:(2 * j + 1) * QK_PAD] = even.astype(BF16)
        qcat_ref[:, (2 * j + 1) * QK_PAD + QK_NOPE:(2 * j + 2) * QK_PAD] = odd.astype(BF16)

    kn = jnp.dot(cb, wuk_ref[...], preferred_element_type=F32)
    for h in range(MLA_HEADS):
        kh = _rms_rows(kn[:, h * QK_NOPE:(h + 1) * QK_NOPE], gkn)
        kcat_ref[:, h * QK_PAD:h * QK_PAD + QK_NOPE] = kh.astype(BF16)
        kcat_ref[:, h * QK_PAD + QK_NOPE:(h + 1) * QK_PAD] = krb
    if with_vt:
        vt_ref[...] = _nt_dot(wuvt_ref[...], cb).astype(BF16)


def _mla_prep(proj, cos4, sin4, gqn, gqr2, gkn, gkr2, gckv, w_uk_b, w_uvt_b, *, tm, with_vt, with_qg):
    n = proj.shape[0]
    tm = _tile(n, tm, 16)
    nt = n // tm
    hq = MLA_HEADS * QK_NOPE
    row = lambda w, off: pl.BlockSpec((tm, w), lambda i, off=off, w=w: (i, off // w))
    const = lambda a: pl.BlockSpec(a.shape, lambda i: (0,) * a.ndim)
    in_specs = [row(hq, SEG_QN), row(MLA_HEADS * QK_ROPE, SEG_QR), row(KV_LORA, SEG_CKV), row(LANES, SEG_MISC),
                pl.BlockSpec((tm, LANES), lambda i: (i, 0)), pl.BlockSpec((tm, LANES), lambda i: (i, 0)),
                const(gqn), const(gqr2), const(gkn), const(gkr2), const(gckv), const(w_uk_b)]
    args = [proj, proj, proj, proj, cos4, sin4, gqn, gqr2, gkn, gkr2, gckv, w_uk_b]
    if with_vt:
        in_specs.append(const(w_uvt_b))
        args.append(w_uvt_b)
    out_specs = [pl.BlockSpec((tm, MLA_HEADS * QK_PAD), lambda i: (i, 0)),
                 pl.BlockSpec((tm, MLA_HEADS * QK_PAD), lambda i: (i, 0)),
                 pl.BlockSpec((tm, KV_LORA), lambda i: (i, 0)),
                 pl.BlockSpec((tm, LANES), lambda i: (i, 0))]
    out_shape = [jax.ShapeDtypeStruct((n, MLA_HEADS * QK_PAD), BF16),
                 jax.ShapeDtypeStruct((n, MLA_HEADS * QK_PAD), BF16),
                 jax.ShapeDtypeStruct((n, KV_LORA), F32),
                 jax.ShapeDtypeStruct((n, LANES), F32)]
    if with_vt:
        out_specs.append(pl.BlockSpec((None, MLA_HEADS * V_HEAD, tm), lambda i: (i, 0, 0)))
        out_shape.append(jax.ShapeDtypeStruct((nt, MLA_HEADS * V_HEAD, tm), BF16))
    if with_qg:
        out_specs.append(pl.BlockSpec((tm, hq), lambda i: (i, 0)))
        out_shape.append(jax.ShapeDtypeStruct((n, hq), BF16))
    return pl.pallas_call(
        functools.partial(_mla_prep_kernel, with_vt=with_vt, with_qg=with_qg),
        grid=(nt,), in_specs=in_specs, out_specs=out_specs, out_shape=out_shape,
        compiler_params=_cparams(("arbitrary",)),
        name="mla_prep",
    )(*args)


def _flash_kernel(q_ref, k_ref, vt_ref, o_ref, m_ref, l_ref, acc_ref, *, tq, scale):
    qi = pl.program_id(2)
    q = q_ref[...]
    m_ref[...] = jnp.full(m_ref.shape, -jnp.inf, F32)
    l_ref[...] = jnp.zeros(l_ref.shape, F32)
    acc_ref[...] = jnp.zeros(acc_ref.shape, F32)

    c2 = scale * 1.4426950408889634

    def block(ki, nblk, masked):
        start = pl.multiple_of(ki * tq, tq)
        k = k_ref[pl.ds(start, nblk * tq), :]
        s = _nt_dot(k, q)
        if masked:
            key = lax.broadcasted_iota(jnp.int32, s.shape, 0)
            qry = lax.broadcasted_iota(jnp.int32, s.shape, 1)
            s = jnp.where(key <= qry, s, -jnp.inf)
        m_old = m_ref[...]
        m_new = jnp.maximum(m_old, jnp.max(s, axis=0, keepdims=True))
        alpha = jnp.exp2((m_old - m_new) * c2)
        p32 = jnp.exp2((s - m_new) * c2)
        l_ref[...] = l_ref[...] * alpha + jnp.sum(p32, axis=0, keepdims=True)
        p = p32.astype(BF16)
        pv = jnp.dot(vt_ref[ki], p[0:tq], preferred_element_type=F32)
        for j in range(1, nblk):
            pv = pv + jnp.dot(vt_ref[ki + j], p[j * tq:(j + 1) * tq], preferred_element_type=F32)
        acc_ref[...] = acc_ref[...] * alpha + pv
        m_ref[...] = m_new

    def body(kp, carry):
        block(2 * kp, 2, False)
        return carry

    lax.fori_loop(0, qi // 2, body, 0)

    @pl.when(qi % 2 == 1)
    def _():
        block(qi - 1, 1, False)

    block(qi, 1, True)
    o_ref[...] = (acc_ref[...] / l_ref[...]).T.astype(o_ref.dtype)


def _flash_attention(q_cat, k_cat, vt, *, batch, seq, tq):
    nq = seq // tq
    kern = functools.partial(_flash_kernel, tq=tq, scale=QK_HEAD ** -0.5)
    return pl.pallas_call(
        kern,
        grid=(batch, MLA_HEADS, nq),
        in_specs=[pl.BlockSpec((tq, QK_PAD), lambda b, h, i: (b * nq + i, h)),
                  pl.BlockSpec((seq, QK_PAD), lambda b, h, i: (b, h)),
                  pl.BlockSpec((nq, V_HEAD, tq), lambda b, h, i: (b, h, 0))],
        out_specs=pl.BlockSpec((tq, V_HEAD), lambda b, h, i: (b * nq + i, h)),
        out_shape=jax.ShapeDtypeStruct((batch * seq, MLA_HEADS * V_HEAD), BF16),
        scratch_shapes=[pltpu.VMEM((1, tq), F32), pltpu.VMEM((1, tq), F32), pltpu.VMEM((V_HEAD, tq), F32)],
        compiler_params=_cparams(("arbitrary", "arbitrary", "arbitrary")),
        name="mla_prompt_attention",
    )(q_cat, k_cat, vt)


PAGES_PER_STEP = 16
PAGE_RING = 3
PAGE_GROUPS = 2


def _paged_kernel(pt_ref, ckv_hbm, krt_hbm, wukt_ref, qa_ref, qcat_ref, qcat_prev_ref, kcat_prev_ref, cn_prev_ref,
                  o_ref, lhs_ref, cbuf, krbuf, csem, krsem, c_scr, krt_scr, s_scr, m_ref, l_ref, acc_ref,
                  *, scale, steps, n_chunks):
    npg = PAGES_PER_STEP
    t = pl.program_id(0)
    n_hd = MLA_HEADS * QK_NOPE
    span = npg * PAGE_SIZE // PAGE_GROUPS
    prev_chunk = (t + steps - 1) % steps

    def page_copies(chunk, slot, resolve_pages):
        base = jnp.minimum(chunk, n_chunks - 1) * npg
        copies = []
        for i in range(npg):
            page = pt_ref[base + i] if resolve_pages else 0
            copies.append(pltpu.make_async_copy(ckv_hbm.at[page], cbuf.at[slot, i], csem.at[slot]))
            copies.append(pltpu.make_async_copy(krt_hbm.at[page], krbuf.at[slot, i], krsem.at[slot]))
        return copies

    @pl.when(t == 0)
    def _():
        for chunk in range(PAGE_RING - 1):
            for cp in page_copies(chunk, chunk, True):
                cp.start()
        lhs_ref[0:n_hd, :] = wukt_ref[...]
        c_scr[1] = jnp.zeros(c_scr.shape[1:], BF16)
        s_scr[1] = jnp.zeros(s_scr.shape[1:], F32)
        m_ref[...] = jnp.zeros(m_ref.shape, F32)
        l_ref[...] = jnp.zeros(l_ref.shape, F32)
        acc_ref[...] = jnp.zeros(acc_ref.shape, F32)

    @pl.when(t % steps == 0)
    def _():
        lhs_ref[n_hd:n_hd + MLA_HEADS, :] = qa_ref[...]

    ring = t % PAGE_RING

    def step(cur, prv):
        for cp in page_copies(t, ring, False):
            cp.wait()
        first = prev_chunk == 0
        s_prev = s_scr[prv]
        m_old = jnp.where(first, -jnp.inf, m_ref[...])
        l_old = jnp.where(first, 0.0, l_ref[...])
        acc_old = jnp.where(first, 0.0, acc_ref[...])
        m_new = jnp.maximum(m_old, jnp.max(s_prev, axis=-1, keepdims=True))
        alpha = jnp.exp(m_old - m_new)
        pr = jnp.exp(s_prev - m_new)
        l_ref[...] = l_old * alpha + jnp.sum(pr, axis=-1, keepdims=True)
        acc_ref[...] = acc_old * alpha + jnp.dot(pr.astype(BF16), c_scr[prv], preferred_element_type=F32)
        m_ref[...] = m_new
        for i in range(npg):
            c_scr[cur, i * PAGE_SIZE:(i + 1) * PAGE_SIZE, :] = cbuf[ring, i].astype(BF16)
            krt_scr[:, i * PAGE_SIZE:(i + 1) * PAGE_SIZE] = krbuf[ring, i].astype(BF16)
        qr = qcat_ref[:, QK_NOPE:QK_NOPE + QK_ROPE]
        for g in range(PAGE_GROUPS):
            c = c_scr[cur, g * span:(g + 1) * span, :]
            res = _nt_dot(lhs_ref[...], c)
            kt = res[0:n_hd, :].reshape(MLA_HEADS, QK_NOPE, span)
            inv = lax.rsqrt(jnp.sum(kt * kt, axis=1) * (1.0 / QK_NOPE) + EPS)
            s_rope = jnp.dot(qr, krt_scr[:, g * span:(g + 1) * span], preferred_element_type=F32)
            s_scr[cur, :, g * span:(g + 1) * span] = (res[n_hd:n_hd + MLA_HEADS, :] * inv + s_rope) * scale
        for cp in page_copies(t + PAGE_RING - 1, (t + PAGE_RING - 1) % PAGE_RING, True):
            cp.start()

    @pl.when(t % 2 == 0)
    def _():
        step(0, 1)

    @pl.when(t % 2 == 1)
    def _():
        step(1, 0)

    @pl.when((prev_chunk == steps - 1) & (t > 0))
    def _():
        s_new = jnp.sum(qcat_prev_ref[...].astype(F32) * kcat_prev_ref[...].astype(F32), axis=-1,
                        keepdims=True) * scale
        m_old = m_ref[...]
        m_fin = jnp.maximum(m_old, s_new)
        a2 = jnp.exp(m_old - m_fin)
        p_new = jnp.exp(s_new - m_fin)
        l_fin = l_ref[...] * a2 + p_new
        cn = cn_prev_ref[...].astype(BF16).astype(F32)
        o_ref[...] = (acc_ref[...] * a2 + p_new * cn) / l_fin

    @pl.when(t == n_chunks)
    def _():
        for ahead in range(1, PAGE_RING):
            for cp in page_copies(t + ahead, (t + ahead) % PAGE_RING, False):
                cp.wait()


def _paged_attention(page_table, ckv_pool, krt_pool, wukt, qa, q_cat, k_cat, cn):
    n_s, n_pages = page_table.shape
    npg = PAGES_PER_STEP
    assert n_pages % npg == 0
    steps = n_pages // npg
    n_chunks = n_s * steps
    span = npg * PAGE_SIZE
    pt_flat = page_table.reshape(n_s * n_pages)

    def cur_sample(shape):
        return pl.BlockSpec((None,) + shape, lambda t, pt: (jnp.minimum(t // steps, n_s - 1), 0, 0))

    def prev_sample(shape):
        return pl.BlockSpec((None,) + shape, lambda t, pt: (jnp.maximum(t - 1, 0) // steps, 0, 0))

    in_specs = [pl.BlockSpec(memory_space=pl.ANY), pl.BlockSpec(memory_space=pl.ANY),
                pl.BlockSpec(wukt.shape, lambda t, pt: (0, 0)),
                cur_sample((MLA_HEADS, KV_LORA)), cur_sample((MLA_HEADS, QK_PAD)),
                prev_sample((MLA_HEADS, QK_PAD)), prev_sample((MLA_HEADS, QK_PAD)), prev_sample((1, KV_LORA))]
    grid_spec = pltpu.PrefetchScalarGridSpec(
        num_scalar_prefetch=1,
        grid=(n_chunks + 1,),
        in_specs=in_specs,
        out_specs=prev_sample((MLA_HEADS, KV_LORA)),
        scratch_shapes=[pltpu.VMEM((MLA_HEADS * QK_NOPE + MLA_HEADS, KV_LORA), BF16),
                        pltpu.VMEM((PAGE_RING, npg, PAGE_SIZE, KV_LORA), F32),
                        pltpu.VMEM((PAGE_RING, npg, QK_ROPE, PAGE_SIZE), F32),
                        pltpu.SemaphoreType.DMA((PAGE_RING,)), pltpu.SemaphoreType.DMA((PAGE_RING,)),
                        pltpu.VMEM((2, span, KV_LORA), BF16), pltpu.VMEM((QK_ROPE, span), BF16),
                        pltpu.VMEM((2, MLA_HEADS, span), F32),
                        pltpu.VMEM((MLA_HEADS, 1), F32), pltpu.VMEM((MLA_HEADS, 1), F32),
                        pltpu.VMEM((MLA_HEADS, KV_LORA), F32)])
    return pl.pallas_call(
        functools.partial(_paged_kernel, scale=QK_HEAD ** -0.5, steps=steps, n_chunks=n_chunks),
        grid_spec=grid_spec,
        out_shape=jax.ShapeDtypeStruct((n_s, MLA_HEADS, KV_LORA), F32),
        compiler_params=_cparams(("arbitrary",)),
        name="mla_sample_attention",
    )(pt_flat, ckv_pool, krt_pool, wukt, qa, q_cat, q_cat, k_cat, cn)


def _gdn_prep_kernel(x_ref, halo_ref, misc_ref, cw_ref, alog_ref, dtb_ref, qkv_ref, kt_ref, gates_ref, xs_ref,
                     *, tiles_per_seq):
    i = pl.program_id(0)
    s = pl.program_id(1)
    tm = x_ref.shape[0]
    first = (i % tiles_per_seq) == 0

    @pl.when(first)
    def _():
        xs_ref[0:SUBLANES, :] = jnp.zeros((SUBLANES, xs_ref.shape[1]), F32)

    @pl.when(jnp.logical_not(first))
    def _():
        xs_ref[0:SUBLANES, :] = halo_ref[...]

    xs_ref[SUBLANES:SUBLANES + tm, :] = x_ref[...]
    acc = xs_ref[pl.ds(SUBLANES - (CONV_W - 1), tm), :] * cw_ref[0:1, :]
    for w in range(1, CONV_W):
        acc = acc + xs_ref[pl.ds(SUBLANES - (CONV_W - 1) + w, tm), :] * cw_ref[w:w + 1, :]
    y = acc * _sigmoid(acc)

    @pl.when(s == 2)
    def _():
        qkv_ref[...] = y

    @pl.when(s < 2)
    def _():
        scale = jnp.where(s == 0, GDN_DK ** -0.5, 1.0).astype(F32)
        for h in range(GDN_HEADS):
            yh = y[:, h * GDN_DK:(h + 1) * GDN_DK]
            yn = yh * (lax.rsqrt(jnp.sum(yh * yh, axis=-1, keepdims=True) + EPS) * scale)
            qkv_ref[:, h * GDN_DK:(h + 1) * GDN_DK] = yn

    @pl.when(s == 1)
    def _():
        kt_ref[...] = qkv_ref[...].T

    @pl.when(s == 0)
    def _():
        x = misc_ref[...] + dtb_ref[...]
        softplus = jnp.maximum(x, 0.0) + jnp.log(1.0 + jnp.exp(-jnp.abs(x)))
        lane = lax.broadcasted_iota(jnp.int32, x.shape, 1)
        is_beta = (lane >= MISC_B) & (lane < MISC_B + GDN_HEADS)
        gates_ref[...] = jnp.where(is_beta, _sigmoid(misc_ref[...]), -jnp.exp(alog_ref[...]) * softplus)


def _gdn_prep(proj, conv_w, alog_l, dtb_l, *, batch, seq, tm):
    n = batch * seq
    tm = _tile(seq, tm, LANES)
    tps = seq // tm
    sec0 = SEG_QKV // GDN_HK
    halo_blocks = tm // SUBLANES
    return pl.pallas_call(
        functools.partial(_gdn_prep_kernel, tiles_per_seq=tps),
        grid=(n // tm, 3),
        in_specs=[pl.BlockSpec((tm, GDN_HK), lambda i, s: (i, sec0 + s)),
                  pl.BlockSpec((SUBLANES, GDN_HK), lambda i, s: (jnp.maximum(i * halo_blocks - 1, 0), sec0 + s)),
                  pl.BlockSpec((tm, LANES), lambda i, s: (i, SEG_MISC // LANES)),
                  pl.BlockSpec((CONV_W, GDN_HK), lambda i, s: (0, s)),
                  pl.BlockSpec((1, LANES), lambda i, s: (0, 0)),
                  pl.BlockSpec((1, LANES), lambda i, s: (0, 0))],
        out_specs=[pl.BlockSpec((tm, GDN_HK), lambda i, s: (i, s)),
                   pl.BlockSpec((None, GDN_HK, tm), lambda i, s: (i // tps, 0, i % tps)),
                   pl.BlockSpec((tm, LANES), lambda i, s: (i, 0))],
        out_shape=[jax.ShapeDtypeStruct((n, 3 * GDN_HK), F32),
                   jax.ShapeDtypeStruct((batch, GDN_HK, seq), F32),
                   jax.ShapeDtypeStruct((n, LANES), F32)],
        scratch_shapes=[pltpu.VMEM((tm + SUBLANES, GDN_HK), F32)],
        compiler_params=_cparams(("arbitrary", "arbitrary")),
        name="gdn_prep",
    )(proj, proj, proj, conv_w, alog_l, dtb_l)


def _split3(x):
    hi = x.astype(BF16)
    r1 = x - hi.astype(F32)
    mid = r1.astype(BF16)
    lo = (r1 - mid.astype(F32)).astype(BF16)
    return hi, mid, lo


def _bdot(a, b):
    return jnp.dot(a.astype(BF16), b.astype(BF16), preferred_element_type=F32)


def _delta_kernel(q_ref, k_ref, v_ref, kt_ref, gates_ref, o_ref, s_out_ref, s_ref):
    c = GDN_CHUNK
    n = pl.program_id(1)
    heads = range(GDN_HEADS)

    @pl.when(n == 0)
    def _():
        s_ref[...] = jnp.zeros(s_ref.shape, F32)

    row = lax.broadcasted_iota(jnp.int32, (c, c), 0)
    col = lax.broadcasted_iota(jnp.int32, (c, c), 1)
    incl = row >= col
    strict = row > col
    eye = (row == col).astype(F32)
    tril = incl.astype(BF16)
    triu = (row <= col).astype(BF16)
    gates = gates_ref[...]
    gates_t = gates.T
    gc_all = sum(jnp.dot(tril, part, preferred_element_type=F32) for part in _split3(gates))
    gct_all = sum(jnp.dot(part, triu, preferred_element_type=F32) for part in _split3(gates_t))

    sl = [slice(h * GDN_DK, (h + 1) * GDN_DK) for h in heads]
    gcol = [gc_all[:, MISC_A + h:MISC_A + h + 1] for h in heads]
    grow = [gct_all[MISC_A + h:MISC_A + h + 1, :] for h in heads]
    glast = [gct_all[MISC_A + h:MISC_A + h + 1, c - 1:c] for h in heads]
    bcol = [gates[:, MISC_B + h:MISC_B + h + 1] for h in heads]
    eg = [jnp.exp(g) for g in gcol]
    e = [jnp.exp(jnp.where(incl, gcol[h] - grow[h], 0.0)) for h in heads]
    kb = [k_ref[:, sl[h]] * bcol[h] for h in heads]
    ktb = [kt_ref[sl[h], :].astype(BF16) for h in heads]
    x = [-(jnp.dot(kb[h].astype(BF16), ktb[h], preferred_element_type=F32) * jnp.where(strict, e[h], 0.0))
         for h in heads]
    t = [eye + x[h] for h in heads]
    for _ in range(6):
        x = [_bdot(x[h], x[h]) for h in heads]
        t = [t[h] + _bdot(t[h], x[h]) for h in heads]
    rhs = [jnp.concatenate([v_ref[:, sl[h]] * bcol[h], kb[h] * eg[h]], axis=1) for h in heads]
    sol = [_bdot(t[h], rhs[h]) for h in heads]
    attn = [jnp.dot(q_ref[:, sl[h]].astype(BF16), ktb[h], preferred_element_type=F32) * jnp.where(incl, e[h], 0.0)
            for h in heads]
    s_old = [s_ref[h] for h in heads]
    lhs = [jnp.concatenate([sol[h][:, GDN_DV:], q_ref[:, sl[h]] * eg[h]], axis=0) for h in heads]
    prod = [_bdot(lhs[h], s_old[h]) for h in heads]
    v_new = [sol[h][:, :GDN_DV] - prod[h][:c] for h in heads]
    for h in heads:
        o_ref[:, sl[h]] = prod[h][c:] + _bdot(attn[h], v_new[h])
    for h in heads:
        kdt = kt_ref[sl[h], :] * jnp.exp(glast[h] - grow[h])
        s_ref[h] = s_old[h] * jnp.exp(glast[h]) + _bdot(kdt, v_new[h])

    @pl.when(n == pl.num_programs(1) - 1)
    def _():
        s_out_ref[...] = s_ref[...]


def _delta_rule(qkv, kt, gates, *, batch, seq):
    c = GDN_CHUNK
    assert seq % c == 0
    nc = seq // c
    sec = lambda s: pl.BlockSpec((c, GDN_HK), lambda b, n, s=s: (b * nc + n, s))
    return pl.pallas_call(
        _delta_kernel,
        grid=(batch, nc),
        in_specs=[sec(0), sec(1), sec(2),
                  pl.BlockSpec((None, GDN_HK, c), lambda b, n: (b, 0, n)),
                  pl.BlockSpec((c, LANES), lambda b, n: (b * nc + n, 0))],
        out_specs=[pl.BlockSpec((c, GDN_HK), lambda b, n: (b * nc + n, 0)),
                   pl.BlockSpec((None, GDN_HEADS, GDN_DK, GDN_DV), lambda b, n: (b, 0, 0, 0))],
        out_shape=[jax.ShapeDtypeStruct((batch * seq, GDN_HK), F32),
                   jax.ShapeDtypeStruct((batch, GDN_HEADS, GDN_DK, GDN_DV), F32)],
        scratch_shapes=[pltpu.VMEM((GDN_HEADS, GDN_DK, GDN_DV), F32)],
        compiler_params=_cparams(("arbitrary", "arbitrary")),
        name="gdn_delta_rule",
    )(qkv, qkv, qkv, kt, gates)


GDN_STEP_ROWS = SUBLANES


def _gdn_step_kernel(x_ref, buf_ref, misc_ref, cw_ref, alog_ref, dtb_ref, s_ref, o_ref, s_out_ref):
    nb = x_ref.shape[0]
    acc = x_ref[...] * cw_ref[CONV_W - 1:CONV_W, :]
    for w in range(CONV_W - 1):
        acc = acc + buf_ref[:, w, :] * cw_ref[w:w + 1, :]
    y = acc * _sigmoid(acc)
    x = misc_ref[...] + dtb_ref[...]
    softplus = jnp.maximum(x, 0.0) + jnp.log(1.0 + jnp.exp(-jnp.abs(x)))
    eg_all = jnp.exp(-jnp.exp(alog_ref[...]) * softplus)
    beta_all = _sigmoid(misc_ref[...])
    for h in range(GDN_HEADS):
        qs = slice(h * GDN_DK, (h + 1) * GDN_DK)
        ks = slice(GDN_HK + h * GDN_DK, GDN_HK + (h + 1) * GDN_DK)
        vs = slice(2 * GDN_HK + h * GDN_DV, 2 * GDN_HK + (h + 1) * GDN_DV)
        q = y[:, qs]
        k = y[:, ks]
        q = q * (lax.rsqrt(jnp.sum(q * q, axis=-1, keepdims=True) + EPS) * (GDN_DK ** -0.5))
        k = k * lax.rsqrt(jnp.sum(k * k, axis=-1, keepdims=True) + EPS)
        eg = eg_all[:, MISC_A + h:MISC_A + h + 1]
        beta = beta_all[:, MISC_B + h:MISC_B + h + 1]
        vb = y[:, vs] * beta
        kcd = k * (beta * eg)
        qd = q * eg
        qk = jnp.sum(q * k, axis=-1, keepdims=True)
        kt = k.T
        for b in range(nb):
            s_old = s_ref[b, h]
            lhs = jnp.concatenate([kcd[b:b + 1], qd[b:b + 1], jnp.zeros((SUBLANES - 2, GDN_DK), F32)], axis=0)
            prod = jnp.dot(lhs.astype(BF16), s_old.astype(BF16), preferred_element_type=F32)
            v_new = vb[b:b + 1] - prod[0:1]
            o_ref[b:b + 1, qs] = prod[1:2] + qk[b:b + 1] * v_new
            s_out_ref[b, h] = s_old * eg[b:b + 1] + kt[:, b:b + 1] * v_new


def _gdn_step(proj, state_conv, state_ssm, conv_w, alog_l, dtb_l):
    n_s = proj.shape[0]
    nb = min(GDN_STEP_ROWS, n_s)
    assert n_s % nb == 0
    return pl.pallas_call(
        _gdn_step_kernel,
        grid=(n_s // nb,),
        in_specs=[pl.BlockSpec((nb, GDN_QKV_W), lambda i: (i, SEG_QKV // GDN_QKV_W)),
                  pl.BlockSpec((nb, CONV_W - 1, GDN_QKV_W), lambda i: (i, 0, 0)),
                  pl.BlockSpec((nb, LANES), lambda i: (i, SEG_MISC // LANES)),
                  pl.BlockSpec((CONV_W, GDN_QKV_W), lambda i: (0, 0)),
                  pl.BlockSpec((1, LANES), lambda i: (0, 0)),
                  pl.BlockSpec((1, LANES), lambda i: (0, 0)),
                  pl.BlockSpec((nb, GDN_HEADS, GDN_DK, GDN_DV), lambda i: (i, 0, 0, 0))],
        out_specs=[pl.BlockSpec((nb, GDN_HK), lambda i: (i, 0)),
                   pl.BlockSpec((nb, GDN_HEADS, GDN_DK, GDN_DV), lambda i: (i, 0, 0, 0))],
        out_shape=[jax.ShapeDtypeStruct((n_s, GDN_HK), F32),
                   jax.ShapeDtypeStruct(state_ssm.shape, F32)],
        compiler_params=_cparams(("arbitrary",)),
        name="gdn_step",
    )(proj, state_conv, proj, conv_w, alog_l, dtb_l, state_ssm)


def _merge_kernel(og_ref, z_ref, oa_ref, ga_ref, gb_ref, gn_ref, wpa_ref, wpb_ref, m_ref, ob_ref):
    @pl.when(pl.program_id(1) == 0)
    def _():
        gn = gn_ref[...]
        for h in range(GDN_HEADS):
            sl = slice(h * GDN_DV, (h + 1) * GDN_DV)
            z = z_ref[:, sl]
            ob_ref[:, sl] = (_rms_rows(og_ref[:, sl], gn) * (z * _sigmoid(z))).astype(BF16)

    pa = jnp.dot(oa_ref[...], wpa_ref[...], preferred_element_type=F32)
    pb = jnp.dot(ob_ref[...], wpb_ref[...], preferred_element_type=F32)
    m_ref[...] = (_sigmoid(ga_ref[...]) * pa + _sigmoid(gb_ref[...]) * pb).astype(BF16)


def _merge(o_g, oa, proj, gdn_norm, w_pa_b, w_pb_b, *, tm):
    n = o_g.shape[0]
    tm = _tile(n, tm, 16)
    tn = 1024
    nj = D_MODEL // tn
    return pl.pallas_call(
        _merge_kernel,
        grid=(n // tm, nj),
        in_specs=[pl.BlockSpec((tm, GDN_HK), lambda i, j: (i, 0)),
                  pl.BlockSpec((tm, GDN_HK), lambda i, j: (i, SEG_Z // GDN_HK)),
                  pl.BlockSpec((tm, MLA_HEADS * V_HEAD), lambda i, j: (i, 0)),
                  pl.BlockSpec((tm, tn), lambda i, j: (i, SEG_GA // tn + j)),
                  pl.BlockSpec((tm, tn), lambda i, j: (i, SEG_GB // tn + j)),
                  pl.BlockSpec((1, GDN_DV), lambda i, j: (0, 0)),
                  pl.BlockSpec((MLA_HEADS * V_HEAD, tn), lambda i, j: (0, j)),
                  pl.BlockSpec((GDN_HK, tn), lambda i, j: (0, j))],
        out_specs=pl.BlockSpec((tm, tn), lambda i, j: (i, j)),
        out_shape=jax.ShapeDtypeStruct((n, D_MODEL), BF16),
        scratch_shapes=[pltpu.VMEM((tm, GDN_HK), BF16)],
        compiler_params=_cparams(("arbitrary", "arbitrary")),
        name="gated_merge",
    )(o_g, proj, oa, proj, proj, gdn_norm, w_pa_b, w_pb_b)


def _pack_bf16_pairs(x):
    w = x.shape[1] // 2
    bits = lax.bitcast_convert_type(x.astype(F32), jnp.uint32)
    return (bits[:, :w] >> 16) | bits[:, w:]


def _unpack_bf16_pairs(p):
    lo = lax.bitcast_convert_type(p << 16, F32)
    hi = lax.bitcast_convert_type(p & jnp.uint32(0xFFFF0000), F32)
    return jnp.concatenate([lo, hi], axis=1).astype(BF16)


ROUTE_E1, ROUTE_E2, ROUTE_W1, ROUTE_W2 = 0, 1, 2, 3


def _first_lane_of_max(x, lane):
    m = jnp.max(x, axis=-1, keepdims=True)
    return m, jnp.min(jnp.where(x == m, lane, LANES), axis=-1, keepdims=True)


def _route(logits, lane):
    neg = -jnp.inf
    is_g = lane < N_GROUPS
    gmax, grp = _first_lane_of_max(jnp.where(is_g, logits, neg), lane)
    p_grp = 1.0 / jnp.sum(jnp.where(is_g, jnp.exp(logits - gmax), 0.0), axis=-1, keepdims=True)
    lo = N_GROUPS + grp * EXP_PER_GROUP
    in_grp = (lane >= lo) & (lane < lo + EXP_PER_GROUP)
    emax = jnp.max(jnp.where(in_grp, logits, neg), axis=-1, keepdims=True)
    ee = jnp.where(in_grp, jnp.exp(logits - emax), 0.0)
    soft = jnp.where(in_grp, ee / jnp.sum(ee, axis=-1, keepdims=True), -1.0)
    w1, i1 = _first_lane_of_max(soft, lane)
    w2, i2 = _first_lane_of_max(jnp.where(lane == i1, -1.0, soft), lane)
    wsum = w1 + w2
    tile = jnp.where(lane == ROUTE_E1, (i1 - N_GROUPS).astype(F32), 0.0)
    tile = jnp.where(lane == ROUTE_E2, (i2 - N_GROUPS).astype(F32), tile)
    tile = jnp.where(lane == ROUTE_W1, w1 / wsum * p_grp, tile)
    return jnp.where(lane == ROUTE_W2, w2 / wsum * p_grp, tile)


def _out_proj_kernel(mp_ref, xp_ref, ms_ref, xs_ref, wo_ref, gf_ref, wrt_ref, brt_ref, h_ref, hn_ref, rt_ref,
                     *, np_tiles):
    i = pl.program_id(0)

    def rows(m_ref, x_ref, n):
        h = x_ref[...] + jnp.dot(m_ref[...], wo_ref[...], preferred_element_type=F32)
        h_ref[0:n, :] = h
        hnb = _rms_rows(h, gf_ref[...]).astype(BF16)
        hn_ref[0:n, :] = _pack_bf16_pairs(hnb)
        logits = jnp.dot(hnb, wrt_ref[...], preferred_element_type=F32) + brt_ref[...]
        rt_ref[0:n, :] = _route(logits, lax.broadcasted_iota(jnp.int32, logits.shape, 1))

    @pl.when(i < np_tiles)
    def _():
        rows(mp_ref, xp_ref, mp_ref.shape[0])

    @pl.when(i == np_tiles)
    def _():
        rows(ms_ref, xs_ref, ms_ref.shape[0])


def _out_proj(m_p, x_p, m_s, x_s, w_o_b, norm_ffn, w_rt_b, b_rt, *, tm):
    n_p, n_s = m_p.shape[0], m_s.shape[0]
    tm = _tile(n_p, tm, 16)
    np_tiles = n_p // tm
    assert n_s <= tm
    const = lambda a: pl.BlockSpec(a.shape, lambda i: (0,) * a.ndim)
    p_spec = pl.BlockSpec((tm, D_MODEL), lambda i: (jnp.minimum(i, np_tiles - 1), 0))
    out = lambda w: pl.BlockSpec((tm, w), lambda i: (i, 0))
    return pl.pallas_call(
        functools.partial(_out_proj_kernel, np_tiles=np_tiles),
        grid=(np_tiles + 1,),
        in_specs=[p_spec, p_spec, const(m_s), const(x_s), const(w_o_b), const(norm_ffn), const(w_rt_b),
                  const(b_rt)],
        out_specs=[out(D_MODEL), out(D_MODEL // 2), out(LANES)],
        out_shape=[jax.ShapeDtypeStruct((n_p + n_s, D_MODEL), F32),
                   jax.ShapeDtypeStruct((n_p + n_s, D_MODEL // 2), jnp.uint32),
                   jax.ShapeDtypeStruct((n_p + n_s, LANES), F32)],
        compiler_params=_cparams(("arbitrary",)),
        name="out_proj_router",
    )(m_p, x_p, m_s, x_s, w_o_b, norm_ffn, w_rt_b, b_rt)


def _expert_weights(sched_ref, w_hbm, wbuf, wb_ref, sem):
    i = pl.program_id(0)

    part = wbuf.shape[1] // WEIGHT_DMA_PARTS

    def fetch(expert, slot):
        return [pltpu.make_async_copy(w_hbm.at[expert, pl.ds(p * part, part)],
                                      wbuf.at[slot, pl.ds(p * part, part)], sem.at[slot, p])
                for p in range(WEIGHT_DMA_PARTS)]

    def start(expert, slot):
        for p, cp in enumerate(fetch(expert, slot)):
            cp.start(priority=p)

    @pl.when(i == 0)
    def _():
        start(sched_ref[0, 0], 0)

    @pl.when(sched_ref[2, i] != 0)
    def _():
        slot = sched_ref[3, i]
        for cp in fetch(sched_ref[0, i], slot):
            cp.wait()

        @pl.when(sched_ref[4, i] >= 0)
        def _():
            start(sched_ref[4, i], 1 - slot)

        wb_ref[...] = wbuf[slot].astype(BF16)


def _gmm_kernel(sched_ref, x_ref, w_hbm, o_ref, wbuf, wb_ref, sem):
    i = pl.program_id(0)
    _expert_weights(sched_ref, w_hbm, wbuf, wb_ref, sem)

    @pl.when(sched_ref[1, i] != 0)
    def _():
        o_ref[...] = jnp.dot(_unpack_bf16_pairs(x_ref[...]), wb_ref[...], preferred_element_type=F32)

    @pl.when(sched_ref[1, i] == 0)
    def _():
        o_ref[...] = jnp.zeros(o_ref.shape, o_ref.dtype)


def _gmm_down_kernel(sched_ref, g_ref, u_ref, w_hbm, acc_hbm, o_ref, wbuf, wb_ref, sem):
    del acc_hbm
    i = pl.program_id(0)
    _expert_weights(sched_ref, w_hbm, wbuf, wb_ref, sem)

    @pl.when(sched_ref[1, i] != 0)
    def _():
        gate = g_ref[...]
        act = gate * _sigmoid(gate) * u_ref[...]
        o_ref[...] = jnp.dot(act.astype(BF16), wb_ref[...], preferred_element_type=F32)

    @pl.when(sched_ref[1, i] == 0)
    def _():
        o_ref[...] = jnp.zeros(o_ref.shape, o_ref.dtype)


def _expert_schedule(bexp, used):
    nb = bexp.shape[0]
    first = jnp.concatenate([jnp.ones((1,), jnp.int32), (bexp[1:] != bexp[:-1]).astype(jnp.int32)])
    run = jnp.cumsum(first) - 1
    run_expert = jnp.zeros((nb,), jnp.int32).at[run].set(bexp)
    nxt = jnp.where(run + 1 <= run[-1], run_expert[jnp.minimum(run + 1, nb - 1)], -1)
    return jnp.stack([bexp, used, first, run % 2, nxt]).astype(jnp.int32)


def _grouped_call(kern, sched, row_inputs, w, *, name, out_into=None, block_offset=0):
    rows = row_inputs[0].shape[0]
    k, n = w.shape[1], w.shape[2]
    nb = rows // MOE_ROWS
    in_specs = ([pl.BlockSpec((MOE_ROWS, a.shape[1]), lambda i, sc: (i, 0)) for a in row_inputs]
                + [pl.BlockSpec(memory_space=pl.ANY)])
    args = [sched, *row_inputs, w]
    aliases = {}
    if out_into is not None:
        in_specs.append(pl.BlockSpec(memory_space=pl.ANY))
        aliases = {len(args): 0}
        args.append(out_into)
        rows = out_into.shape[0]
    grid_spec = pltpu.PrefetchScalarGridSpec(
        num_scalar_prefetch=1,
        grid=(nb,),
        in_specs=in_specs,
        out_specs=pl.BlockSpec((MOE_ROWS, n), lambda i, sc: (i + block_offset, 0)),
        scratch_shapes=[pltpu.VMEM((2, k, n), F32), pltpu.VMEM((k, n), BF16),
                        pltpu.SemaphoreType.DMA((2, WEIGHT_DMA_PARTS))])
    return pl.pallas_call(
        kern, grid_spec=grid_spec,
        out_shape=jax.ShapeDtypeStruct((rows, n), F32),
        input_output_aliases=aliases,
        compiler_params=_cparams(("arbitrary",)),
        name=name,
    )(*args)


def _l2n(x):
    return x * lax.rsqrt(jnp.sum(x * x, axis=-1, keepdims=True) + EPS)


def _rms(x, g):
    return x * lax.rsqrt(jnp.mean(x * x, axis=-1, keepdims=True) + EPS) * g


def _rope_tables(pos):
    inv = ROPE_THETA ** (-jnp.arange(0, QK_ROPE, 2, dtype=F32) / QK_ROPE)
    ang = pos[:, None] * inv[None, :]
    cos, sin = jnp.cos(ang), jnp.sin(ang)
    return jnp.concatenate([cos, cos, cos, cos], axis=1), jnp.concatenate([-sin, sin, -sin, sin], axis=1)


def _w_in_segments():
    segs = []
    for h in range(MLA_HEADS):
        segs.append((h * QK_HEAD, QK_NOPE, SEG_QN + h * QK_NOPE))
        segs.append((h * QK_HEAD + QK_NOPE, QK_ROPE, SEG_QR + h * QK_ROPE))
    o = MLA_HEADS * QK_HEAD
    segs.append((o, KV_LORA, SEG_CKV)); o += KV_LORA
    segs.append((o, QK_ROPE, SEG_MISC)); o += QK_ROPE
    segs.append((o, GDN_QKV_W, SEG_QKV)); o += GDN_QKV_W
    segs.append((o, GDN_HEADS, SEG_MISC + MISC_A)); o += GDN_HEADS
    segs.append((o, GDN_HEADS, SEG_MISC + MISC_B)); o += GDN_HEADS
    segs.append((o, GDN_HEADS * GDN_DV, SEG_Z)); o += GDN_HEADS * GDN_DV
    segs.append((o, D_MODEL, SEG_GA)); o += D_MODEL
    segs.append((o, D_MODEL, SEG_GB))
    return segs


def _reorder_kernel(w_ref, o_ref):
    for src, width, dst in _w_in_segments():
        o_ref[:, dst:dst + width] = w_ref[:, src:src + width].astype(BF16)
    tail = SEG_MISC + MISC_B + GDN_HEADS
    o_ref[:, tail:D_PROJ] = jnp.zeros((o_ref.shape[0], D_PROJ - tail), BF16)


def _reorder_w_in_pallas(w_in):
    d, d_in = w_in.shape
    tr = 128
    return pl.pallas_call(
        _reorder_kernel,
        grid=(d // tr,),
        in_specs=[pl.BlockSpec((tr, d_in), lambda i: (i, 0))],
        out_specs=pl.BlockSpec((tr, D_PROJ), lambda i: (i, 0)),
        out_shape=jax.ShapeDtypeStruct((d, D_PROJ), BF16),
        compiler_params=_cparams(("arbitrary",)),
        name="w_in_reorder",
    )(w_in)


def _reorder_w_in(w_in):
    o = 0
    wq = w_in[:, o:o + MLA_HEADS * QK_HEAD].reshape(D_MODEL, MLA_HEADS, QK_HEAD); o += MLA_HEADS * QK_HEAD
    wckv = w_in[:, o:o + KV_LORA]; o += KV_LORA
    wkr = w_in[:, o:o + QK_ROPE]; o += QK_ROPE
    wqkv = w_in[:, o:o + GDN_QKV_W]; o += GDN_QKV_W
    wa = w_in[:, o:o + GDN_HEADS]; o += GDN_HEADS
    wb = w_in[:, o:o + GDN_HEADS]; o += GDN_HEADS
    wz = w_in[:, o:o + GDN_HEADS * GDN_DV]; o += GDN_HEADS * GDN_DV
    wga = w_in[:, o:o + D_MODEL]; o += D_MODEL
    wgb = w_in[:, o:o + D_MODEL]
    pad = jnp.zeros((D_MODEL, D_PROJ - SEG_MISC - QK_ROPE - 2 * GDN_HEADS), w_in.dtype)
    return jnp.concatenate([
        wz, wga, wgb, wqkv, wq[:, :, :QK_NOPE].reshape(D_MODEL, -1), wq[:, :, QK_NOPE:].reshape(D_MODEL, -1),
        wckv, wkr, wa, wb, pad], axis=1).astype(BF16)


def _lane_vec(v, offset):
    return jnp.zeros((1, LANES), F32).at[0, offset:offset + v.shape[0]].set(v.astype(F32))


def _hier_moe(h, hn, route, w_gate, w_up, w_down, *, n_first):
    t = h.shape[0]
    expert = route[:, ROUTE_E1:ROUTE_E2 + 1].astype(jnp.int32)
    top_w = route[:, ROUTE_W1:ROUTE_W2 + 1]
    n_assign = t * TOP_K
    e_flat = expert.reshape(n_assign)
    onehot = (e_flat[:, None] == jnp.arange(N_EXPERTS, dtype=jnp.int32)[None, :]).astype(jnp.int32)
    csum = jnp.cumsum(onehot, axis=0)
    rank = jnp.take_along_axis(csum, e_flat[:, None], axis=1)[:, 0] - 1
    counts = csum[-1]
    pcounts = (counts + MOE_ROWS - 1) // MOE_ROWS * MOE_ROWS
    pends = jnp.cumsum(pcounts)
    pstarts = pends - pcounts
    dest = pstarts[e_flat] + rank
    n_blocks = -(-(-(-n_assign // MOE_ROWS) + N_EXPERTS) // MOE_SPLITS) * MOE_SPLITS
    n_rows = n_blocks * MOE_ROWS
    tok = jnp.arange(n_assign, dtype=jnp.int32) // TOP_K
    src = jnp.zeros((n_rows,), jnp.int32).at[dest].set(tok)
    bstart = jnp.arange(n_blocks, dtype=jnp.int32) * MOE_ROWS
    bexp = jnp.minimum(jnp.searchsorted(pends, bstart, side='right'), N_EXPERTS - 1).astype(jnp.int32)
    used = (bstart < pends[-1]).astype(jnp.int32)
    bq = n_blocks // MOE_SPLITS
    yb = jnp.zeros((n_rows, D_MODEL), F32)
    for q in range(MOE_SPLITS):
        xb = hn[src[q * bq * MOE_ROWS:(q + 1) * bq * MOE_ROWS]]
        sched = _expert_schedule(bexp[q * bq:(q + 1) * bq], used[q * bq:(q + 1) * bq])
        gate = _grouped_call(_gmm_kernel, sched, [xb], w_gate, name="moe_gate")
        up = _grouped_call(_gmm_kernel, sched, [xb], w_up, name="moe_up")
        yb = _grouped_call(_gmm_down_kernel, sched, [gate, up], w_down, name="moe_down",
                           out_into=yb, block_offset=q * bq)
    dest2 = dest.reshape(t, TOP_K)
    picked = [yb[dest2[:, k]] for k in range(TOP_K)]

    def combine(rows):
        y = h[rows]
        for k in range(TOP_K):
            y = y + picked[k][rows] * top_w[rows, k:k + 1]
        return y

    return combine(slice(0, n_first)), combine(slice(n_first, t))


def kernel(x_prompt, x_sample, cache_ckv, cache_krope, state_ssm, state_conv, page_table, norm_attn, w_in, norm_ckv, w_uk, w_uv, q_gain_nope, q_gain_rope, k_gain_nope, k_gain_rope, conv_w, gdn_a_log, gdn_dt_bias, gdn_norm, w_pa, w_pb, w_o, norm_ffn, w_router_group, b_router_group, w_router_expert, b_router_expert, w_gate, w_up, w_down):
    depth = w_in.shape[0]
    bsz, seq, _ = x_prompt.shape
    n_s, s_new, _ = x_sample.shape
    assert s_new == 1
    n_p = bsz * seq
    n_all = n_p + n_s
    n_past = page_table.shape[1] * PAGE_SIZE
    cos_p, sin_p = _rope_tables(jnp.tile(jnp.arange(seq, dtype=F32), bsz))
    cos_s, sin_s = _rope_tables(jnp.full((n_s,), n_past, F32))
    tq = _tile(seq, 512, LANES)
    xp = x_prompt.reshape(n_p, D_MODEL)
    xs = x_sample.reshape(n_s, D_MODEL)
    outs = [[] for _ in range(8)]
    row = lambda v: v.astype(F32)[None, :]
    for l in range(depth):
        w_in_r = _reorder_w_in_pallas(w_in[l])
        w_uk_b = w_uk[l].astype(BF16)
        w_uvt_b = w_uv[l].T.astype(BF16)
        wukt = w_uk_b.T
        w_pa_b, w_pb_b, w_o_b = w_pa[l].astype(BF16), w_pb[l].astype(BF16), w_o[l].astype(BF16)
        w_rt_b = jnp.concatenate([w_router_group[l], w_router_expert[l],
                                  jnp.zeros((D_MODEL, LANES - N_GROUPS - N_EXPERTS), F32)], axis=1).astype(BF16)
        gains = (row(q_gain_nope[l]), jnp.tile(row(q_gain_rope[l]), (1, 2)), row(k_gain_nope[l]),
                 jnp.tile(row(k_gain_rope[l]), (1, 2)), row(norm_ckv[l]))
        alog_l = _lane_vec(gdn_a_log[l], MISC_A)
        dtb_l = _lane_vec(gdn_dt_bias[l], MISC_A)
        proj_p = _in_proj(xp, row(norm_attn[l]), w_in_r, tm=1024)
        proj_s = _in_proj(xs, row(norm_attn[l]), w_in_r, tm=n_s)

        qcat_p, kcat_p, c_p, kr_p, vt_p = _mla_prep(proj_p, cos_p, sin_p, *gains, w_uk_b, w_uvt_b,
                                                    tm=tq, with_vt=True, with_qg=False)
        oa_p = _flash_attention(qcat_p, kcat_p, vt_p, batch=bsz, seq=seq, tq=tq)

        qcat_s, kcat_s, c_s, kr_s, qg_s = _mla_prep(proj_s, cos_s, sin_s, *gains, w_uk_b, w_uvt_b,
                                                    tm=n_s, with_vt=False, with_qg=True)
        qa = _headwise_matmul(qg_s, wukt, heads=MLA_HEADS, b_rows_by_head=True, name="q_absorb")
        o_lat = _paged_attention(page_table, cache_ckv[l], jnp.swapaxes(cache_krope[l], 1, 2), wukt,
                                 qa.astype(BF16).reshape(n_s, MLA_HEADS, KV_LORA),
                                 qcat_s.reshape(n_s, MLA_HEADS, QK_PAD), kcat_s.reshape(n_s, MLA_HEADS, QK_PAD),
                                 c_s[:, None, :])
        oa_s = _headwise_matmul(o_lat.astype(BF16).reshape(n_s, MLA_HEADS * KV_LORA), w_uv[l].astype(BF16),
                                heads=MLA_HEADS, b_rows_by_head=False, name="v_absorb").astype(BF16)

        qkv_p, kt_p, gates_p = _gdn_prep(proj_p, conv_w[l], alog_l, dtb_l, batch=bsz, seq=seq, tm=512)
        o_g, ssm_p = _delta_rule(qkv_p, kt_p, gates_p, batch=bsz, seq=seq)
        conv_p = proj_p.reshape(bsz, seq, D_PROJ)[:, seq - (CONV_W - 1):, SEG_QKV:SEG_QKV + GDN_QKV_W]

        o_s, ssm_s = _gdn_step(proj_s, state_conv[l], state_ssm[l], conv_w[l], alog_l, dtb_l)
        conv_s = jnp.concatenate([state_conv[l][:, 1:], proj_s[:, None, SEG_QKV:SEG_QKV + GDN_QKV_W]], axis=1)

        m_p = _merge(o_g, oa_p, proj_p, row(gdn_norm[l]), w_pa_b, w_pb_b, tm=512)
        m_s = _merge(o_s, oa_s, proj_s, row(gdn_norm[l]), w_pa_b, w_pb_b, tm=n_s)
        b_rt = _lane_vec(jnp.concatenate([b_router_group[l], b_router_expert[l]]), 0)
        h_all, hn_all, route = _out_proj(m_p, xp, m_s, xs, w_o_b, row(norm_ffn[l]), w_rt_b, b_rt, tm=512)
        xp, xs = _hier_moe(h_all, hn_all, route, w_gate[l], w_up[l], w_down[l], n_first=n_p)

        for lst, val in zip(outs, (c_p.reshape(bsz, seq, KV_LORA), kr_p[:, :QK_ROPE].reshape(bsz, seq, QK_ROPE),
                                   c_s.reshape(n_s, 1, KV_LORA), kr_s[:, :QK_ROPE].reshape(n_s, 1, QK_ROPE),
                                   ssm_p, conv_p, ssm_s, conv_s)):
            lst.append(val)
    return (xp.reshape(bsz, seq, D_MODEL), xs.reshape(n_s, 1, D_MODEL)) + tuple(jnp.stack(o) for o in outs)
```

```python
import functools

import jax
import jax.numpy as jnp
from jax import lax
from jax.experimental import pallas as pl
from jax.experimental.pallas import tpu as pltpu

F32 = jnp.float32
BF16 = jnp.bfloat16

D_MODEL = 2048
PAGE_SIZE = 128
MLA_HEADS = 16
QK_NOPE = 128
QK_ROPE = 64
QK_HEAD = QK_NOPE + QK_ROPE
QK_PAD = 256
V_HEAD = 128
KV_LORA = 512
ROPE_THETA = 10000.0
GDN_HEADS = 16
GDN_DK = 128
GDN_DV = 128
GDN_HK = GDN_HEADS * GDN_DK
GDN_QKV_W = GDN_HEADS * (2 * GDN_DK + GDN_DV)
CONV_W = 4
GDN_CHUNK = 128
N_GROUPS = 8
EXP_PER_GROUP = 8
N_EXPERTS = N_GROUPS * EXP_PER_GROUP
TOP_K = 2
D_EXPERT = 1408
MOE_ROWS = 128
MOE_SPLITS = 4
WEIGHT_DMA_PARTS = 2
WEIGHT_SLOTS = 3
EPS = 1e-6
LANES = 128
SUBLANES = 8

V7X_VMEM_LIMIT = 56 * 1024 * 1024

SEG_Z = 0
SEG_GA = SEG_Z + GDN_HEADS * GDN_DV
SEG_GB = SEG_GA + D_MODEL
SEG_QKV = SEG_GB + D_MODEL
SEG_QN = SEG_QKV + GDN_QKV_W
SEG_QR = SEG_QN + MLA_HEADS * QK_NOPE
SEG_CKV = SEG_QR + MLA_HEADS * QK_ROPE
SEG_MISC = SEG_CKV + KV_LORA
PROJ_TN = 1024
D_PROJ = -(-(SEG_MISC + LANES) // PROJ_TN) * PROJ_TN
MISC_A = QK_ROPE
MISC_B = QK_ROPE + GDN_HEADS


def _cparams(sem, vmem=V7X_VMEM_LIMIT):
    return pltpu.CompilerParams(dimension_semantics=sem, vmem_limit_bytes=vmem)


def _tile(dim, cap, align):
    if dim <= cap:
        return dim
    for t in range(cap - cap % align, 0, -align):
        if dim % t == 0:
            return t
    raise ValueError(f"no {align}-aligned tile <= {cap} divides {dim}")


def _nt_dot(a, b):
    return lax.dot_general(a, b, (((1,), (1,)), ((), ())), preferred_element_type=F32)


def _rms_rows(x, gain):
    return x * lax.rsqrt(jnp.mean(x * x, axis=-1, keepdims=True) + EPS) * gain


def _sigmoid(x):
    return 1.0 / (1.0 + jnp.exp(-x))


def _in_proj_kernel(x_ref, g_ref, w_ref, o_ref, u_ref):
    @pl.when(pl.program_id(1) == 0)
    def _():
        u_ref[...] = _rms_rows(x_ref[...], g_ref[...]).astype(BF16)

    o_ref[...] = jnp.dot(u_ref[...], w_ref[...], preferred_element_type=F32)


def _in_proj(x, gain, w, *, tm):
    m, d = x.shape
    n = w.shape[1]
    tm = _tile(m, tm, 16)
    tn = PROJ_TN
    assert n % tn == 0
    return pl.pallas_call(
        _in_proj_kernel,
        grid=(m // tm, n // tn),
        in_specs=[pl.BlockSpec((tm, d), lambda i, j: (i, 0)),
                  pl.BlockSpec((1, d), lambda i, j: (0, 0)),
                  pl.BlockSpec((d, tn), lambda i, j: (0, j))],
        out_specs=pl.BlockSpec((tm, tn), lambda i, j: (i, j)),
        out_shape=jax.ShapeDtypeStruct((m, n), F32),
        scratch_shapes=[pltpu.VMEM((tm, d), BF16)],
        compiler_params=_cparams(("arbitrary", "arbitrary")),
        name="in_proj",
    )(x, gain, w)


def _mm_kernel(a_ref, b_ref, o_ref):
    o_ref[...] = jnp.dot(a_ref[...], b_ref[...], preferred_element_type=F32).astype(o_ref.dtype)


def _headwise_matmul(a, b, *, heads, b_rows_by_head, name):
    m = a.shape[0]
    k = a.shape[1] // heads
    if b_rows_by_head:
        n = b.shape[1]
        b_spec = pl.BlockSpec((k, n), lambda h: (h, 0))
    else:
        n = b.shape[1] // heads
        b_spec = pl.BlockSpec((k, n), lambda h: (0, h))
    return pl.pallas_call(
        _mm_kernel,
        grid=(heads,),
        in_specs=[pl.BlockSpec((m, k), lambda h: (0, h)), b_spec],
        out_specs=pl.BlockSpec((m, n), lambda h: (0, h)),
        out_shape=jax.ShapeDtypeStruct((m, heads * n), F32),
        compiler_params=_cparams(("arbitrary",)),
        name=name,
    )(a, b)


def _swap_halves(y, lane):
    half = QK_ROPE // 2
    return jnp.where(lane % QK_ROPE < half, pltpu.roll(y, LANES - half, axis=1), pltpu.roll(y, half, axis=1))


def _rope_pair(x, gain2, cos4, sin4, lane):
    lo = lane < QK_ROPE
    ss = x * x
    s_lo = jnp.sum(jnp.where(lo, ss, 0.0), axis=-1, keepdims=True)
    s_hi = jnp.sum(jnp.where(lo, 0.0, ss), axis=-1, keepdims=True)
    r = jnp.where(lo, lax.rsqrt(s_lo * (1.0 / QK_ROPE) + EPS), lax.rsqrt(s_hi * (1.0 / QK_ROPE) + EPS))
    y = x * r * gain2
    return y * cos4 + _swap_halves(y, lane) * sin4


def _mla_prep_kernel(qn_ref, qr_ref, ckv_ref, misc_ref, cos_ref, sin_ref, gqn_ref, gqr_ref, gkn_ref, gkr_ref,
                     gc_ref, wuk_ref, *rest, with_vt, with_qg):
    rest = list(rest)
    wuvt_ref = rest.pop(0) if with_vt else None
    qcat_ref, kcat_ref, c_ref, kr_ref = rest[:4]
    rest = rest[4:]
    vt_ref = rest.pop(0) if with_vt else None
    qg_ref = rest.pop(0) if with_qg else None
    rows = qn_ref.shape[0]
    lane = lax.broadcasted_iota(jnp.int32, (rows, LANES), 1)
    lo = lane < QK_ROPE
    cos4 = cos_ref[...]
    sin4 = sin_ref[...]

    c = _rms_rows(ckv_ref[...], gc_ref[...])
    c_ref[...] = c
    cb = c.astype(BF16)
    kr = jnp.where(lo, _rope_pair(misc_ref[...], gkr_ref[...], cos4, sin4, lane), 0.0)
    kr_ref[...] = kr
    krb = kr.astype(BF16)

    gqn = gqn_ref[...]
    gkn = gkn_ref[...]
    for h in range(MLA_HEADS):
        qn = _rms_rows(qn_ref[:, h * QK_NOPE:(h + 1) * QK_NOPE], gqn)
        qcat_ref[:, h * QK_PAD:h * QK_PAD + QK_NOPE] = qn.astype(BF16)
        if with_qg:
            qg_ref[:, h * QK_NOPE:(h + 1) * QK_NOPE] = (qn * gkn).astype(BF16)
    gqr = gqr_ref[...]
    for j in range(MLA_HEADS // 2):
        rot = _rope_pair(qr_ref[:, j * LANES:(j + 1) * LANES], gqr, cos4, sin4, lane)
        even = jnp.where(lo, rot, 0.0)
        odd = jnp.where(lo, pltpu.roll(rot, QK_ROPE, axis=1), 0.0)
        qcat_ref[:, (2 * j) * QK_PAD + QK_NOPE:(2 * j + 1) * QK_PAD] = even.astype(BF16)
        qcat_ref[:, (2 * j + 1) * QK_PAD + QK_NOPE:(2 * j + 2) * QK_PAD] = odd.astype(BF16)

    kn = jnp.dot(cb, wuk_ref[...], preferred_element_type=F32)
    for h in range(MLA_HEADS):
        kh = _rms_rows(kn[:, h * QK_NOPE:(h + 1) * QK_NOPE], gkn)
        kcat_ref[:, h * QK_PAD:h * QK_PAD + QK_NOPE] = kh.astype(BF16)
        kcat_ref[:, h * QK_PAD + QK_NOPE:(h + 1) * QK_PAD] = krb
    if with_vt:
        vt_ref[...] = _nt_dot(wuvt_ref[...], cb).astype(BF16)


def _mla_prep(proj, cos4, sin4, gqn, gqr2, gkn, gkr2, gckv, w_uk_b, w_uvt_b, *, tm, with_vt, with_qg):
    n = proj.shape[0]
    tm = _tile(n, tm, 16)
    nt = n // tm
    hq = MLA_HEADS * QK_NOPE
    row = lambda w, off: pl.BlockSpec((tm, w), lambda i, off=off, w=w: (i, off // w))
    const = lambda a: pl.BlockSpec(a.shape, lambda i: (0,) * a.ndim)
    in_specs = [row(hq, SEG_QN), row(MLA_HEADS * QK_ROPE, SEG_QR), row(KV_LORA, SEG_CKV), row(LANES, SEG_MISC),
                pl.BlockSpec((tm, LANES), lambda i: (i, 0)), pl.BlockSpec((tm, LANES), lambda i: (i, 0)),
                const(gqn), const(gqr2), const(gkn), const(gkr2), const(gckv), const(w_uk_b)]
    args = [proj, proj, proj, proj, cos4, sin4, gqn, gqr2, gkn, gkr2, gckv, w_uk_b]
    if with_vt:
        in_specs.append(const(w_uvt_b))
        args.append(w_uvt_b)
    out_specs = [pl.BlockSpec((tm, MLA_HEADS * QK_PAD), lambda i: (i, 0)),
                 pl.BlockSpec((tm, MLA_HEADS * QK_PAD), lambda i: (i, 0)),
                 pl.BlockSpec((tm, KV_LORA), lambda i: (i, 0)),
                 pl.BlockSpec((tm, LANES), lambda i: (i, 0))]
    out_shape = [jax.ShapeDtypeStruct((n, MLA_HEADS * QK_PAD), BF16),
                 jax.ShapeDtypeStruct((n, MLA_HEADS * QK_PAD), BF16),
                 jax.ShapeDtypeStruct((n, KV_LORA), F32),
                 jax.ShapeDtypeStruct((n, LANES), F32)]
    if with_vt:
        out_specs.append(pl.BlockSpec((None, MLA_HEADS * V_HEAD, tm), lambda i: (i, 0, 0)))
        out_shape.append(jax.ShapeDtypeStruct((nt, MLA_HEADS * V_HEAD, tm), BF16))
    if with_qg:
        out_specs.append(pl.BlockSpec((tm, hq), lambda i: (i, 0)))
        out_shape.append(jax.ShapeDtypeStruct((n, hq), BF16))
    return pl.pallas_call(
        functools.partial(_mla_prep_kernel, with_vt=with_vt, with_qg=with_qg),
        grid=(nt,), in_specs=in_specs, out_specs=out_specs, out_shape=out_shape,
        compiler_params=_cparams(("arbitrary",)),
        name="mla_prep",
    )(*args)


def _flash_kernel(q_ref, k_ref, vt_ref, o_ref, m_ref, l_ref, acc_ref, *, tq, scale):
    qi = pl.program_id(2)
    q = q_ref[...]
    m_ref[...] = jnp.full(m_ref.shape, -jnp.inf, F32)
    l_ref[...] = jnp.zeros(l_ref.shape, F32)
    acc_ref[...] = jnp.zeros(acc_ref.shape, F32)

    c2 = scale * 1.4426950408889634

    def block(ki, nblk, masked):
        start = pl.multiple_of(ki * tq, tq)
        k = k_ref[pl.ds(start, nblk * tq), :]
        s = _nt_dot(k, q)
        if masked:
            key = lax.broadcasted_iota(jnp.int32, s.shape, 0)
            qry = lax.broadcasted_iota(jnp.int32, s.shape, 1)
            s = jnp.where(key <= qry, s, -jnp.inf)
        m_old = m_ref[...]
        m_new = jnp.maximum(m_old, jnp.max(s, axis=0, keepdims=True))
        alpha = jnp.exp2((m_old - m_new) * c2)
        p32 = jnp.exp2((s - m_new) * c2)
        l_ref[...] = l_ref[...] * alpha + jnp.sum(p32, axis=0, keepdims=True)
        p = p32.astype(BF16)
        pv = jnp.dot(vt_ref[ki], p[0:tq], preferred_element_type=F32)
        for j in range(1, nblk):
            pv = pv + jnp.dot(vt_ref[ki + j], p[j * tq:(j + 1) * tq], preferred_element_type=F32)
        acc_ref[...] = acc_ref[...] * alpha + pv
        m_ref[...] = m_new

    def body(kp, carry):
        block(2 * kp, 2, False)
        return carry

    lax.fori_loop(0, qi // 2, body, 0)

    @pl.when(qi % 2 == 1)
    def _():
        block(qi - 1, 1, False)

    block(qi, 1, True)
    o_ref[...] = (acc_ref[...] / l_ref[...]).T.astype(o_ref.dtype)


def _flash_attention(q_cat, k_cat, vt, *, batch, seq, tq):
    nq = seq // tq
    kern = functools.partial(_flash_kernel, tq=tq, scale=QK_HEAD ** -0.5)
    return pl.pallas_call(
        kern,
        grid=(batch, MLA_HEADS, nq),
        in_specs=[pl.BlockSpec((tq, QK_PAD), lambda b, h, i: (b * nq + i, h)),
                  pl.BlockSpec((seq, QK_PAD), lambda b, h, i: (b, h)),
                  pl.BlockSpec((nq, V_HEAD, tq), lambda b, h, i: (b, h, 0))],
        out_specs=pl.BlockSpec((tq, V_HEAD), lambda b, h, i: (b * nq + i, h)),
        out_shape=jax.ShapeDtypeStruct((batch * seq, MLA_HEADS * V_HEAD), BF16),
        scratch_shapes=[pltpu.VMEM((1, tq), F32), pltpu.VMEM((1, tq), F32), pltpu.VMEM((V_HEAD, tq), F32)],
        compiler_params=_cparams(("arbitrary", "arbitrary", "arbitrary")),
        name="mla_prompt_attention",
    )(q_cat, k_cat, vt)


PAGES_PER_STEP = 16
PAGE_RING = 3
PAGE_GROUPS = 2


def _paged_kernel(pt_ref, ckv_hbm, krt_hbm, wukt_ref, qa_ref, qcat_ref, qcat_prev_ref, kcat_prev_ref, cn_prev_ref,
                  o_ref, lhs_ref, cbuf, krbuf, csem, krsem, c_scr, krt_scr, s_scr, m_ref, l_ref, acc_ref,
                  *, scale, steps, n_chunks):
    npg = PAGES_PER_STEP
    t = pl.program_id(0)
    n_hd = MLA_HEADS * QK_NOPE
    span = npg * PAGE_SIZE // PAGE_GROUPS
    prev_chunk = (t + steps - 1) % steps

    def page_copies(chunk, slot, resolve_pages):
        base = jnp.minimum(chunk, n_chunks - 1) * npg
        copies = []
        for i in range(npg):
            page = pt_ref[base + i] if resolve_pages else 0
            copies.append(pltpu.make_async_copy(ckv_hbm.at[page], cbuf.at[slot, i], csem.at[slot]))
            copies.append(pltpu.make_async_copy(krt_hbm.at[page], krbuf.at[slot, i], krsem.at[slot]))
        return copies

    @pl.when(t == 0)
    def _():
        for chunk in range(PAGE_RING - 1):
            for cp in page_copies(chunk, chunk, True):
                cp.start()
        lhs_ref[0:n_hd, :] = wukt_ref[...]
        c_scr[1] = jnp.zeros(c_scr.shape[1:], BF16)
        s_scr[1] = jnp.zeros(s_scr.shape[1:], F32)
        m_ref[...] = jnp.zeros(m_ref.shape, F32)
        l_ref[...] = jnp.zeros(l_ref.shape, F32)
        acc_ref[...] = jnp.zeros(acc_ref.shape, F32)

    @pl.when(t % steps == 0)
    def _():
        lhs_ref[n_hd:n_hd + MLA_HEADS, :] = qa_ref[...]

    ring = t % PAGE_RING

    def step(cur, prv):
        for cp in page_copies(t, ring, False):
            cp.wait()
        first = prev_chunk == 0
        s_prev = s_scr[prv]
        m_old = jnp.where(first, -jnp.inf, m_ref[...])
        l_old = jnp.where(first, 0.0, l_ref[...])
        acc_old = jnp.where(first, 0.0, acc_ref[...])
        m_new = jnp.maximum(m_old, jnp.max(s_prev, axis=-1, keepdims=True))
        alpha = jnp.exp(m_old - m_new)
        pr = jnp.exp(s_prev - m_new)
        l_ref[...] = l_old * alpha + jnp.sum(pr, axis=-1, keepdims=True)
        acc_ref[...] = acc_old * alpha + jnp.dot(pr.astype(BF16), c_scr[prv], preferred_element_type=F32)
        m_ref[...] = m_new
        for i in range(npg):
            c_scr[cur, i * PAGE_SIZE:(i + 1) * PAGE_SIZE, :] = cbuf[ring, i].astype(BF16)
            krt_scr[:, i * PAGE_SIZE:(i + 1) * PAGE_SIZE] = krbuf[ring, i].astype(BF16)
        qr = qcat_ref[:, QK_NOPE:QK_NOPE + QK_ROPE]
        for g in range(PAGE_GROUPS):
            c = c_scr[cur, g * span:(g + 1) * span, :]
            res = _nt_dot(lhs_ref[...], c)
            kt = res[0:n_hd, :].reshape(MLA_HEADS, QK_NOPE, span)
            inv = lax.rsqrt(jnp.sum(kt * kt, axis=1) * (1.0 / QK_NOPE) + EPS)
            s_rope = jnp.dot(qr, krt_scr[:, g * span:(g + 1) * span], preferred_element_type=F32)
            s_scr[cur, :, g * span:(g + 1) * span] = (res[n_hd:n_hd + MLA_HEADS, :] * inv + s_rope) * scale
        for cp in page_copies(t + PAGE_RING - 1, (t + PAGE_RING - 1) % PAGE_RING, True):
            cp.start()

    @pl.when(t % 2 == 0)
    def _():
        step(0, 1)

    @pl.when(t % 2 == 1)
    def _():
        step(1, 0)

    @pl.when((prev_chunk == steps - 1) & (t > 0))
    def _():
        s_new = jnp.sum(qcat_prev_ref[...].astype(F32) * kcat_prev_ref[...].astype(F32), axis=-1,
                        keepdims=True) * scale
        m_old = m_ref[...]
        m_fin = jnp.maximum(m_old, s_new)
        a2 = jnp.exp(m_old - m_fin)
        p_new = jnp.exp(s_new - m_fin)
        l_fin = l_ref[...] * a2 + p_new
        cn = cn_prev_ref[...].astype(BF16).astype(F32)
        o_ref[...] = (acc_ref[...] * a2 + p_new * cn) / l_fin

    @pl.when(t == n_chunks)
    def _():
        for ahead in range(1, PAGE_RING):
            for cp in page_copies(t + ahead, (t + ahead) % PAGE_RING, False):
                cp.wait()


def _paged_attention(page_table, ckv_pool, krt_pool, wukt, qa, q_cat, k_cat, cn):
    n_s, n_pages = page_table.shape
    npg = PAGES_PER_STEP
    assert n_pages % npg == 0
    steps = n_pages // npg
    n_chunks = n_s * steps
    span = npg * PAGE_SIZE
    pt_flat = page_table.reshape(n_s * n_pages)

    def cur_sample(shape):
        return pl.BlockSpec((None,) + shape, lambda t, pt: (jnp.minimum(t // steps, n_s - 1), 0, 0))

    def prev_sample(shape):
        return pl.BlockSpec((None,) + shape, lambda t, pt: (jnp.maximum(t - 1, 0) // steps, 0, 0))

    in_specs = [pl.BlockSpec(memory_space=pl.ANY), pl.BlockSpec(memory_space=pl.ANY),
                pl.BlockSpec(wukt.shape, lambda t, pt: (0, 0)),
                cur_sample((MLA_HEADS, KV_LORA)), cur_sample((MLA_HEADS, QK_PAD)),
                prev_sample((MLA_HEADS, QK_PAD)), prev_sample((MLA_HEADS, QK_PAD)), prev_sample((1, KV_LORA))]
    grid_spec = pltpu.PrefetchScalarGridSpec(
        num_scalar_prefetch=1,
        grid=(n_chunks + 1,),
        in_specs=in_specs,
        out_specs=prev_sample((MLA_HEADS, KV_LORA)),
        scratch_shapes=[pltpu.VMEM((MLA_HEADS * QK_NOPE + MLA_HEADS, KV_LORA), BF16),
                        pltpu.VMEM((PAGE_RING, npg, PAGE_SIZE, KV_LORA), F32),
                        pltpu.VMEM((PAGE_RING, npg, QK_ROPE, PAGE_SIZE), F32),
                        pltpu.SemaphoreType.DMA((PAGE_RING,)), pltpu.SemaphoreType.DMA((PAGE_RING,)),
                        pltpu.VMEM((2, span, KV_LORA), BF16), pltpu.VMEM((QK_ROPE, span), BF16),
                        pltpu.VMEM((2, MLA_HEADS, span), F32),
                        pltpu.VMEM((MLA_HEADS, 1), F32), pltpu.VMEM((MLA_HEADS, 1), F32),
                        pltpu.VMEM((MLA_HEADS, KV_LORA), F32)])
    return pl.pallas_call(
        functools.partial(_paged_kernel, scale=QK_HEAD ** -0.5, steps=steps, n_chunks=n_chunks),
        grid_spec=grid_spec,
        out_shape=jax.ShapeDtypeStruct((n_s, MLA_HEADS, KV_LORA), F32),
        compiler_params=_cparams(("arbitrary",)),
        name="mla_sample_attention",
    )(pt_flat, ckv_pool, krt_pool, wukt, qa, q_cat, q_cat, k_cat, cn)


def _gdn_prep_kernel(x_ref, halo_ref, misc_ref, cw_ref, alog_ref, dtb_ref, qkv_ref, kt_ref, gates_ref, xs_ref,
                     *, tiles_per_seq):
    i = pl.program_id(0)
    s = pl.program_id(1)
    tm = x_ref.shape[0]
    first = (i % tiles_per_seq) == 0

    @pl.when(first)
    def _():
        xs_ref[0:SUBLANES, :] = jnp.zeros((SUBLANES, xs_ref.shape[1]), F32)

    @pl.when(jnp.logical_not(first))
    def _():
        xs_ref[0:SUBLANES, :] = halo_ref[...]

    xs_ref[SUBLANES:SUBLANES + tm, :] = x_ref[...]
    acc = xs_ref[pl.ds(SUBLANES - (CONV_W - 1), tm), :] * cw_ref[0:1, :]
    for w in range(1, CONV_W):
        acc = acc + xs_ref[pl.ds(SUBLANES - (CONV_W - 1) + w, tm), :] * cw_ref[w:w + 1, :]
    y = acc * _sigmoid(acc)

    @pl.when(s == 2)
    def _():
        qkv_ref[...] = y

    @pl.when(s < 2)
    def _():
        scale = jnp.where(s == 0, GDN_DK ** -0.5, 1.0).astype(F32)
        for h in range(GDN_HEADS):
            yh = y[:, h * GDN_DK:(h + 1) * GDN_DK]
            yn = yh * (lax.rsqrt(jnp.sum(yh * yh, axis=-1, keepdims=True) + EPS) * scale)
            qkv_ref[:, h * GDN_DK:(h + 1) * GDN_DK] = yn

    @pl.when(s == 1)
    def _():
        kt_ref[...] = qkv_ref[...].T

    @pl.when(s == 0)
    def _():
        x = misc_ref[...] + dtb_ref[...]
        softplus = jnp.maximum(x, 0.0) + jnp.log(1.0 + jnp.exp(-jnp.abs(x)))
        lane = lax.broadcasted_iota(jnp.int32, x.shape, 1)
        is_beta = (lane >= MISC_B) & (lane < MISC_B + GDN_HEADS)
        gates_ref[...] = jnp.where(is_beta, _sigmoid(misc_ref[...]), -jnp.exp(alog_ref[...]) * softplus)


def _gdn_prep(proj, conv_w, alog_l, dtb_l, *, batch, seq, tm):
    n = batch * seq
    tm = _tile(seq, tm, LANES)
    tps = seq // tm
    sec0 = SEG_QKV // GDN_HK
    halo_blocks = tm // SUBLANES
    return pl.pallas_call(
        functools.partial(_gdn_prep_kernel, tiles_per_seq=tps),
        grid=(n // tm, 3),
        in_specs=[pl.BlockSpec((tm, GDN_HK), lambda i, s: (i, sec0 + s)),
                  pl.BlockSpec((SUBLANES, GDN_HK), lambda i, s: (jnp.maximum(i * halo_blocks - 1, 0), sec0 + s)),
                  pl.BlockSpec((tm, LANES), lambda i, s: (i, SEG_MISC // LANES)),
                  pl.BlockSpec((CONV_W, GDN_HK), lambda i, s: (0, s)),
                  pl.BlockSpec((1, LANES), lambda i, s: (0, 0)),
                  pl.BlockSpec((1, LANES), lambda i, s: (0, 0))],
        out_specs=[pl.BlockSpec((tm, GDN_HK), lambda i, s: (i, s)),
                   pl.BlockSpec((None, GDN_HK, tm), lambda i, s: (i // tps, 0, i % tps)),
                   pl.BlockSpec((tm, LANES), lambda i, s: (i, 0))],
        out_shape=[jax.ShapeDtypeStruct((n, 3 * GDN_HK), F32),
                   jax.ShapeDtypeStruct((batch, GDN_HK, seq), F32),
                   jax.ShapeDtypeStruct((n, LANES), F32)],
        scratch_shapes=[pltpu.VMEM((tm + SUBLANES, GDN_HK), F32)],
        compiler_params=_cparams(("arbitrary", "arbitrary")),
        name="gdn_prep",
    )(proj, proj, proj, conv_w, alog_l, dtb_l)


def _split3(x):
    hi = x.astype(BF16)
    r1 = x - hi.astype(F32)
    mid = r1.astype(BF16)
    lo = (r1 - mid.astype(F32)).astype(BF16)
    return hi, mid, lo


def _bdot(a, b):
    return jnp.dot(a.astype(BF16), b.astype(BF16), preferred_element_type=F32)


def _delta_kernel(q_ref, k_ref, v_ref, kt_ref, gates_ref, o_ref, s_out_ref, s_ref):
    c = GDN_CHUNK
    n = pl.program_id(1)
    heads = range(GDN_HEADS)

    @pl.when(n == 0)
    def _():
        s_ref[...] = jnp.zeros(s_ref.shape, F32)

    row = lax.broadcasted_iota(jnp.int32, (c, c), 0)
    col = lax.broadcasted_iota(jnp.int32, (c, c), 1)
    incl = row >= col
    strict = row > col
    eye = (row == col).astype(F32)
    tril = incl.astype(BF16)
    triu = (row <= col).astype(BF16)
    gates = gates_ref[...]
    gates_t = gates.T
    gc_all = sum(jnp.dot(tril, part, preferred_element_type=F32) for part in _split3(gates))
    gct_all = sum(jnp.dot(part, triu, preferred_element_type=F32) for part in _split3(gates_t))

    sl = [slice(h * GDN_DK, (h + 1) * GDN_DK) for h in heads]
    gcol = [gc_all[:, MISC_A + h:MISC_A + h + 1] for h in heads]
    grow = [gct_all[MISC_A + h:MISC_A + h + 1, :] for h in heads]
    glast = [gct_all[MISC_A + h:MISC_A + h + 1, c - 1:c] for h in heads]
    bcol = [gates[:, MISC_B + h:MISC_B + h + 1] for h in heads]
    eg = [jnp.exp(g) for g in gcol]
    e = [jnp.exp(jnp.where(incl, gcol[h] - grow[h], 0.0)) for h in heads]
    kb = [k_ref[:, sl[h]] * bcol[h] for h in heads]
    ktb = [kt_ref[sl[h], :].astype(BF16) for h in heads]
    x = [-(jnp.dot(kb[h].astype(BF16), ktb[h], preferred_element_type=F32) * jnp.where(strict, e[h], 0.0))
         for h in heads]
    t = [eye + x[h] for h in heads]
    for _ in range(6):
        x = [_bdot(x[h], x[h]) for h in heads]
        t = [t[h] + _bdot(t[h], x[h]) for h in heads]
    rhs = [jnp.concatenate([v_ref[:, sl[h]] * bcol[h], kb[h] * eg[h]], axis=1) for h in heads]
    sol = [_bdot(t[h], rhs[h]) for h in heads]
    attn = [jnp.dot(q_ref[:, sl[h]].astype(BF16), ktb[h], preferred_element_type=F32) * jnp.where(incl, e[h], 0.0)
            for h in heads]
    s_old = [s_ref[h] for h in heads]
    lhs = [jnp.concatenate([sol[h][:, GDN_DV:], q_ref[:, sl[h]] * eg[h]], axis=0) for h in heads]
    prod = [_bdot(lhs[h], s_old[h]) for h in heads]
    v_new = [sol[h][:, :GDN_DV] - prod[h][:c] for h in heads]
    for h in heads:
        o_ref[:, sl[h]] = prod[h][c:] + _bdot(attn[h], v_new[h])
    for h in heads:
        kdt = kt_ref[sl[h], :] * jnp.exp(glast[h] - grow[h])
        s_ref[h] = s_old[h] * jnp.exp(glast[h]) + _bdot(kdt, v_new[h])

    @pl.when(n == pl.num_programs(1) - 1)
    def _():
        s_out_ref[...] = s_ref[...]


def _delta_rule(qkv, kt, gates, *, batch, seq):
    c = GDN_CHUNK
    assert seq % c == 0
    nc = seq // c
    sec = lambda s: pl.BlockSpec((c, GDN_HK), lambda b, n, s=s: (b * nc + n, s))
    return pl.pallas_call(
        _delta_kernel,
        grid=(batch, nc),
        in_specs=[sec(0), sec(1), sec(2),
                  pl.BlockSpec((None, GDN_HK, c), lambda b, n: (b, 0, n)),
                  pl.BlockSpec((c, LANES), lambda b, n: (b * nc + n, 0))],
        out_specs=[pl.BlockSpec((c, GDN_HK), lambda b, n: (b * nc + n, 0)),
                   pl.BlockSpec((None, GDN_HEADS, GDN_DK, GDN_DV), lambda b, n: (b, 0, 0, 0))],
        out_shape=[jax.ShapeDtypeStruct((batch * seq, GDN_HK), F32),
                   jax.ShapeDtypeStruct((batch, GDN_HEADS, GDN_DK, GDN_DV), F32)],
        scratch_shapes=[pltpu.VMEM((GDN_HEADS, GDN_DK, GDN_DV), F32)],
        compiler_params=_cparams(("arbitrary", "arbitrary")),
        name="gdn_delta_rule",
    )(qkv, qkv, qkv, kt, gates)


GDN_STEP_ROWS = SUBLANES


def _gdn_step_kernel(x_ref, buf_ref, misc_ref, cw_ref, alog_ref, dtb_ref, s_ref, o_ref, s_out_ref):
    nb = x_ref.shape[0]
    acc = x_ref[...] * cw_ref[CONV_W - 1:CONV_W, :]
    for w in range(CONV_W - 1):
        acc = acc + buf_ref[:, w, :] * cw_ref[w:w + 1, :]
    y = acc * _sigmoid(acc)
    x = misc_ref[...] + dtb_ref[...]
    softplus = jnp.maximum(x, 0.0) + jnp.log(1.0 + jnp.exp(-jnp.abs(x)))
    eg_all = jnp.exp(-jnp.exp(alog_ref[...]) * softplus)
    beta_all = _sigmoid(misc_ref[...])
    for h in range(GDN_HEADS):
        qs = slice(h * GDN_DK, (h + 1) * GDN_DK)
        ks = slice(GDN_HK + h * GDN_DK, GDN_HK + (h + 1) * GDN_DK)
        vs = slice(2 * GDN_HK + h * GDN_DV, 2 * GDN_HK + (h + 1) * GDN_DV)
        q = y[:, qs]
        k = y[:, ks]
        q = q * (lax.rsqrt(jnp.sum(q * q, axis=-1, keepdims=True) + EPS) * (GDN_DK ** -0.5))
        k = k * lax.rsqrt(jnp.sum(k * k, axis=-1, keepdims=True) + EPS)
        eg = eg_all[:, MISC_A + h:MISC_A + h + 1]
        beta = beta_all[:, MISC_B + h:MISC_B + h + 1]
        vb = y[:, vs] * beta
        kcd = k * (beta * eg)
        qd = q * eg
        qk = jnp.sum(q * k, axis=-1, keepdims=True)
        kt = k.T
        for b in range(nb):
            s_old = s_ref[b, h]
            lhs = jnp.concatenate([kcd[b:b + 1], qd[b:b + 1], jnp.zeros((SUBLANES - 2, GDN_DK), F32)], axis=0)
            prod = jnp.dot(lhs.astype(BF16), s_old.astype(BF16), preferred_element_type=F32)
            v_new = vb[b:b + 1] - prod[0:1]
            o_ref[b:b + 1, qs] = prod[1:2] + qk[b:b + 1] * v_new
            s_out_ref[b, h] = s_old * eg[b:b + 1] + kt[:, b:b + 1] * v_new


def _gdn_step(proj, state_conv, state_ssm, conv_w, alog_l, dtb_l):
    n_s = proj.shape[0]
    nb = min(GDN_STEP_ROWS, n_s)
    assert n_s % nb == 0
    return pl.pallas_call(
        _gdn_step_kernel,
        grid=(n_s // nb,),
        in_specs=[pl.BlockSpec((nb, GDN_QKV_W), lambda i: (i, SEG_QKV // GDN_QKV_W)),
                  pl.BlockSpec((nb, CONV_W - 1, GDN_QKV_W), lambda i: (i, 0, 0)),
                  pl.BlockSpec((nb, LANES), lambda i: (i, SEG_MISC // LANES)),
                  pl.BlockSpec((CONV_W, GDN_QKV_W), lambda i: (0, 0)),
                  pl.BlockSpec((1, LANES), lambda i: (0, 0)),
                  pl.BlockSpec((1, LANES), lambda i: (0, 0)),
                  pl.BlockSpec((nb, GDN_HEADS, GDN_DK, GDN_DV), lambda i: (i, 0, 0, 0))],
        out_specs=[pl.BlockSpec((nb, GDN_HK), lambda i: (i, 0)),
                   pl.BlockSpec((nb, GDN_HEADS, GDN_DK, GDN_DV), lambda i: (i, 0, 0, 0))],
        out_shape=[jax.ShapeDtypeStruct((n_s, GDN_HK), F32),
                   jax.ShapeDtypeStruct(state_ssm.shape, F32)],
        compiler_params=_cparams(("arbitrary",)),
        name="gdn_step",
    )(proj, state_conv, proj, conv_w, alog_l, dtb_l, state_ssm)


def _merge_kernel(og_ref, z_ref, oa_ref, ga_ref, gb_ref, gn_ref, wpa_ref, wpb_ref, m_ref, ob_ref):
    @pl.when(pl.program_id(1) == 0)
    def _():
        gn = gn_ref[...]
        for h in range(GDN_HEADS):
            sl = slice(h * GDN_DV, (h + 1) * GDN_DV)
            z = z_ref[:, sl]
            ob_ref[:, sl] = (_rms_rows(og_ref[:, sl], gn) * (z * _sigmoid(z))).astype(BF16)

    pa = jnp.dot(oa_ref[...], wpa_ref[...], preferred_element_type=F32)
    pb = jnp.dot(ob_ref[...], wpb_ref[...], preferred_element_type=F32)
    m_ref[...] = (_sigmoid(ga_ref[...]) * pa + _sigmoid(gb_ref[...]) * pb).astype(BF16)


def _merge(o_g, oa, proj, gdn_norm, w_pa_b, w_pb_b, *, tm):
    n = o_g.shape[0]
    tm = _tile(n, tm, 16)
    tn = 1024
    nj = D_MODEL // tn
    return pl.pallas_call(
        _merge_kernel,
        grid=(n // tm, nj),
        in_specs=[pl.BlockSpec((tm, GDN_HK), lambda i, j: (i, 0)),
                  pl.BlockSpec((tm, GDN_HK), lambda i, j: (i, SEG_Z // GDN_HK)),
                  pl.BlockSpec((tm, MLA_HEADS * V_HEAD), lambda i, j: (i, 0)),
                  pl.BlockSpec((tm, tn), lambda i, j: (i, SEG_GA // tn + j)),
                  pl.BlockSpec((tm, tn), lambda i, j: (i, SEG_GB // tn + j)),
                  pl.BlockSpec((1, GDN_DV), lambda i, j: (0, 0)),
                  pl.BlockSpec((MLA_HEADS * V_HEAD, tn), lambda i, j: (0, j)),
                  pl.BlockSpec((GDN_HK, tn), lambda i, j: (0, j))],
        out_specs=pl.BlockSpec((tm, tn), lambda i, j: (i, j)),
        out_shape=jax.ShapeDtypeStruct((n, D_MODEL), BF16),
        scratch_shapes=[pltpu.VMEM((tm, GDN_HK), BF16)],
        compiler_params=_cparams(("arbitrary", "arbitrary")),
        name="gated_merge",
    )(o_g, proj, oa, proj, proj, gdn_norm, w_pa_b, w_pb_b)


def _pack_bf16_pairs(x):
    w = x.shape[1] // 2
    bits = lax.bitcast_convert_type(x.astype(F32), jnp.uint32)
    return (bits[:, :w] >> 16) | bits[:, w:]


def _unpack_bf16_pairs(p):
    lo = lax.bitcast_convert_type(p << 16, F32)
    hi = lax.bitcast_convert_type(p & jnp.uint32(0xFFFF0000), F32)
    return jnp.concatenate([lo, hi], axis=1).astype(BF16)


ROUTE_E1, ROUTE_E2, ROUTE_W1, ROUTE_W2 = 0, 1, 2, 3


def _first_lane_of_max(x, lane):
    m = jnp.max(x, axis=-1, keepdims=True)
    return m, jnp.min(jnp.where(x == m, lane, LANES), axis=-1, keepdims=True)


def _route(logits, lane):
    neg = -jnp.inf
    is_g = lane < N_GROUPS
    gmax, grp = _first_lane_of_max(jnp.where(is_g, logits, neg), lane)
    p_grp = 1.0 / jnp.sum(jnp.where(is_g, jnp.exp(logits - gmax), 0.0), axis=-1, keepdims=True)
    lo = N_GROUPS + grp * EXP_PER_GROUP
    in_grp = (lane >= lo) & (lane < lo + EXP_PER_GROUP)
    emax = jnp.max(jnp.where(in_grp, logits, neg), axis=-1, keepdims=True)
    ee = jnp.where(in_grp, jnp.exp(logits - emax), 0.0)
    soft = jnp.where(in_grp, ee / jnp.sum(ee, axis=-1, keepdims=True), -1.0)
    w1, i1 = _first_lane_of_max(soft, lane)
    w2, i2 = _first_lane_of_max(jnp.where(lane == i1, -1.0, soft), lane)
    wsum = w1 + w2
    tile = jnp.where(lane == ROUTE_E1, (i1 - N_GROUPS).astype(F32), 0.0)
    tile = jnp.where(lane == ROUTE_E2, (i2 - N_GROUPS).astype(F32), tile)
    tile = jnp.where(lane == ROUTE_W1, w1 / wsum * p_grp, tile)
    return jnp.where(lane == ROUTE_W2, w2 / wsum * p_grp, tile)


def _out_proj_kernel(mp_ref, xp_ref, ms_ref, xs_ref, wo_ref, gf_ref, wrt_ref, brt_ref, h_ref, hn_ref, rt_ref,
                     *, np_tiles):
    i = pl.program_id(0)

    def rows(m_ref, x_ref, n):
        h = x_ref[...] + jnp.dot(m_ref[...], wo_ref[...], preferred_element_type=F32)
        h_ref[0:n, :] = h
        hnb = _rms_rows(h, gf_ref[...]).astype(BF16)
        hn_ref[0:n, :] = _pack_bf16_pairs(hnb)
        logits = jnp.dot(hnb, wrt_ref[...], preferred_element_type=F32) + brt_ref[...]
        rt_ref[0:n, :] = _route(logits, lax.broadcasted_iota(jnp.int32, logits.shape, 1))

    @pl.when(i < np_tiles)
    def _():
        rows(mp_ref, xp_ref, mp_ref.shape[0])

    @pl.when(i == np_tiles)
    def _():
        rows(ms_ref, xs_ref, ms_ref.shape[0])


def _out_proj(m_p, x_p, m_s, x_s, w_o_b, norm_ffn, w_rt_b, b_rt, *, tm):
    n_p, n_s = m_p.shape[0], m_s.shape[0]
    tm = _tile(n_p, tm, 16)
    np_tiles = n_p // tm
    assert n_s <= tm
    const = lambda a: pl.BlockSpec(a.shape, lambda i: (0,) * a.ndim)
    p_spec = pl.BlockSpec((tm, D_MODEL), lambda i: (jnp.minimum(i, np_tiles - 1), 0))
    out = lambda w: pl.BlockSpec((tm, w), lambda i: (i, 0))
    return pl.pallas_call(
        functools.partial(_out_proj_kernel, np_tiles=np_tiles),
        grid=(np_tiles + 1,),
        in_specs=[p_spec, p_spec, const(m_s), const(x_s), const(w_o_b), const(norm_ffn), const(w_rt_b),
                  const(b_rt)],
        out_specs=[out(D_MODEL), out(D_MODEL // 2), out(LANES)],
        out_shape=[jax.ShapeDtypeStruct((n_p + n_s, D_MODEL), F32),
                   jax.ShapeDtypeStruct((n_p + n_s, D_MODEL // 2), jnp.uint32),
                   jax.ShapeDtypeStruct((n_p + n_s, LANES), F32)],
        compiler_params=_cparams(("arbitrary",)),
        name="out_proj_router",
    )(m_p, x_p, m_s, x_s, w_o_b, norm_ffn, w_rt_b, b_rt)


def _expert_weights(sched_ref, w_hbm, wbuf, wb_ref, sem):
    i = pl.program_id(0)

    part = wbuf.shape[1] // WEIGHT_DMA_PARTS

    def fetch(expert, slot):
        return [pltpu.make_async_copy(w_hbm.at[expert, pl.ds(p * part, part)],
                                      wbuf.at[slot, pl.ds(p * part, part)], sem.at[slot, p])
                for p in range(WEIGHT_DMA_PARTS)]

    def start(expert, slot):
        for p, cp in enumerate(fetch(expert, slot)):
            cp.start(priority=p)

    @pl.when(i == 0)
    def _():
        start(sched_ref[0, 0], 0)

        @pl.when(sched_ref[5, 0] >= 0)
        def _():
            start(sched_ref[5, 0], 1)

    @pl.when(sched_ref[2, i] != 0)
    def _():
        slot = sched_ref[3, i]
        for cp in fetch(sched_ref[0, i], slot):
            cp.wait()

        @pl.when(sched_ref[4, i] >= 0)
        def _():
            start(sched_ref[4, i], (slot + WEIGHT_SLOTS - 1) % WEIGHT_SLOTS)

        wb_ref[...] = wbuf[slot].astype(BF16)


def _gmm_kernel(sched_ref, x_ref, w_hbm, o_ref, wbuf, wb_ref, sem):
    i = pl.program_id(0)
    _expert_weights(sched_ref, w_hbm, wbuf, wb_ref, sem)

    @pl.when(sched_ref[1, i] != 0)
    def _():
        o_ref[...] = jnp.dot(_unpack_bf16_pairs(x_ref[...]), wb_ref[...], preferred_element_type=F32)

    @pl.when(sched_ref[1, i] == 0)
    def _():
        o_ref[...] = jnp.zeros(o_ref.shape, o_ref.dtype)


def _gmm_down_kernel(sched_ref, g_ref, u_ref, w_hbm, acc_hbm, o_ref, wbuf, wb_ref, sem):
    del acc_hbm
    i = pl.program_id(0)
    _expert_weights(sched_ref, w_hbm, wbuf, wb_ref, sem)

    @pl.when(sched_ref[1, i] != 0)
    def _():
        gate = g_ref[...]
        act = gate * _sigmoid(gate) * u_ref[...]
        o_ref[...] = jnp.dot(act.astype(BF16), wb_ref[...], preferred_element_type=F32)

    @pl.when(sched_ref[1, i] == 0)
    def _():
        o_ref[...] = jnp.zeros(o_ref.shape, o_ref.dtype)


def _expert_schedule(bexp, used):
    nb = bexp.shape[0]
    first = jnp.concatenate([jnp.ones((1,), jnp.int32), (bexp[1:] != bexp[:-1]).astype(jnp.int32)])
    run = jnp.cumsum(first) - 1
    run_expert = jnp.zeros((nb,), jnp.int32).at[run].set(bexp)
    ahead = WEIGHT_SLOTS - 1
    nxt = jnp.where(run + ahead <= run[-1], run_expert[jnp.minimum(run + ahead, nb - 1)], -1)
    second = jnp.broadcast_to(jnp.where(run[-1] >= 1, run_expert[jnp.minimum(1, nb - 1)], -1), (nb,))
    return jnp.stack([bexp, used, first, run % WEIGHT_SLOTS, nxt, second]).astype(jnp.int32)


def _grouped_call(kern, sched, row_inputs, w, *, name, out_into=None, block_offset=0):
    rows = row_inputs[0].shape[0]
    k, n = w.shape[1], w.shape[2]
    nb = rows // MOE_ROWS
    in_specs = ([pl.BlockSpec((MOE_ROWS, a.shape[1]), lambda i, sc: (i, 0)) for a in row_inputs]
                + [pl.BlockSpec(memory_space=pl.ANY)])
    args = [sched, *row_inputs, w]
    aliases = {}
    if out_into is not None:
        in_specs.append(pl.BlockSpec(memory_space=pl.ANY))
        aliases = {len(args): 0}
        args.append(out_into)
        rows = out_into.shape[0]
    grid_spec = pltpu.PrefetchScalarGridSpec(
        num_scalar_prefetch=1,
        grid=(nb,),
        in_specs=in_specs,
        out_specs=pl.BlockSpec((MOE_ROWS, n), lambda i, sc: (i + block_offset, 0)),
        scratch_shapes=[pltpu.VMEM((WEIGHT_SLOTS, k, n), F32), pltpu.VMEM((k, n), BF16),
                        pltpu.SemaphoreType.DMA((WEIGHT_SLOTS, WEIGHT_DMA_PARTS))])
    return pl.pallas_call(
        kern, grid_spec=grid_spec,
        out_shape=jax.ShapeDtypeStruct((rows, n), F32),
        input_output_aliases=aliases,
        compiler_params=_cparams(("arbitrary",)),
        name=name,
    )(*args)


def _rope_tables(pos):
    inv = ROPE_THETA ** (-jnp.arange(0, QK_ROPE, 2, dtype=F32) / QK_ROPE)
    ang = pos[:, None] * inv[None, :]
    cos, sin = jnp.cos(ang), jnp.sin(ang)
    return jnp.concatenate([cos, cos, cos, cos], axis=1), jnp.concatenate([-sin, sin, -sin, sin], axis=1)


def _w_in_segments():
    segs = []
    for h in range(MLA_HEADS):
        segs.append((h * QK_HEAD, QK_NOPE, SEG_QN + h * QK_NOPE))
        segs.append((h * QK_HEAD + QK_NOPE, QK_ROPE, SEG_QR + h * QK_ROPE))
    o = MLA_HEADS * QK_HEAD
    segs.append((o, KV_LORA, SEG_CKV)); o += KV_LORA
    segs.append((o, QK_ROPE, SEG_MISC)); o += QK_ROPE
    segs.append((o, GDN_QKV_W, SEG_QKV)); o += GDN_QKV_W
    segs.append((o, GDN_HEADS, SEG_MISC + MISC_A)); o += GDN_HEADS
    segs.append((o, GDN_HEADS, SEG_MISC + MISC_B)); o += GDN_HEADS
    segs.append((o, GDN_HEADS * GDN_DV, SEG_Z)); o += GDN_HEADS * GDN_DV
    segs.append((o, D_MODEL, SEG_GA)); o += D_MODEL
    segs.append((o, D_MODEL, SEG_GB))
    return segs


def _reorder_kernel(w_ref, o_ref):
    for src, width, dst in _w_in_segments():
        o_ref[:, dst:dst + width] = w_ref[:, src:src + width].astype(BF16)
    tail = SEG_MISC + MISC_B + GDN_HEADS
    o_ref[:, tail:D_PROJ] = jnp.zeros((o_ref.shape[0], D_PROJ - tail), BF16)


def _reorder_w_in_pallas(w_in):
    d, d_in = w_in.shape
    tr = 128
    return pl.pallas_call(
        _reorder_kernel,
        grid=(d // tr,),
        in_specs=[pl.BlockSpec((tr, d_in), lambda i: (i, 0))],
        out_specs=pl.BlockSpec((tr, D_PROJ), lambda i: (i, 0)),
        out_shape=jax.ShapeDtypeStruct((d, D_PROJ), BF16),
        compiler_params=_cparams(("arbitrary",)),
        name="w_in_reorder",
    )(w_in)


def _lane_vec(v, offset):
    return jnp.zeros((1, LANES), F32).at[0, offset:offset + v.shape[0]].set(v.astype(F32))


def _hier_moe(h, hn, route, w_gate, w_up, w_down, *, n_first):
    t = h.shape[0]
    expert = route[:, ROUTE_E1:ROUTE_E2 + 1].astype(jnp.int32)
    top_w = route[:, ROUTE_W1:ROUTE_W2 + 1]
    n_assign = t * TOP_K
    e_flat = expert.reshape(n_assign)
    onehot = (e_flat[:, None] == jnp.arange(N_EXPERTS, dtype=jnp.int32)[None, :]).astype(jnp.int32)
    csum = jnp.cumsum(onehot, axis=0)
    rank = jnp.take_along_axis(csum, e_flat[:, None], axis=1)[:, 0] - 1
    counts = csum[-1]
    pcounts = (counts + MOE_ROWS - 1) // MOE_ROWS * MOE_ROWS
    pends = jnp.cumsum(pcounts)
    pstarts = pends - pcounts
    dest = pstarts[e_flat] + rank
    n_blocks = -(-(-(-n_assign // MOE_ROWS) + N_EXPERTS) // MOE_SPLITS) * MOE_SPLITS
    n_rows = n_blocks * MOE_ROWS
    tok = jnp.arange(n_assign, dtype=jnp.int32) // TOP_K
    src = jnp.zeros((n_rows,), jnp.int32).at[dest].set(tok)
    bstart = jnp.arange(n_blocks, dtype=jnp.int32) * MOE_ROWS
    bexp = jnp.minimum(jnp.searchsorted(pends, bstart, side='right'), N_EXPERTS - 1).astype(jnp.int32)
    used = (bstart < pends[-1]).astype(jnp.int32)
    bq = n_blocks // MOE_SPLITS
    yb = jnp.zeros((n_rows, D_MODEL), F32)
    for q in range(MOE_SPLITS):
        xb = hn[src[q * bq * MOE_ROWS:(q + 1) * bq * MOE_ROWS]]
        sched = _expert_schedule(bexp[q * bq:(q + 1) * bq], used[q * bq:(q + 1) * bq])
        gate = _grouped_call(_gmm_kernel, sched, [xb], w_gate, name="moe_gate")
        up = _grouped_call(_gmm_kernel, sched, [xb], w_up, name="moe_up")
        yb = _grouped_call(_gmm_down_kernel, sched, [gate, up], w_down, name="moe_down",
                           out_into=yb, block_offset=q * bq)
    dest2 = dest.reshape(t, TOP_K)
    picked = [yb[dest2[:, k]] for k in range(TOP_K)]

    def combine(rows):
        y = h[rows]
        for k in range(TOP_K):
            y = y + picked[k][rows] * top_w[rows, k:k + 1]
        return y

    return combine(slice(0, n_first)), combine(slice(n_first, t))


def kernel(x_prompt, x_sample, cache_ckv, cache_krope, state_ssm, state_conv, page_table, norm_attn, w_in, norm_ckv, w_uk, w_uv, q_gain_nope, q_gain_rope, k_gain_nope, k_gain_rope, conv_w, gdn_a_log, gdn_dt_bias, gdn_norm, w_pa, w_pb, w_o, norm_ffn, w_router_group, b_router_group, w_router_expert, b_router_expert, w_gate, w_up, w_down):
    depth = w_in.shape[0]
    bsz, seq, _ = x_prompt.shape
    n_s, s_new, _ = x_sample.shape
    assert s_new == 1
    n_p = bsz * seq
    n_all = n_p + n_s
    n_past = page_table.shape[1] * PAGE_SIZE
    cos_p, sin_p = _rope_tables(jnp.tile(jnp.arange(seq, dtype=F32), bsz))
    cos_s, sin_s = _rope_tables(jnp.full((n_s,), n_past, F32))
    tq = _tile(seq, 512, LANES)
    xp = x_prompt.reshape(n_p, D_MODEL)
    xs = x_sample.reshape(n_s, D_MODEL)
    outs = [[] for _ in range(8)]
    row = lambda v: v.astype(F32)[None, :]
    for l in range(depth):
        w_in_r = _reorder_w_in_pallas(w_in[l])
        w_uk_b = w_uk[l].astype(BF16)
        w_uvt_b = w_uv[l].T.astype(BF16)
        wukt = w_uk_b.T
        w_pa_b, w_pb_b, w_o_b = w_pa[l].astype(BF16), w_pb[l].astype(BF16), w_o[l].astype(BF16)
        w_rt_b = jnp.concatenate([w_router_group[l], w_router_expert[l],
                                  jnp.zeros((D_MODEL, LANES - N_GROUPS - N_EXPERTS), F32)], axis=1).astype(BF16)
        gains = (row(q_gain_nope[l]), jnp.tile(row(q_gain_rope[l]), (1, 2)), row(k_gain_nope[l]),
                 jnp.tile(row(k_gain_rope[l]), (1, 2)), row(norm_ckv[l]))
        alog_l = _lane_vec(gdn_a_log[l], MISC_A)
        dtb_l = _lane_vec(gdn_dt_bias[l], MISC_A)
        proj_p = _in_proj(xp, row(norm_attn[l]), w_in_r, tm=1024)
        proj_s = _in_proj(xs, row(norm_attn[l]), w_in_r, tm=n_s)

        qcat_p, kcat_p, c_p, kr_p, vt_p = _mla_prep(proj_p, cos_p, sin_p, *gains, w_uk_b, w_uvt_b,
                                                    tm=tq, with_vt=True, with_qg=False)
        oa_p = _flash_attention(qcat_p, kcat_p, vt_p, batch=bsz, seq=seq, tq=tq)

        qcat_s, kcat_s, c_s, kr_s, qg_s = _mla_prep(proj_s, cos_s, sin_s, *gains, w_uk_b, w_uvt_b,
                                                    tm=n_s, with_vt=False, with_qg=True)
        qa = _headwise_matmul(qg_s, wukt, heads=MLA_HEADS, b_rows_by_head=True, name="q_absorb")
        o_lat = _paged_attention(page_table, cache_ckv[l], jnp.swapaxes(cache_krope[l], 1, 2), wukt,
                                 qa.astype(BF16).reshape(n_s, MLA_HEADS, KV_LORA),
                                 qcat_s.reshape(n_s, MLA_HEADS, QK_PAD), kcat_s.reshape(n_s, MLA_HEADS, QK_PAD),
                                 c_s[:, None, :])
        oa_s = _headwise_matmul(o_lat.astype(BF16).reshape(n_s, MLA_HEADS * KV_LORA), w_uv[l].astype(BF16),
                                heads=MLA_HEADS, b_rows_by_head=False, name="v_absorb").astype(BF16)

        qkv_p, kt_p, gates_p = _gdn_prep(proj_p, conv_w[l], alog_l, dtb_l, batch=bsz, seq=seq, tm=512)
        o_g, ssm_p = _delta_rule(qkv_p, kt_p, gates_p, batch=bsz, seq=seq)
        conv_p = proj_p.reshape(bsz, seq, D_PROJ)[:, seq - (CONV_W - 1):, SEG_QKV:SEG_QKV + GDN_QKV_W]

        o_s, ssm_s = _gdn_step(proj_s, state_conv[l], state_ssm[l], conv_w[l], alog_l, dtb_l)
        conv_s = jnp.concatenate([state_conv[l][:, 1:], proj_s[:, None, SEG_QKV:SEG_QKV + GDN_QKV_W]], axis=1)

        m_p = _merge(o_g, oa_p, proj_p, row(gdn_norm[l]), w_pa_b, w_pb_b, tm=512)
        m_s = _merge(o_s, oa_s, proj_s, row(gdn_norm[l]), w_pa_b, w_pb_b, tm=n_s)
        b_rt = _lane_vec(jnp.concatenate([b_router_group[l], b_router_expert[l]]), 0)
        h_all, hn_all, route = _out_proj(m_p, xp, m_s, xs, w_o_b, row(norm_ffn[l]), w_rt_b, b_rt, tm=512)
        xp, xs = _hier_moe(h_all, hn_all, route, w_gate[l], w_up[l], w_down[l], n_first=n_p)

        for lst, val in zip(outs, (c_p.reshape(bsz, seq, KV_LORA), kr_p[:, :QK_ROPE].reshape(bsz, seq, QK_ROPE),
                                   c_s.reshape(n_s, 1, KV_LORA), kr_s[:, :QK_ROPE].reshape(n_s, 1, QK_ROPE),
                                   ssm_p, conv_p, ssm_s, conv_s)):
            lst.append(val)
    return (xp.reshape(bsz, seq, D_MODEL), xs.reshape(n_s, 1, D_MODEL)) + tuple(jnp.stack(o) for o in outs)
```

```python
import functools

import jax
import jax.numpy as jnp
from jax import lax
from jax.experimental import pallas as pl
from jax.experimental.pallas import tpu as pltpu

F32 = jnp.float32
BF16 = jnp.bfloat16

D_MODEL = 2048
PAGE_SIZE = 128
MLA_HEADS = 16
QK_NOPE = 128
QK_ROPE = 64
QK_HEAD = QK_NOPE + QK_ROPE
QK_PAD = 256
V_HEAD = 128
KV_LORA = 512
ROPE_THETA = 10000.0
GDN_HEADS = 16
GDN_DK = 128
GDN_DV = 128
GDN_HK = GDN_HEADS * GDN_DK
GDN_QKV_W = GDN_HEADS * (2 * GDN_DK + GDN_DV)
CONV_W = 4
GDN_CHUNK = 128
N_GROUPS = 8
EXP_PER_GROUP = 8
N_EXPERTS = N_GROUPS * EXP_PER_GROUP
TOP_K = 2
D_EXPERT = 1408
MOE_ROWS = 128
MOE_SPLITS = 1
WEIGHT_DMA_PARTS = 2
WEIGHT_SLOTS = 3
EPS = 1e-6
LANES = 128
SUBLANES = 8

V7X_VMEM_LIMIT = 56 * 1024 * 1024

SEG_Z = 0
SEG_GA = SEG_Z + GDN_HEADS * GDN_DV
SEG_GB = SEG_GA + D_MODEL
SEG_QKV = SEG_GB + D_MODEL
SEG_QN = SEG_QKV + GDN_QKV_W
SEG_QR = SEG_QN + MLA_HEADS * QK_NOPE
SEG_CKV = SEG_QR + MLA_HEADS * QK_ROPE
SEG_MISC = SEG_CKV + KV_LORA
PROJ_TN = 1024
D_PROJ = -(-(SEG_MISC + LANES) // PROJ_TN) * PROJ_TN
MISC_A = QK_ROPE
MISC_B = QK_ROPE + GDN_HEADS


def _cparams(sem, vmem=V7X_VMEM_LIMIT):
    return pltpu.CompilerParams(dimension_semantics=sem, vmem_limit_bytes=vmem)


def _tile(dim, cap, align):
    if dim <= cap:
        return dim
    for t in range(cap - cap % align, 0, -align):
        if dim % t == 0:
            return t
    raise ValueError(f"no {align}-aligned tile <= {cap} divides {dim}")


def _nt_dot(a, b):
    return lax.dot_general(a, b, (((1,), (1,)), ((), ())), preferred_element_type=F32)


def _rms_rows(x, gain):
    return x * lax.rsqrt(jnp.mean(x * x, axis=-1, keepdims=True) + EPS) * gain


def _sigmoid(x):
    return 1.0 / (1.0 + jnp.exp(-x))


def _in_proj_kernel(x_ref, g_ref, w_ref, o_ref, u_ref):
    @pl.when(pl.program_id(1) == 0)
    def _():
        u_ref[...] = _rms_rows(x_ref[...], g_ref[...]).astype(BF16)

    o_ref[...] = jnp.dot(u_ref[...], w_ref[...], preferred_element_type=F32)


def _in_proj(x, gain, w, *, tm):
    m, d = x.shape
    n = w.shape[1]
    tm = _tile(m, tm, 16)
    tn = PROJ_TN
    assert n % tn == 0
    return pl.pallas_call(
        _in_proj_kernel,
        grid=(m // tm, n // tn),
        in_specs=[pl.BlockSpec((tm, d), lambda i, j: (i, 0)),
                  pl.BlockSpec((1, d), lambda i, j: (0, 0)),
                  pl.BlockSpec((d, tn), lambda i, j: (0, j))],
        out_specs=pl.BlockSpec((tm, tn), lambda i, j: (i, j)),
        out_shape=jax.ShapeDtypeStruct((m, n), F32),
        scratch_shapes=[pltpu.VMEM((tm, d), BF16)],
        compiler_params=_cparams(("arbitrary", "arbitrary")),
        name="in_proj",
    )(x, gain, w)


def _mm_kernel(a_ref, b_ref, o_ref):
    o_ref[...] = jnp.dot(a_ref[...], b_ref[...], preferred_element_type=F32).astype(o_ref.dtype)


def _headwise_matmul(a, b, *, heads, b_rows_by_head, name):
    m = a.shape[0]
    k = a.shape[1] // heads
    if b_rows_by_head:
        n = b.shape[1]
        b_spec = pl.BlockSpec((k, n), lambda h: (h, 0))
    else:
        n = b.shape[1] // heads
        b_spec = pl.BlockSpec((k, n), lambda h: (0, h))
    return pl.pallas_call(
        _mm_kernel,
        grid=(heads,),
        in_specs=[pl.BlockSpec((m, k), lambda h: (0, h)), b_spec],
        out_specs=pl.BlockSpec((m, n), lambda h: (0, h)),
        out_shape=jax.ShapeDtypeStruct((m, heads * n), F32),
        compiler_params=_cparams(("arbitrary",)),
        name=name,
    )(a, b)


def _swap_halves(y, lane):
    half = QK_ROPE // 2
    return jnp.where(lane % QK_ROPE < half, pltpu.roll(y, LANES - half, axis=1), pltpu.roll(y, half, axis=1))


def _rope_pair(x, gain2, cos4, sin4, lane):
    lo = lane < QK_ROPE
    ss = x * x
    s_lo = jnp.sum(jnp.where(lo, ss, 0.0), axis=-1, keepdims=True)
    s_hi = jnp.sum(jnp.where(lo, 0.0, ss), axis=-1, keepdims=True)
    r = jnp.where(lo, lax.rsqrt(s_lo * (1.0 / QK_ROPE) + EPS), lax.rsqrt(s_hi * (1.0 / QK_ROPE) + EPS))
    y = x * r * gain2
    return y * cos4 + _swap_halves(y, lane) * sin4


def _mla_prep_kernel(qn_ref, qr_ref, ckv_ref, misc_ref, cos_ref, sin_ref, gqn_ref, gqr_ref, gkn_ref, gkr_ref,
                     gc_ref, wuk_ref, *rest, with_vt, with_qg):
    rest = list(rest)
    wuvt_ref = rest.pop(0) if with_vt else None
    qcat_ref, kcat_ref, c_ref, kr_ref = rest[:4]
    rest = rest[4:]
    vt_ref = rest.pop(0) if with_vt else None
    qg_ref = rest.pop(0) if with_qg else None
    rows = qn_ref.shape[0]
    lane = lax.broadcasted_iota(jnp.int32, (rows, LANES), 1)
    lo = lane < QK_ROPE
    cos4 = cos_ref[...]
    sin4 = sin_ref[...]

    c = _rms_rows(ckv_ref[...], gc_ref[...])
    c_ref[...] = c
    cb = c.astype(BF16)
    kr = jnp.where(lo, _rope_pair(misc_ref[...], gkr_ref[...], cos4, sin4, lane), 0.0)
    kr_ref[...] = kr
    krb = kr.astype(BF16)

    gqn = gqn_ref[...]
    gkn = gkn_ref[...]
    for h in range(MLA_HEADS):
        qn = _rms_rows(qn_ref[:, h * QK_NOPE:(h + 1) * QK_NOPE], gqn)
        qcat_ref[:, h * QK_PAD:h * QK_PAD + QK_NOPE] = qn.astype(BF16)
        if with_qg:
            qg_ref[:, h * QK_NOPE:(h + 1) * QK_NOPE] = (qn * gkn).astype(BF16)
    gqr = gqr_ref[...]
    for j in range(MLA_HEADS // 2):
        rot = _rope_pair(qr_ref[:, j * LANES:(j + 1) * LANES], gqr, cos4, sin4, lane)
        even = jnp.where(lo, rot, 0.0)
        odd = jnp.where(lo, pltpu.roll(rot, QK_ROPE, axis=1), 0.0)
        qcat_ref[:, (2 * j) * QK_PAD + QK_NOPE:(2 * j + 1) * QK_PAD] = even.astype(BF16)
        qcat_ref[:, (2 * j + 1) * QK_PAD + QK_NOPE:(2 * j + 2) * QK_PAD] = odd.astype(BF16)

    kn = jnp.dot(cb, wuk_ref[...], preferred_element_type=F32)
    for h in range(MLA_HEADS):
        kh = _rms_rows(kn[:, h * QK_NOPE:(h + 1) * QK_NOPE], gkn)
        kcat_ref[:, h * QK_PAD:h * QK_PAD + QK_NOPE] = kh.astype(BF16)
        kcat_ref[:, h * QK_PAD + QK_NOPE:(h + 1) * QK_PAD] = krb
    if with_vt:
        vt_ref[...] = _nt_dot(wuvt_ref[...], cb).astype(BF16)


def _mla_prep(proj, cos4, sin4, gqn, gqr2, gkn, gkr2, gckv, w_uk_b, w_uvt_b, *, tm, with_vt, with_qg):
    n = proj.shape[0]
    tm = _tile(n, tm, 16)
    nt = n // tm
    hq = MLA_HEADS * QK_NOPE
    row = lambda w, off: pl.BlockSpec((tm, w), lambda i, off=off, w=w: (i, off // w))
    const = lambda a: pl.BlockSpec(a.shape, lambda i: (0,) * a.ndim)
    in_specs = [row(hq, SEG_QN), row(MLA_HEADS * QK_ROPE, SEG_QR), row(KV_LORA, SEG_CKV), row(LANES, SEG_MISC),
                pl.BlockSpec((tm, LANES), lambda i: (i, 0)), pl.BlockSpec((tm, LANES), lambda i: (i, 0)),
                const(gqn), const(gqr2), const(gkn), const(gkr2), const(gckv), const(w_uk_b)]
    args = [proj, proj, proj, proj, cos4, sin4, gqn, gqr2, gkn, gkr2, gckv, w_uk_b]
    if with_vt:
        in_specs.append(const(w_uvt_b))
        args.append(w_uvt_b)
    out_specs = [pl.BlockSpec((tm, MLA_HEADS * QK_PAD), lambda i: (i, 0)),
                 pl.BlockSpec((tm, MLA_HEADS * QK_PAD), lambda i: (i, 0)),
                 pl.BlockSpec((tm, KV_LORA), lambda i: (i, 0)),
                 pl.BlockSpec((tm, LANES), lambda i: (i, 0))]
    out_shape = [jax.ShapeDtypeStruct((n, MLA_HEADS * QK_PAD), BF16),
                 jax.ShapeDtypeStruct((n, MLA_HEADS * QK_PAD), BF16),
                 jax.ShapeDtypeStruct((n, KV_LORA), F32),
                 jax.ShapeDtypeStruct((n, LANES), F32)]
    if with_vt:
        out_specs.append(pl.BlockSpec((None, MLA_HEADS * V_HEAD, tm), lambda i: (i, 0, 0)))
        out_shape.append(jax.ShapeDtypeStruct((nt, MLA_HEADS * V_HEAD, tm), BF16))
    if with_qg:
        out_specs.append(pl.BlockSpec((tm, hq), lambda i: (i, 0)))
        out_shape.append(jax.ShapeDtypeStruct((n, hq), BF16))
    return pl.pallas_call(
        functools.partial(_mla_prep_kernel, with_vt=with_vt, with_qg=with_qg),
        grid=(nt,), in_specs=in_specs, out_specs=out_specs, out_shape=out_shape,
        compiler_params=_cparams(("arbitrary",)),
        name="mla_prep",
    )(*args)


def _flash_kernel(q_ref, k_ref, vt_ref, o_ref, m_ref, l_ref, acc_ref, *, tq, scale):
    qi = pl.program_id(2)
    q = q_ref[...]
    m_ref[...] = jnp.full(m_ref.shape, -jnp.inf, F32)
    l_ref[...] = jnp.zeros(l_ref.shape, F32)
    acc_ref[...] = jnp.zeros(acc_ref.shape, F32)

    c2 = scale * 1.4426950408889634

    def block(ki, nblk, masked):
        start = pl.multiple_of(ki * tq, tq)
        k = k_ref[pl.ds(start, nblk * tq), :]
        s = _nt_dot(k, q)
        if masked:
            key = lax.broadcasted_iota(jnp.int32, s.shape, 0)
            qry = lax.broadcasted_iota(jnp.int32, s.shape, 1)
            s = jnp.where(key <= qry, s, -jnp.inf)
        m_old = m_ref[...]
        m_new = jnp.maximum(m_old, jnp.max(s, axis=0, keepdims=True))
        alpha = jnp.exp2((m_old - m_new) * c2)
        p32 = jnp.exp2((s - m_new) * c2)
        l_ref[...] = l_ref[...] * alpha + jnp.sum(p32, axis=0, keepdims=True)
        p = p32.astype(BF16)
        pv = jnp.dot(vt_ref[ki], p[0:tq], preferred_element_type=F32)
        for j in range(1, nblk):
            pv = pv + jnp.dot(vt_ref[ki + j], p[j * tq:(j + 1) * tq], preferred_element_type=F32)
        acc_ref[...] = acc_ref[...] * alpha + pv
        m_ref[...] = m_new

    def body(kp, carry):
        block(2 * kp, 2, False)
        return carry

    lax.fori_loop(0, qi // 2, body, 0)

    @pl.when(qi % 2 == 1)
    def _():
        block(qi - 1, 1, False)

    block(qi, 1, True)
    o_ref[...] = (acc_ref[...] / l_ref[...]).T.astype(o_ref.dtype)


def _flash_attention(q_cat, k_cat, vt, *, batch, seq, tq):
    nq = seq // tq
    kern = functools.partial(_flash_kernel, tq=tq, scale=QK_HEAD ** -0.5)
    return pl.pallas_call(
        kern,
        grid=(batch, MLA_HEADS, nq),
        in_specs=[pl.BlockSpec((tq, QK_PAD), lambda b, h, i: (b * nq + i, h)),
                  pl.BlockSpec((seq, QK_PAD), lambda b, h, i: (b, h)),
                  pl.BlockSpec((nq, V_HEAD, tq), lambda b, h, i: (b, h, 0))],
        out_specs=pl.BlockSpec((tq, V_HEAD), lambda b, h, i: (b * nq + i, h)),
        out_shape=jax.ShapeDtypeStruct((batch * seq, MLA_HEADS * V_HEAD), BF16),
        scratch_shapes=[pltpu.VMEM((1, tq), F32), pltpu.VMEM((1, tq), F32), pltpu.VMEM((V_HEAD, tq), F32)],
        compiler_params=_cparams(("arbitrary", "arbitrary", "arbitrary")),
        name="mla_prompt_attention",
    )(q_cat, k_cat, vt)


PAGES_PER_STEP = 16
PAGE_RING = 3
PAGE_GROUPS = 2


def _paged_kernel(pt_ref, ckv_hbm, krt_hbm, wukt_ref, qa_ref, qcat_ref, qcat_prev_ref, kcat_prev_ref, cn_prev_ref,
                  o_ref, lhs_ref, cbuf, krbuf, csem, krsem, c_scr, krt_scr, s_scr, m_ref, l_ref, acc_ref,
                  *, scale, steps, n_chunks):
    npg = PAGES_PER_STEP
    t = pl.program_id(0)
    n_hd = MLA_HEADS * QK_NOPE
    span = npg * PAGE_SIZE // PAGE_GROUPS
    prev_chunk = (t + steps - 1) % steps

    def page_copies(chunk, slot, resolve_pages):
        base = jnp.minimum(chunk, n_chunks - 1) * npg
        copies = []
        for i in range(npg):
            page = pt_ref[base + i] if resolve_pages else 0
            copies.append(pltpu.make_async_copy(ckv_hbm.at[page], cbuf.at[slot, i], csem.at[slot]))
            copies.append(pltpu.make_async_copy(krt_hbm.at[page], krbuf.at[slot, i], krsem.at[slot]))
        return copies

    @pl.when(t == 0)
    def _():
        for chunk in range(PAGE_RING - 1):
            for cp in page_copies(chunk, chunk, True):
                cp.start()
        lhs_ref[0:n_hd, :] = wukt_ref[...]
        c_scr[1] = jnp.zeros(c_scr.shape[1:], BF16)
        s_scr[1] = jnp.zeros(s_scr.shape[1:], F32)
        m_ref[...] = jnp.zeros(m_ref.shape, F32)
        l_ref[...] = jnp.zeros(l_ref.shape, F32)
        acc_ref[...] = jnp.zeros(acc_ref.shape, F32)

    @pl.when(t % steps == 0)
    def _():
        lhs_ref[n_hd:n_hd + MLA_HEADS, :] = qa_ref[...]

    ring = t % PAGE_RING

    def step(cur, prv):
        for cp in page_copies(t, ring, False):
            cp.wait()
        first = prev_chunk == 0
        s_prev = s_scr[prv]
        m_old = jnp.where(first, -jnp.inf, m_ref[...])
        l_old = jnp.where(first, 0.0, l_ref[...])
        acc_old = jnp.where(first, 0.0, acc_ref[...])
        m_new = jnp.maximum(m_old, jnp.max(s_prev, axis=-1, keepdims=True))
        alpha = jnp.exp(m_old - m_new)
        pr = jnp.exp(s_prev - m_new)
        l_ref[...] = l_old * alpha + jnp.sum(pr, axis=-1, keepdims=True)
        acc_ref[...] = acc_old * alpha + jnp.dot(pr.astype(BF16), c_scr[prv], preferred_element_type=F32)
        m_ref[...] = m_new
        for i in range(npg):
            c_scr[cur, i * PAGE_SIZE:(i + 1) * PAGE_SIZE, :] = cbuf[ring, i].astype(BF16)
            krt_scr[:, i * PAGE_SIZE:(i + 1) * PAGE_SIZE] = krbuf[ring, i].astype(BF16)
        qr = qcat_ref[:, QK_NOPE:QK_NOPE + QK_ROPE]
        for g in range(PAGE_GROUPS):
            c = c_scr[cur, g * span:(g + 1) * span, :]
            res = _nt_dot(lhs_ref[...], c)
            kt = res[0:n_hd, :].reshape(MLA_HEADS, QK_NOPE, span)
            inv = lax.rsqrt(jnp.sum(kt * kt, axis=1) * (1.0 / QK_NOPE) + EPS)
            s_rope = jnp.dot(qr, krt_scr[:, g * span:(g + 1) * span], preferred_element_type=F32)
            s_scr[cur, :, g * span:(g + 1) * span] = (res[n_hd:n_hd + MLA_HEADS, :] * inv + s_rope) * scale
        for cp in page_copies(t + PAGE_RING - 1, (t + PAGE_RING - 1) % PAGE_RING, True):
            cp.start()

    @pl.when(t % 2 == 0)
    def _():
        step(0, 1)

    @pl.when(t % 2 == 1)
    def _():
        step(1, 0)

    @pl.when((prev_chunk == steps - 1) & (t > 0))
    def _():
        s_new = jnp.sum(qcat_prev_ref[...].astype(F32) * kcat_prev_ref[...].astype(F32), axis=-1,
                        keepdims=True) * scale
        m_old = m_ref[...]
        m_fin = jnp.maximum(m_old, s_new)
        a2 = jnp.exp(m_old - m_fin)
        p_new = jnp.exp(s_new - m_fin)
        l_fin = l_ref[...] * a2 + p_new
        cn = cn_prev_ref[...].astype(BF16).astype(F32)
        o_ref[...] = (acc_ref[...] * a2 + p_new * cn) / l_fin

    @pl.when(t == n_chunks)
    def _():
        for ahead in range(1, PAGE_RING):
            for cp in page_copies(t + ahead, (t + ahead) % PAGE_RING, False):
                cp.wait()


def _paged_attention(page_table, ckv_pool, krt_pool, wukt, qa, q_cat, k_cat, cn):
    n_s, n_pages = page_table.shape
    npg = PAGES_PER_STEP
    assert n_pages % npg == 0
    steps = n_pages // npg
    n_chunks = n_s * steps
    span = npg * PAGE_SIZE
    pt_flat = page_table.reshape(n_s * n_pages)

    def cur_sample(shape):
        return pl.BlockSpec((None,) + shape, lambda t, pt: (jnp.minimum(t // steps, n_s - 1), 0, 0))

    def prev_sample(shape):
        return pl.BlockSpec((None,) + shape, lambda t, pt: (jnp.maximum(t - 1, 0) // steps, 0, 0))

    in_specs = [pl.BlockSpec(memory_space=pl.ANY), pl.BlockSpec(memory_space=pl.ANY),
                pl.BlockSpec(wukt.shape, lambda t, pt: (0, 0)),
                cur_sample((MLA_HEADS, KV_LORA)), cur_sample((MLA_HEADS, QK_PAD)),
                prev_sample((MLA_HEADS, QK_PAD)), prev_sample((MLA_HEADS, QK_PAD)), prev_sample((1, KV_LORA))]
    grid_spec = pltpu.PrefetchScalarGridSpec(
        num_scalar_prefetch=1,
        grid=(n_chunks + 1,),
        in_specs=in_specs,
        out_specs=prev_sample((MLA_HEADS, KV_LORA)),
        scratch_shapes=[pltpu.VMEM((MLA_HEADS * QK_NOPE + MLA_HEADS, KV_LORA), BF16),
                        pltpu.VMEM((PAGE_RING, npg, PAGE_SIZE, KV_LORA), F32),
                        pltpu.VMEM((PAGE_RING, npg, QK_ROPE, PAGE_SIZE), F32),
                        pltpu.SemaphoreType.DMA((PAGE_RING,)), pltpu.SemaphoreType.DMA((PAGE_RING,)),
                        pltpu.VMEM((2, span, KV_LORA), BF16), pltpu.VMEM((QK_ROPE, span), BF16),
                        pltpu.VMEM((2, MLA_HEADS, span), F32),
                        pltpu.VMEM((MLA_HEADS, 1), F32), pltpu.VMEM((MLA_HEADS, 1), F32),
                        pltpu.VMEM((MLA_HEADS, KV_LORA), F32)])
    return pl.pallas_call(
        functools.partial(_paged_kernel, scale=QK_HEAD ** -0.5, steps=steps, n_chunks=n_chunks),
        grid_spec=grid_spec,
        out_shape=jax.ShapeDtypeStruct((n_s, MLA_HEADS, KV_LORA), F32),
        compiler_params=_cparams(("arbitrary",)),
        name="mla_sample_attention",
    )(pt_flat, ckv_pool, krt_pool, wukt, qa, q_cat, q_cat, k_cat, cn)


def _gdn_prep_kernel(x_ref, halo_ref, misc_ref, cw_ref, alog_ref, dtb_ref, qkv_ref, kt_ref, gates_ref, xs_ref,
                     *, tiles_per_seq):
    i = pl.program_id(0)
    s = pl.program_id(1)
    tm = x_ref.shape[0]
    first = (i % tiles_per_seq) == 0

    @pl.when(first)
    def _():
        xs_ref[0:SUBLANES, :] = jnp.zeros((SUBLANES, xs_ref.shape[1]), F32)

    @pl.when(jnp.logical_not(first))
    def _():
        xs_ref[0:SUBLANES, :] = halo_ref[...]

    xs_ref[SUBLANES:SUBLANES + tm, :] = x_ref[...]
    acc = xs_ref[pl.ds(SUBLANES - (CONV_W - 1), tm), :] * cw_ref[0:1, :]
    for w in range(1, CONV_W):
        acc = acc + xs_ref[pl.ds(SUBLANES - (CONV_W - 1) + w, tm), :] * cw_ref[w:w + 1, :]
    y = acc * _sigmoid(acc)

    @pl.when(s == 2)
    def _():
        qkv_ref[...] = y

    @pl.when(s < 2)
    def _():
        scale = jnp.where(s == 0, GDN_DK ** -0.5, 1.0).astype(F32)
        for h in range(GDN_HEADS):
            yh = y[:, h * GDN_DK:(h + 1) * GDN_DK]
            yn = yh * (lax.rsqrt(jnp.sum(yh * yh, axis=-1, keepdims=True) + EPS) * scale)
            qkv_ref[:, h * GDN_DK:(h + 1) * GDN_DK] = yn

    @pl.when(s == 1)
    def _():
        kt_ref[...] = qkv_ref[...].T

    @pl.when(s == 0)
    def _():
        x = misc_ref[...] + dtb_ref[...]
        softplus = jnp.maximum(x, 0.0) + jnp.log(1.0 + jnp.exp(-jnp.abs(x)))
        lane = lax.broadcasted_iota(jnp.int32, x.shape, 1)
        is_beta = (lane >= MISC_B) & (lane < MISC_B + GDN_HEADS)
        gates_ref[...] = jnp.where(is_beta, _sigmoid(misc_ref[...]), -jnp.exp(alog_ref[...]) * softplus)


def _gdn_prep(proj, conv_w, alog_l, dtb_l, *, batch, seq, tm):
    n = batch * seq
    tm = _tile(seq, tm, LANES)
    tps = seq // tm
    sec0 = SEG_QKV // GDN_HK
    halo_blocks = tm // SUBLANES
    return pl.pallas_call(
        functools.partial(_gdn_prep_kernel, tiles_per_seq=tps),
        grid=(n // tm, 3),
        in_specs=[pl.BlockSpec((tm, GDN_HK), lambda i, s: (i, sec0 + s)),
                  pl.BlockSpec((SUBLANES, GDN_HK), lambda i, s: (jnp.maximum(i * halo_blocks - 1, 0), sec0 + s)),
                  pl.BlockSpec((tm, LANES), lambda i, s: (i, SEG_MISC // LANES)),
                  pl.BlockSpec((CONV_W, GDN_HK), lambda i, s: (0, s)),
                  pl.BlockSpec((1, LANES), lambda i, s: (0, 0)),
                  pl.BlockSpec((1, LANES), lambda i, s: (0, 0))],
        out_specs=[pl.BlockSpec((tm, GDN_HK), lambda i, s: (i, s)),
                   pl.BlockSpec((None, GDN_HK, tm), lambda i, s: (i // tps, 0, i % tps)),
                   pl.BlockSpec((tm, LANES), lambda i, s: (i, 0))],
        out_shape=[jax.ShapeDtypeStruct((n, 3 * GDN_HK), F32),
                   jax.ShapeDtypeStruct((batch, GDN_HK, seq), F32),
                   jax.ShapeDtypeStruct((n, LANES), F32)],
        scratch_shapes=[pltpu.VMEM((tm + SUBLANES, GDN_HK), F32)],
        compiler_params=_cparams(("arbitrary", "arbitrary")),
        name="gdn_prep",
    )(proj, proj, proj, conv_w, alog_l, dtb_l)


def _split3(x):
    hi = x.astype(BF16)
    r1 = x - hi.astype(F32)
    mid = r1.astype(BF16)
    lo = (r1 - mid.astype(F32)).astype(BF16)
    return hi, mid, lo


def _bdot(a, b):
    return jnp.dot(a.astype(BF16), b.astype(BF16), preferred_element_type=F32)


def _delta_kernel(q_ref, k_ref, v_ref, kt_ref, gates_ref, o_ref, s_out_ref, s_ref):
    c = GDN_CHUNK
    n = pl.program_id(1)
    heads = range(GDN_HEADS)

    @pl.when(n == 0)
    def _():
        s_ref[...] = jnp.zeros(s_ref.shape, F32)

    row = lax.broadcasted_iota(jnp.int32, (c, c), 0)
    col = lax.broadcasted_iota(jnp.int32, (c, c), 1)
    incl = row >= col
    strict = row > col
    eye = (row == col).astype(F32)
    tril = incl.astype(BF16)
    triu = (row <= col).astype(BF16)
    gates = gates_ref[...]
    gates_t = gates.T
    gc_all = sum(jnp.dot(tril, part, preferred_element_type=F32) for part in _split3(gates))
    gct_all = sum(jnp.dot(part, triu, preferred_element_type=F32) for part in _split3(gates_t))

    sl = [slice(h * GDN_DK, (h + 1) * GDN_DK) for h in heads]
    gcol = [gc_all[:, MISC_A + h:MISC_A + h + 1] for h in heads]
    grow = [gct_all[MISC_A + h:MISC_A + h + 1, :] for h in heads]
    glast = [gct_all[MISC_A + h:MISC_A + h + 1, c - 1:c] for h in heads]
    bcol = [gates[:, MISC_B + h:MISC_B + h + 1] for h in heads]
    eg = [jnp.exp(g) for g in gcol]
    e = [jnp.exp(jnp.where(incl, gcol[h] - grow[h], 0.0)) for h in heads]
    kb = [k_ref[:, sl[h]] * bcol[h] for h in heads]
    ktb = [kt_ref[sl[h], :].astype(BF16) for h in heads]
    x = [-(jnp.dot(kb[h].astype(BF16), ktb[h], preferred_element_type=F32) * jnp.where(strict, e[h], 0.0))
         for h in heads]
    t = [eye + x[h] for h in heads]
    for _ in range(6):
        x = [_bdot(x[h], x[h]) for h in heads]
        t = [t[h] + _bdot(t[h], x[h]) for h in heads]
    rhs = [jnp.concatenate([v_ref[:, sl[h]] * bcol[h], kb[h] * eg[h]], axis=1) for h in heads]
    sol = [_bdot(t[h], rhs[h]) for h in heads]
    attn = [jnp.dot(q_ref[:, sl[h]].astype(BF16), ktb[h], preferred_element_type=F32) * jnp.where(incl, e[h], 0.0)
            for h in heads]
    s_old = [s_ref[h] for h in heads]
    lhs = [jnp.concatenate([sol[h][:, GDN_DV:], q_ref[:, sl[h]] * eg[h]], axis=0) for h in heads]
    prod = [_bdot(lhs[h], s_old[h]) for h in heads]
    v_new = [sol[h][:, :GDN_DV] - prod[h][:c] for h in heads]
    for h in heads:
        o_ref[:, sl[h]] = prod[h][c:] + _bdot(attn[h], v_new[h])
    for h in heads:
        kdt = kt_ref[sl[h], :] * jnp.exp(glast[h] - grow[h])
        s_ref[h] = s_old[h] * jnp.exp(glast[h]) + _bdot(kdt, v_new[h])

    @pl.when(n == pl.num_programs(1) - 1)
    def _():
        s_out_ref[...] = s_ref[...]


def _delta_rule(qkv, kt, gates, *, batch, seq):
    c = GDN_CHUNK
    assert seq % c == 0
    nc = seq // c
    sec = lambda s: pl.BlockSpec((c, GDN_HK), lambda b, n, s=s: (b * nc + n, s))
    return pl.pallas_call(
        _delta_kernel,
        grid=(batch, nc),
        in_specs=[sec(0), sec(1), sec(2),
                  pl.BlockSpec((None, GDN_HK, c), lambda b, n: (b, 0, n)),
                  pl.BlockSpec((c, LANES), lambda b, n: (b * nc + n, 0))],
        out_specs=[pl.BlockSpec((c, GDN_HK), lambda b, n: (b * nc + n, 0)),
                   pl.BlockSpec((None, GDN_HEADS, GDN_DK, GDN_DV), lambda b, n: (b, 0, 0, 0))],
        out_shape=[jax.ShapeDtypeStruct((batch * seq, GDN_HK), F32),
                   jax.ShapeDtypeStruct((batch, GDN_HEADS, GDN_DK, GDN_DV), F32)],
        scratch_shapes=[pltpu.VMEM((GDN_HEADS, GDN_DK, GDN_DV), F32)],
        compiler_params=_cparams(("arbitrary", "arbitrary")),
        name="gdn_delta_rule",
    )(qkv, qkv, qkv, kt, gates)


GDN_STEP_ROWS = SUBLANES


def _gdn_step_kernel(x_ref, buf_ref, misc_ref, cw_ref, alog_ref, dtb_ref, s_ref, o_ref, s_out_ref):
    nb = x_ref.shape[0]
    acc = x_ref[...] * cw_ref[CONV_W - 1:CONV_W, :]
    for w in range(CONV_W - 1):
        acc = acc + buf_ref[:, w, :] * cw_ref[w:w + 1, :]
    y = acc * _sigmoid(acc)
    x = misc_ref[...] + dtb_ref[...]
    softplus = jnp.maximum(x, 0.0) + jnp.log(1.0 + jnp.exp(-jnp.abs(x)))
    eg_all = jnp.exp(-jnp.exp(alog_ref[...]) * softplus)
    beta_all = _sigmoid(misc_ref[...])
    for h in range(GDN_HEADS):
        qs = slice(h * GDN_DK, (h + 1) * GDN_DK)
        ks = slice(GDN_HK + h * GDN_DK, GDN_HK + (h + 1) * GDN_DK)
        vs = slice(2 * GDN_HK + h * GDN_DV, 2 * GDN_HK + (h + 1) * GDN_DV)
        q = y[:, qs]
        k = y[:, ks]
        q = q * (lax.rsqrt(jnp.sum(q * q, axis=-1, keepdims=True) + EPS) * (GDN_DK ** -0.5))
        k = k * lax.rsqrt(jnp.sum(k * k, axis=-1, keepdims=True) + EPS)
        eg = eg_all[:, MISC_A + h:MISC_A + h + 1]
        beta = beta_all[:, MISC_B + h:MISC_B + h + 1]
        vb = y[:, vs] * beta
        kcd = k * (beta * eg)
        qd = q * eg
        qk = jnp.sum(q * k, axis=-1, keepdims=True)
        kt = k.T
        for b in range(nb):
            s_old = s_ref[b, h]
            lhs = jnp.concatenate([kcd[b:b + 1], qd[b:b + 1], jnp.zeros((SUBLANES - 2, GDN_DK), F32)], axis=0)
            prod = jnp.dot(lhs.astype(BF16), s_old.astype(BF16), preferred_element_type=F32)
            v_new = vb[b:b + 1] - prod[0:1]
            o_ref[b:b + 1, qs] = prod[1:2] + qk[b:b + 1] * v_new
            s_out_ref[b, h] = s_old * eg[b:b + 1] + kt[:, b:b + 1] * v_new


def _gdn_step(proj, state_conv, state_ssm, conv_w, alog_l, dtb_l):
    n_s = proj.shape[0]
    nb = min(GDN_STEP_ROWS, n_s)
    assert n_s % nb == 0
    return pl.pallas_call(
        _gdn_step_kernel,
        grid=(n_s // nb,),
        in_specs=[pl.BlockSpec((nb, GDN_QKV_W), lambda i: (i, SEG_QKV // GDN_QKV_W)),
                  pl.BlockSpec((nb, CONV_W - 1, GDN_QKV_W), lambda i: (i, 0, 0)),
                  pl.BlockSpec((nb, LANES), lambda i: (i, SEG_MISC // LANES)),
                  pl.BlockSpec((CONV_W, GDN_QKV_W), lambda i: (0, 0)),
                  pl.BlockSpec((1, LANES), lambda i: (0, 0)),
                  pl.BlockSpec((1, LANES), lambda i: (0, 0)),
                  pl.BlockSpec((nb, GDN_HEADS, GDN_DK, GDN_DV), lambda i: (i, 0, 0, 0))],
        out_specs=[pl.BlockSpec((nb, GDN_HK), lambda i: (i, 0)),
                   pl.BlockSpec((nb, GDN_HEADS, GDN_DK, GDN_DV), lambda i: (i, 0, 0, 0))],
        out_shape=[jax.ShapeDtypeStruct((n_s, GDN_HK), F32),
                   jax.ShapeDtypeStruct(state_ssm.shape, F32)],
        compiler_params=_cparams(("arbitrary",)),
        name="gdn_step",
    )(proj, state_conv, proj, conv_w, alog_l, dtb_l, state_ssm)


def _merge_kernel(og_ref, z_ref, oa_ref, ga_ref, gb_ref, gn_ref, wpa_ref, wpb_ref, m_ref, ob_ref):
    @pl.when(pl.program_id(1) == 0)
    def _():
        gn = gn_ref[...]
        for h in range(GDN_HEADS):
            sl = slice(h * GDN_DV, (h + 1) * GDN_DV)
            z = z_ref[:, sl]
            ob_ref[:, sl] = (_rms_rows(og_ref[:, sl], gn) * (z * _sigmoid(z))).astype(BF16)

    pa = jnp.dot(oa_ref[...], wpa_ref[...], preferred_element_type=F32)
    pb = jnp.dot(ob_ref[...], wpb_ref[...], preferred_element_type=F32)
    m_ref[...] = (_sigmoid(ga_ref[...]) * pa + _sigmoid(gb_ref[...]) * pb).astype(BF16)


def _merge(o_g, oa, proj, gdn_norm, w_pa_b, w_pb_b, *, tm):
    n = o_g.shape[0]
    tm = _tile(n, tm, 16)
    tn = 1024
    nj = D_MODEL // tn
    return pl.pallas_call(
        _merge_kernel,
        grid=(n // tm, nj),
        in_specs=[pl.BlockSpec((tm, GDN_HK), lambda i, j: (i, 0)),
                  pl.BlockSpec((tm, GDN_HK), lambda i, j: (i, SEG_Z // GDN_HK)),
                  pl.BlockSpec((tm, MLA_HEADS * V_HEAD), lambda i, j: (i, 0)),
                  pl.BlockSpec((tm, tn), lambda i, j: (i, SEG_GA // tn + j)),
                  pl.BlockSpec((tm, tn), lambda i, j: (i, SEG_GB // tn + j)),
                  pl.BlockSpec((1, GDN_DV), lambda i, j: (0, 0)),
                  pl.BlockSpec((MLA_HEADS * V_HEAD, tn), lambda i, j: (0, j)),
                  pl.BlockSpec((GDN_HK, tn), lambda i, j: (0, j))],
        out_specs=pl.BlockSpec((tm, tn), lambda i, j: (i, j)),
        out_shape=jax.ShapeDtypeStruct((n, D_MODEL), BF16),
        scratch_shapes=[pltpu.VMEM((tm, GDN_HK), BF16)],
        compiler_params=_cparams(("arbitrary", "arbitrary")),
        name="gated_merge",
    )(o_g, proj, oa, proj, proj, gdn_norm, w_pa_b, w_pb_b)


def _pack_bf16_pairs(x):
    w = x.shape[1] // 2
    bits = lax.bitcast_convert_type(x.astype(F32), jnp.uint32)
    return (bits[:, :w] >> 16) | bits[:, w:]


def _unpack_bf16_pairs(p):
    lo = lax.bitcast_convert_type(p << 16, F32)
    hi = lax.bitcast_convert_type(p & jnp.uint32(0xFFFF0000), F32)
    return jnp.concatenate([lo, hi], axis=1).astype(BF16)


ROUTE_E1, ROUTE_E2, ROUTE_W1, ROUTE_W2 = 0, 1, 2, 3


def _first_lane_of_max(x, lane):
    m = jnp.max(x, axis=-1, keepdims=True)
    return m, jnp.min(jnp.where(x == m, lane, LANES), axis=-1, keepdims=True)


def _route(logits, lane):
    neg = -jnp.inf
    is_g = lane < N_GROUPS
    gmax, grp = _first_lane_of_max(jnp.where(is_g, logits, neg), lane)
    p_grp = 1.0 / jnp.sum(jnp.where(is_g, jnp.exp(logits - gmax), 0.0), axis=-1, keepdims=True)
    lo = N_GROUPS + grp * EXP_PER_GROUP
    in_grp = (lane >= lo) & (lane < lo + EXP_PER_GROUP)
    emax = jnp.max(jnp.where(in_grp, logits, neg), axis=-1, keepdims=True)
    ee = jnp.where(in_grp, jnp.exp(logits - emax), 0.0)
    soft = jnp.where(in_grp, ee / jnp.sum(ee, axis=-1, keepdims=True), -1.0)
    w1, i1 = _first_lane_of_max(soft, lane)
    w2, i2 = _first_lane_of_max(jnp.where(lane == i1, -1.0, soft), lane)
    wsum = w1 + w2
    tile = jnp.where(lane == ROUTE_E1, (i1 - N_GROUPS).astype(F32), 0.0)
    tile = jnp.where(lane == ROUTE_E2, (i2 - N_GROUPS).astype(F32), tile)
    tile = jnp.where(lane == ROUTE_W1, w1 / wsum * p_grp, tile)
    return jnp.where(lane == ROUTE_W2, w2 / wsum * p_grp, tile)


def _out_proj_kernel(mp_ref, xp_ref, ms_ref, xs_ref, wo_ref, gf_ref, wrt_ref, brt_ref, h_ref, hn_ref, rt_ref,
                     *, np_tiles):
    i = pl.program_id(0)

    def rows(m_ref, x_ref, n):
        h = x_ref[...] + jnp.dot(m_ref[...], wo_ref[...], preferred_element_type=F32)
        h_ref[0:n, :] = h
        hnb = _rms_rows(h, gf_ref[...]).astype(BF16)
        hn_ref[0:n, :] = _pack_bf16_pairs(hnb)
        logits = jnp.dot(hnb, wrt_ref[...], preferred_element_type=F32) + brt_ref[...]
        rt_ref[0:n, :] = _route(logits, lax.broadcasted_iota(jnp.int32, logits.shape, 1))

    @pl.when(i < np_tiles)
    def _():
        rows(mp_ref, xp_ref, mp_ref.shape[0])

    @pl.when(i == np_tiles)
    def _():
        rows(ms_ref, xs_ref, ms_ref.shape[0])


def _out_proj(m_p, x_p, m_s, x_s, w_o_b, norm_ffn, w_rt_b, b_rt, *, tm):
    n_p, n_s = m_p.shape[0], m_s.shape[0]
    tm = _tile(n_p, tm, 16)
    np_tiles = n_p // tm
    assert n_s <= tm
    const = lambda a: pl.BlockSpec(a.shape, lambda i: (0,) * a.ndim)
    p_spec = pl.BlockSpec((tm, D_MODEL), lambda i: (jnp.minimum(i, np_tiles - 1), 0))
    out = lambda w: pl.BlockSpec((tm, w), lambda i: (i, 0))
    return pl.pallas_call(
        functools.partial(_out_proj_kernel, np_tiles=np_tiles),
        grid=(np_tiles + 1,),
        in_specs=[p_spec, p_spec, const(m_s), const(x_s), const(w_o_b), const(norm_ffn), const(w_rt_b),
                  const(b_rt)],
        out_specs=[out(D_MODEL), out(D_MODEL // 2), out(LANES)],
        out_shape=[jax.ShapeDtypeStruct((n_p + n_s, D_MODEL), F32),
                   jax.ShapeDtypeStruct((n_p + n_s, D_MODEL // 2), jnp.uint32),
                   jax.ShapeDtypeStruct((n_p + n_s, LANES), F32)],
        compiler_params=_cparams(("arbitrary",)),
        name="out_proj_router",
    )(m_p, x_p, m_s, x_s, w_o_b, norm_ffn, w_rt_b, b_rt)


def _expert_weights(sched_ref, w_hbm, wbuf, wb_ref, sem):
    i = pl.program_id(0)

    part = wbuf.shape[1] // WEIGHT_DMA_PARTS

    def fetch(expert, slot):
        return [pltpu.make_async_copy(w_hbm.at[expert, pl.ds(p * part, part)],
                                      wbuf.at[slot, pl.ds(p * part, part)], sem.at[slot, p])
                for p in range(WEIGHT_DMA_PARTS)]

    def start(expert, slot):
        for p, cp in enumerate(fetch(expert, slot)):
            cp.start(priority=p)

    @pl.when(i == 0)
    def _():
        start(sched_ref[0, 0], 0)

        @pl.when(sched_ref[5, 0] >= 0)
        def _():
            start(sched_ref[5, 0], 1)

    @pl.when(sched_ref[2, i] != 0)
    def _():
        slot = sched_ref[3, i]
        for cp in fetch(sched_ref[0, i], slot):
            cp.wait()

        @pl.when(sched_ref[4, i] >= 0)
        def _():
            start(sched_ref[4, i], (slot + WEIGHT_SLOTS - 1) % WEIGHT_SLOTS)

        wb_ref[...] = wbuf[slot].astype(BF16)


def _gmm_kernel(sched_ref, x_ref, w_hbm, o_ref, wbuf, wb_ref, sem):
    i = pl.program_id(0)
    _expert_weights(sched_ref, w_hbm, wbuf, wb_ref, sem)

    @pl.when(sched_ref[1, i] != 0)
    def _():
        o_ref[...] = jnp.dot(_unpack_bf16_pairs(x_ref[...]), wb_ref[...], preferred_element_type=F32)

    @pl.when(sched_ref[1, i] == 0)
    def _():
        o_ref[...] = jnp.zeros(o_ref.shape, o_ref.dtype)


def _gmm_down_kernel(sched_ref, g_ref, u_ref, w_hbm, acc_hbm, o_ref, wbuf, wb_ref, sem):
    del acc_hbm
    i = pl.program_id(0)
    _expert_weights(sched_ref, w_hbm, wbuf, wb_ref, sem)

    @pl.when(sched_ref[1, i] != 0)
    def _():
        gate = g_ref[...]
        act = gate * _sigmoid(gate) * u_ref[...]
        o_ref[...] = jnp.dot(act.astype(BF16), wb_ref[...], preferred_element_type=F32)

    @pl.when(sched_ref[1, i] == 0)
    def _():
        o_ref[...] = jnp.zeros(o_ref.shape, o_ref.dtype)


def _expert_schedule(bexp, used):
    nb = bexp.shape[0]
    first = jnp.concatenate([jnp.ones((1,), jnp.int32), (bexp[1:] != bexp[:-1]).astype(jnp.int32)])
    run = jnp.cumsum(first) - 1
    run_expert = jnp.zeros((nb,), jnp.int32).at[run].set(bexp)
    ahead = WEIGHT_SLOTS - 1
    nxt = jnp.where(run + ahead <= run[-1], run_expert[jnp.minimum(run + ahead, nb - 1)], -1)
    second = jnp.broadcast_to(jnp.where(run[-1] >= 1, run_expert[jnp.minimum(1, nb - 1)], -1), (nb,))
    return jnp.stack([bexp, used, first, run % WEIGHT_SLOTS, nxt, second]).astype(jnp.int32)


def _grouped_call(kern, sched, row_inputs, w, *, name, out_into=None, block_offset=0):
    rows = row_inputs[0].shape[0]
    k, n = w.shape[1], w.shape[2]
    nb = rows // MOE_ROWS
    in_specs = ([pl.BlockSpec((MOE_ROWS, a.shape[1]), lambda i, sc: (i, 0)) for a in row_inputs]
                + [pl.BlockSpec(memory_space=pl.ANY)])
    args = [sched, *row_inputs, w]
    aliases = {}
    if out_into is not None:
        in_specs.append(pl.BlockSpec(memory_space=pl.ANY))
        aliases = {len(args): 0}
        args.append(out_into)
        rows = out_into.shape[0]
    grid_spec = pltpu.PrefetchScalarGridSpec(
        num_scalar_prefetch=1,
        grid=(nb,),
        in_specs=in_specs,
        out_specs=pl.BlockSpec((MOE_ROWS, n), lambda i, sc: (i + block_offset, 0)),
        scratch_shapes=[pltpu.VMEM((WEIGHT_SLOTS, k, n), F32), pltpu.VMEM((k, n), BF16),
                        pltpu.SemaphoreType.DMA((WEIGHT_SLOTS, WEIGHT_DMA_PARTS))])
    return pl.pallas_call(
        kern, grid_spec=grid_spec,
        out_shape=jax.ShapeDtypeStruct((rows, n), F32),
        input_output_aliases=aliases,
        compiler_params=_cparams(("arbitrary",)),
        name=name,
    )(*args)


def _rope_tables(pos):
    inv = ROPE_THETA ** (-jnp.arange(0, QK_ROPE, 2, dtype=F32) / QK_ROPE)
    ang = pos[:, None] * inv[None, :]
    cos, sin = jnp.cos(ang), jnp.sin(ang)
    return jnp.concatenate([cos, cos, cos, cos], axis=1), jnp.concatenate([-sin, sin, -sin, sin], axis=1)


def _w_in_segments():
    segs = []
    for h in range(MLA_HEADS):
        segs.append((h * QK_HEAD, QK_NOPE, SEG_QN + h * QK_NOPE))
        segs.append((h * QK_HEAD + QK_NOPE, QK_ROPE, SEG_QR + h * QK_ROPE))
    o = MLA_HEADS * QK_HEAD
    segs.append((o, KV_LORA, SEG_CKV)); o += KV_LORA
    segs.append((o, QK_ROPE, SEG_MISC)); o += QK_ROPE
    segs.append((o, GDN_QKV_W, SEG_QKV)); o += GDN_QKV_W
    segs.append((o, GDN_HEADS, SEG_MISC + MISC_A)); o += GDN_HEADS
    segs.append((o, GDN_HEADS, SEG_MISC + MISC_B)); o += GDN_HEADS
    segs.append((o, GDN_HEADS * GDN_DV, SEG_Z)); o += GDN_HEADS * GDN_DV
    segs.append((o, D_MODEL, SEG_GA)); o += D_MODEL
    segs.append((o, D_MODEL, SEG_GB))
    return segs


def _reorder_kernel(w_ref, o_ref):
    for src, width, dst in _w_in_segments():
        o_ref[:, dst:dst + width] = w_ref[:, src:src + width].astype(BF16)
    tail = SEG_MISC + MISC_B + GDN_HEADS
    o_ref[:, tail:D_PROJ] = jnp.zeros((o_ref.shape[0], D_PROJ - tail), BF16)


def _reorder_w_in_pallas(w_in):
    d, d_in = w_in.shape
    tr = 128
    return pl.pallas_call(
        _reorder_kernel,
        grid=(d // tr,),
        in_specs=[pl.BlockSpec((tr, d_in), lambda i: (i, 0))],
        out_specs=pl.BlockSpec((tr, D_PROJ), lambda i: (i, 0)),
        out_shape=jax.ShapeDtypeStruct((d, D_PROJ), BF16),
        compiler_params=_cparams(("arbitrary",)),
        name="w_in_reorder",
    )(w_in)


def _lane_vec(v, offset):
    return jnp.zeros((1, LANES), F32).at[0, offset:offset + v.shape[0]].set(v.astype(F32))


def _hier_moe(h, hn, route, w_gate, w_up, w_down, *, n_first):
    t = h.shape[0]
    expert = route[:, ROUTE_E1:ROUTE_E2 + 1].astype(jnp.int32)
    top_w = route[:, ROUTE_W1:ROUTE_W2 + 1]
    n_assign = t * TOP_K
    e_flat = expert.reshape(n_assign)
    onehot = (e_flat[:, None] == jnp.arange(N_EXPERTS, dtype=jnp.int32)[None, :]).astype(jnp.int32)
    csum = jnp.cumsum(onehot, axis=0)
    rank = jnp.take_along_axis(csum, e_flat[:, None], axis=1)[:, 0] - 1
    counts = csum[-1]
    pcounts = (counts + MOE_ROWS - 1) // MOE_ROWS * MOE_ROWS
    pends = jnp.cumsum(pcounts)
    pstarts = pends - pcounts
    dest = pstarts[e_flat] + rank
    n_blocks = -(-(-(-n_assign // MOE_ROWS) + N_EXPERTS) // MOE_SPLITS) * MOE_SPLITS
    n_rows = n_blocks * MOE_ROWS
    tok = jnp.arange(n_assign, dtype=jnp.int32) // TOP_K
    src = jnp.zeros((n_rows,), jnp.int32).at[dest].set(tok)
    bstart = jnp.arange(n_blocks, dtype=jnp.int32) * MOE_ROWS
    bexp = jnp.minimum(jnp.searchsorted(pends, bstart, side='right'), N_EXPERTS - 1).astype(jnp.int32)
    used = (bstart < pends[-1]).astype(jnp.int32)
    bq = n_blocks // MOE_SPLITS
    yb = jnp.zeros((n_rows, D_MODEL), F32)
    for q in range(MOE_SPLITS):
        xb = hn[src[q * bq * MOE_ROWS:(q + 1) * bq * MOE_ROWS]]
        sched = _expert_schedule(bexp[q * bq:(q + 1) * bq], used[q * bq:(q + 1) * bq])
        gate = _grouped_call(_gmm_kernel, sched, [xb], w_gate, name="moe_gate")
        up = _grouped_call(_gmm_kernel, sched, [xb], w_up, name="moe_up")
        yb = _grouped_call(_gmm_down_kernel, sched, [gate, up], w_down, name="moe_down",
                           out_into=yb, block_offset=q * bq)
    dest2 = dest.reshape(t, TOP_K)
    picked = [yb[dest2[:, k]] for k in range(TOP_K)]

    def combine(rows):
        y = h[rows]
        for k in range(TOP_K):
            y = y + picked[k][rows] * top_w[rows, k:k + 1]
        return y

    return combine(slice(0, n_first)), combine(slice(n_first, t))


def kernel(x_prompt, x_sample, cache_ckv, cache_krope, state_ssm, state_conv, page_table, norm_attn, w_in, norm_ckv, w_uk, w_uv, q_gain_nope, q_gain_rope, k_gain_nope, k_gain_rope, conv_w, gdn_a_log, gdn_dt_bias, gdn_norm, w_pa, w_pb, w_o, norm_ffn, w_router_group, b_router_group, w_router_expert, b_router_expert, w_gate, w_up, w_down):
    depth = w_in.shape[0]
    bsz, seq, _ = x_prompt.shape
    n_s, s_new, _ = x_sample.shape
    assert s_new == 1
    n_p = bsz * seq
    n_all = n_p + n_s
    n_past = page_table.shape[1] * PAGE_SIZE
    cos_p, sin_p = _rope_tables(jnp.tile(jnp.arange(seq, dtype=F32), bsz))
    cos_s, sin_s = _rope_tables(jnp.full((n_s,), n_past, F32))
    tq = _tile(seq, 512, LANES)
    xp = x_prompt.reshape(n_p, D_MODEL)
    xs = x_sample.reshape(n_s, D_MODEL)
    outs = [[] for _ in range(8)]
    row = lambda v: v.astype(F32)[None, :]
    for l in range(depth):
        w_in_r = _reorder_w_in_pallas(w_in[l])
        w_uk_b = w_uk[l].astype(BF16)
        w_uvt_b = w_uv[l].T.astype(BF16)
        wukt = w_uk_b.T
        w_pa_b, w_pb_b, w_o_b = w_pa[l].astype(BF16), w_pb[l].astype(BF16), w_o[l].astype(BF16)
        w_rt_b = jnp.concatenate([w_router_group[l], w_router_expert[l],
                                  jnp.zeros((D_MODEL, LANES - N_GROUPS - N_EXPERTS), F32)], axis=1).astype(BF16)
        gains = (row(q_gain_nope[l]), jnp.tile(row(q_gain_rope[l]), (1, 2)), row(k_gain_nope[l]),
                 jnp.tile(row(k_gain_rope[l]), (1, 2)), row(norm_ckv[l]))
        alog_l = _lane_vec(gdn_a_log[l], MISC_A)
        dtb_l = _lane_vec(gdn_dt_bias[l], MISC_A)
        proj_p = _in_proj(xp, row(norm_attn[l]), w_in_r, tm=1024)
        proj_s = _in_proj(xs, row(norm_attn[l]), w_in_r, tm=n_s)

        qcat_p, kcat_p, c_p, kr_p, vt_p = _mla_prep(proj_p, cos_p, sin_p, *gains, w_uk_b, w_uvt_b,
                                                    tm=tq, with_vt=True, with_qg=False)
        oa_p = _flash_attention(qcat_p, kcat_p, vt_p, batch=bsz, seq=seq, tq=tq)

        qcat_s, kcat_s, c_s, kr_s, qg_s = _mla_prep(proj_s, cos_s, sin_s, *gains, w_uk_b, w_uvt_b,
                                                    tm=n_s, with_vt=False, with_qg=True)
        qa = _headwise_matmul(qg_s, wukt, heads=MLA_HEADS, b_rows_by_head=True, name="q_absorb")
        o_lat = _paged_attention(page_table, cache_ckv[l], jnp.swapaxes(cache_krope[l], 1, 2), wukt,
                                 qa.astype(BF16).reshape(n_s, MLA_HEADS, KV_LORA),
                                 qcat_s.reshape(n_s, MLA_HEADS, QK_PAD), kcat_s.reshape(n_s, MLA_HEADS, QK_PAD),
                                 c_s[:, None, :])
        oa_s = _headwise_matmul(o_lat.astype(BF16).reshape(n_s, MLA_HEADS * KV_LORA), w_uv[l].astype(BF16),
                                heads=MLA_HEADS, b_rows_by_head=False, name="v_absorb").astype(BF16)

        qkv_p, kt_p, gates_p = _gdn_prep(proj_p, conv_w[l], alog_l, dtb_l, batch=bsz, seq=seq, tm=512)
        o_g, ssm_p = _delta_rule(qkv_p, kt_p, gates_p, batch=bsz, seq=seq)
        conv_p = proj_p.reshape(bsz, seq, D_PROJ)[:, seq - (CONV_W - 1):, SEG_QKV:SEG_QKV + GDN_QKV_W]

        o_s, ssm_s = _gdn_step(proj_s, state_conv[l], state_ssm[l], conv_w[l], alog_l, dtb_l)
        conv_s = jnp.concatenate([state_conv[l][:, 1:], proj_s[:, None, SEG_QKV:SEG_QKV + GDN_QKV_W]], axis=1)

        m_p = _merge(o_g, oa_p, proj_p, row(gdn_norm[l]), w_pa_b, w_pb_b, tm=512)
        m_s = _merge(o_s, oa_s, proj_s, row(gdn_norm[l]), w_pa_b, w_pb_b, tm=n_s)
        b_rt = _lane_vec(jnp.concatenate([b_router_group[l], b_router_expert[l]]), 0)
        h_all, hn_all, route = _out_proj(m_p, xp, m_s, xs, w_o_b, row(norm_ffn[l]), w_rt_b, b_rt, tm=512)
        xp, xs = _hier_moe(h_all, hn_all, route, w_gate[l], w_up[l], w_down[l], n_first=n_p)

        for lst, val in zip(outs, (c_p.reshape(bsz, seq, KV_LORA), kr_p[:, :QK_ROPE].reshape(bsz, seq, QK_ROPE),
                                   c_s.reshape(n_s, 1, KV_LORA), kr_s[:, :QK_ROPE].reshape(n_s, 1, QK_ROPE),
                                   ssm_p, conv_p, ssm_s, conv_s)):
            lst.append(val)
    return (xp.reshape(bsz, seq, D_MODEL), xs.reshape(n_s, 1, D_MODEL)) + tuple(jnp.stack(o) for o in outs)
```
